```python
import math
import jax, jax.numpy as jnp
from jax import lax
import numpy as np

D_MODEL = 1024
BATCH = 16
SEQ = 2048
DEPTH = 1

D_MIX = D_MODEL
D_SSM = D_MIX // 2
D_DN = D_MIX - D_SSM
SSM_GROUP = 16
SSM_GROUPS = D_SSM // SSM_GROUP
SSM_STATE = 64
DN_HEAD_DIM = 128
DN_HEADS = D_DN // DN_HEAD_DIM
DN_CONV = 4
DN_CHUNK = 64
PLE_DIM = 256
NORM_EPS = 1e-6

SPLIT_1 = D_SSM
SPLIT_2 = SPLIT_1 + D_SSM
SPLIT_3 = SPLIT_2 + D_DN
SPLIT_4 = SPLIT_3 + D_DN
SPLIT_5 = SPLIT_4 + D_DN
SPLIT_6 = SPLIT_5 + D_DN
SPLIT_7 = SPLIT_6 + DN_HEADS
D_IN_PROJ = SPLIT_7 + DN_HEADS

kernel_name = "hymba_s5_gated_deltanet_block"


def rms_norm(x, g):
    xf = x.astype(jnp.float32)
    y = xf * lax.rsqrt(jnp.mean(xf * xf, axis=-1, keepdims=True) + NORM_EPS)
    return (y * g.astype(jnp.float32)).astype(x.dtype)


def l2_normalize(x):
    return x * lax.rsqrt(jnp.sum(x * x, axis=-1, keepdims=True) + NORM_EPS)


def _complex_linear_combine(earlier, later):
    a1r, a1i, b1r, b1i = earlier
    a2r, a2i, b2r, b2i = later
    ar = a2r * a1r - a2i * a1i
    ai = a2r * a1i + a2i * a1r
    br = a2r * b1r - a2i * b1i + b2r
    bi = a2r * b1i + a2i * b1r + b2i
    return (ar, ai, br, bi)


def s5_branch(u, A_re, A_im, B_re, B_im, C_re, C_im, D, log_dt, w_glu, b_glu):
    f32 = jnp.float32
    bsz, L, _ = u.shape
    uf = u.astype(f32)
    ug = uf.reshape(bsz, L, SSM_GROUPS, SSM_GROUP)
    dt = jnp.exp(log_dt.astype(f32))[:, None]
    lr = A_re.astype(f32)
    li = A_im.astype(f32)
    mag = jnp.exp(lr * dt)
    ang = li * dt
    ab_r = mag * jnp.cos(ang)
    ab_i = mag * jnp.sin(ang)
    den = lr * lr + li * li
    nr = ab_r - 1.0
    ni = ab_i
    cr = (nr * lr + ni * li) / den
    ci = (ni * lr - nr * li) / den
    Br = B_re.astype(f32)
    Bi = B_im.astype(f32)
    bb_r = cr[..., None] * Br - ci[..., None] * Bi
    bb_i = cr[..., None] * Bi + ci[..., None] * Br
    bu_r = jnp.einsum('gph,blgh->blgp', bb_r, ug)
    bu_i = jnp.einsum('gph,blgh->blgp', bb_i, ug)
    a_r = jnp.broadcast_to(ab_r, (1, L) + ab_r.shape)
    a_i = jnp.broadcast_to(ab_i, (1, L) + ab_i.shape)
    _, _, s_r, s_i = lax.associative_scan(_complex_linear_combine, (a_r, a_i, bu_r, bu_i), axis=1)
    y = (jnp.einsum('ghp,blgp->blgh', C_re.astype(f32), s_r)
         - jnp.einsum('ghp,blgp->blgh', C_im.astype(f32), s_i))
    y = y.reshape(bsz, L, D_SSM) + D.astype(f32) * uf
    y = jax.nn.gelu(y)
    y = y * jax.nn.sigmoid(y @ w_glu.astype(f32) + b_glu.astype(f32))
    return y


def causal_depthwise_conv(x, w):
    K, C = w.shape
    return lax.conv_general_dilated(x, w[:, None, :], window_strides=(1,), padding=((K - 1, 0),),
                                    dimension_numbers=('NWC', 'WIO', 'NWC'), feature_group_count=C)


def chunk_gated_delta_rule(q, k, v, g, beta):
    f32 = jnp.float32
    bsz, L, H, Dk = q.shape
    Dv = v.shape[-1]
    C = DN_CHUNK
    N = L // C

    def to_chunks(t):
        return t.reshape(bsz, N, C, H, -1).transpose(0, 1, 3, 2, 4)

    q, k, v = to_chunks(q), to_chunks(k), to_chunks(v)
    g = g.reshape(bsz, N, C, H).transpose(0, 1, 3, 2)
    beta = beta.reshape(bsz, N, C, H).transpose(0, 1, 3, 2)
    g = jnp.cumsum(g, axis=-1)
    causal = jnp.tril(jnp.ones((C, C), dtype=bool))
    strict = jnp.tril(jnp.ones((C, C), dtype=bool), k=-1)
    diff = g[..., :, None] - g[..., None, :]
    decay = jnp.where(causal, jnp.exp(jnp.where(causal, diff, 0.0)), 0.0)
    k_beta = k * beta[..., None]
    v_beta = v * beta[..., None]
    lower = jnp.where(strict, jnp.einsum('bnhid,bnhjd->bnhij', k_beta, k) * decay, 0.0)
    tri = lower + jnp.eye(C, dtype=f32)
    rhs = jnp.concatenate([v_beta, k_beta * jnp.exp(g)[..., None]], axis=-1)
    sol = lax.linalg.triangular_solve(tri, rhs, left_side=True, lower=True, unit_diagonal=True)
    u = sol[..., :Dv]
    w = sol[..., Dv:]
    attn_intra = jnp.where(causal, jnp.einsum('bnhid,bnhjd->bnhij', q, k) * decay, 0.0)

    def step(S, inp):
        q_c, k_c, u_c, w_c, g_c, a_c = inp
        v_new = u_c - jnp.einsum('bhck,bhkv->bhcv', w_c, S)
        o = (jnp.einsum('bhck,bhkv->bhcv', q_c * jnp.exp(g_c)[..., None], S)
             + jnp.einsum('bhij,bhjv->bhiv', a_c, v_new))
        g_last = g_c[..., -1]
        k_dec = k_c * jnp.exp(g_last[..., None] - g_c)[..., None]
        S = S * jnp.exp(g_last)[..., None, None] + jnp.einsum('bhck,bhcv->bhkv', k_dec, v_new)
        return S, o

    xs = tuple(jnp.moveaxis(t, 1, 0) for t in (q, k, u, w, g, attn_intra))
    S0 = jnp.zeros((bsz, H, Dk, Dv), dtype=f32)
    _, o = lax.scan(step, S0, xs)
    o = jnp.moveaxis(o, 0, 1).transpose(0, 1, 3, 2, 4).reshape(bsz, L, H, Dv)
    return o


def gated_deltanet_branch(q, k, v, z, b_raw, a_raw, conv_w, A_log, dt_bias, norm_g):
    f32 = jnp.float32
    bsz, L, _ = q.shape
    qkv = jnp.concatenate([q, k, v], axis=-1).astype(f32)
    qkv = jax.nn.silu(causal_depthwise_conv(qkv, conv_w.astype(f32)))
    q, k, v = jnp.split(qkv, [D_DN, 2 * D_DN], axis=-1)
    q = l2_normalize(q.reshape(bsz, L, DN_HEADS, DN_HEAD_DIM)) * (DN_HEAD_DIM ** -0.5)
    k = l2_normalize(k.reshape(bsz, L, DN_HEADS, DN_HEAD_DIM))
    v = v.reshape(bsz, L, DN_HEADS, DN_HEAD_DIM)
    beta = jax.nn.sigmoid(b_raw.astype(f32))
    g = -jnp.exp(A_log.astype(f32)) * jax.nn.softplus(a_raw.astype(f32) + dt_bias.astype(f32))
    o = chunk_gated_delta_rule(q, k, v, g, beta)
    zf = z.astype(f32).reshape(bsz, L, DN_HEADS, DN_HEAD_DIM)
    o = rms_norm(o, norm_g) * jax.nn.silu(zf)
    return o.reshape(bsz, L, D_DN)


def setup_inputs(seed: int = 0) -> dict:
    key = jax.random.key(seed)
    ks = jax.random.split(key, 24)
    f32 = jnp.float32
    nrm = lambda k, s, sc: jax.random.normal(k, s, f32) * sc
    x = jax.random.normal(ks[0], (BATCH, SEQ, D_MODEL), f32)
    p = jax.random.normal(ks[1], (DEPTH, BATCH, SEQ, PLE_DIM), f32)
    norm_mix_g = 1.0 + nrm(ks[2], (DEPTH, D_MODEL), 0.02)
    w_in = nrm(ks[3], (DEPTH, D_MODEL, D_IN_PROJ), D_MODEL ** -0.5)
    n_idx = jnp.arange(SSM_STATE, dtype=f32)
    ssm_A_re = -0.5 + nrm(ks[4], (DEPTH, SSM_GROUPS, SSM_STATE), 0.01)
    ssm_A_im = math.pi * n_idx + nrm(ks[5], (DEPTH, SSM_GROUPS, SSM_STATE), 0.01)
    ssm_B_re = nrm(ks[6], (DEPTH, SSM_GROUPS, SSM_STATE, SSM_GROUP), (2 * SSM_GROUP) ** -0.5)
    ssm_B_im = nrm(ks[7], (DEPTH, SSM_GROUPS, SSM_STATE, SSM_GROUP), (2 * SSM_GROUP) ** -0.5)
    ssm_C_re = nrm(ks[8], (DEPTH, SSM_GROUPS, SSM_GROUP, SSM_STATE), SSM_STATE ** -0.5)
    ssm_C_im = nrm(ks[9], (DEPTH, SSM_GROUPS, SSM_GROUP, SSM_STATE), SSM_STATE ** -0.5)
    ssm_D = nrm(ks[10], (DEPTH, D_SSM), 1.0)
    ssm_log_dt = jax.random.uniform(ks[11], (DEPTH, SSM_GROUPS), f32, math.log(1e-3), math.log(1e-1))
    ssm_w_glu = nrm(ks[12], (DEPTH, D_SSM, D_SSM), D_SSM ** -0.5)
    ssm_b_glu = nrm(ks[13], (DEPTH, D_SSM), 0.01)
    dn_conv_w = nrm(ks[14], (DEPTH, DN_CONV, 3 * D_DN), DN_CONV ** -0.5)
    dn_A_log = jnp.log(jax.random.uniform(ks[15], (DEPTH, DN_HEADS), f32, 1.0, 16.0))
    dt0 = jnp.exp(jax.random.uniform(ks[16], (DEPTH, DN_HEADS), f32, math.log(1e-3), math.log(1e-1)))
    dn_dt_bias = dt0 + jnp.log(-jnp.expm1(-dt0))
    dn_norm_g = 1.0 + nrm(ks[17], (DEPTH, DN_HEAD_DIM), 0.02)
    w_out = nrm(ks[18], (DEPTH, D_MIX, D_MODEL), D_MIX ** -0.5)
    w_ple_proj = nrm(ks[19], (DEPTH, PLE_DIM, D_MODEL), PLE_DIM ** -0.5)
    ple_norm_g = 1.0 + nrm(ks[20], (DEPTH, D_MODEL), 0.02)
    w_ple_gate = nrm(ks[21], (DEPTH, D_MODEL, D_MODEL), D_MODEL ** -0.5)
    final_norm_g = 1.0 + nrm(ks[22], (D_MODEL,), 0.02)
    return {"x": x, "p": p, "norm_mix_g": norm_mix_g, "w_in": w_in,
            "ssm_A_re": ssm_A_re, "ssm_A_im": ssm_A_im, "ssm_B_re": ssm_B_re, "ssm_B_im": ssm_B_im,
            "ssm_C_re": ssm_C_re, "ssm_C_im": ssm_C_im, "ssm_D": ssm_D, "ssm_log_dt": ssm_log_dt,
            "ssm_w_glu": ssm_w_glu, "ssm_b_glu": ssm_b_glu,
            "dn_conv_w": dn_conv_w, "dn_A_log": dn_A_log, "dn_dt_bias": dn_dt_bias, "dn_norm_g": dn_norm_g,
            "w_out": w_out, "w_ple_proj": w_ple_proj, "ple_norm_g": ple_norm_g, "w_ple_gate": w_ple_gate,
            "final_norm_g": final_norm_g}


def reference(x, p, norm_mix_g, w_in, ssm_A_re, ssm_A_im, ssm_B_re, ssm_B_im, ssm_C_re, ssm_C_im,
              ssm_D, ssm_log_dt, ssm_w_glu, ssm_b_glu, dn_conv_w, dn_A_log, dn_dt_bias, dn_norm_g,
              w_out, w_ple_proj, ple_norm_g, w_ple_gate, final_norm_g):
    h = x
    for i in range(DEPTH):
        a = rms_norm(h, norm_mix_g[i])
        proj = a @ w_in[i]
        u_s, z_s, q, k, v, z_d, b_raw, a_raw = jnp.split(
            proj, [SPLIT_1, SPLIT_2, SPLIT_3, SPLIT_4, SPLIT_5, SPLIT_6, SPLIT_7], axis=-1)
        y_s = s5_branch(u_s, ssm_A_re[i], ssm_A_im[i], ssm_B_re[i], ssm_B_im[i], ssm_C_re[i],
                        ssm_C_im[i], ssm_D[i], ssm_log_dt[i], ssm_w_glu[i], ssm_b_glu[i])
        y_s = y_s * jax.nn.silu(z_s.astype(jnp.float32))
        y_d = gated_deltanet_branch(q, k, v, z_d, b_raw, a_raw, dn_conv_w[i], dn_A_log[i],
                                    dn_dt_bias[i], dn_norm_g[i])
        mix = jnp.concatenate([y_s, y_d], axis=-1).astype(h.dtype) @ w_out[i]
        h = h + mix
        e = rms_norm(p[i] @ w_ple_proj[i], ple_norm_g[i])
        h = h + jax.nn.sigmoid(h @ w_ple_gate[i]) * e
    return rms_norm(h, final_norm_g)
```

```python
import functools

import jax
import jax.numpy as jnp
from jax import lax
from jax.experimental import pallas as pl
from jax.experimental.pallas import tpu as pltpu

F32 = jnp.float32
BF16 = jnp.bfloat16

NORM_EPS = 1e-6
SSM_GROUP = 16
SSM_STATE = 64
DN_HEAD_DIM = 128
DN_CONV = 4
DN_CHUNK = 64

LANES = 128
SSM_LANE_BLOCK = 128
SSM_STATE_BLOCK = (SSM_LANE_BLOCK // SSM_GROUP) * SSM_STATE

PROJ_TILE = 512
S5_TIME_TILE = 32
GDN_TILE = 256
VMEM_LIMIT = 48 * 1024 * 1024


def _dot(a, b):
    return jnp.dot(a.astype(BF16), b.astype(BF16), preferred_element_type=F32)


def _dot_nt(a, b):
    return lax.dot_general(a.astype(BF16), b.astype(BF16), (((1,), (1,)), ((), ())),
                           preferred_element_type=F32)


def _rms(x, g):
    return x * lax.rsqrt(jnp.mean(x * x, axis=-1, keepdims=True) + NORM_EPS) * g


def _s5_prep_kernel(lr_ref, li_ref, ldt_ref, br_ref, bi_ref, abr_ref, abi_ref, bbr_ref, bbi_ref):
    lr = lr_ref[...]
    li = li_ref[...]
    dt = jnp.exp(ldt_ref[...])
    mag = jnp.exp(lr * dt)
    ang = li * dt
    ab_r = mag * jnp.cos(ang)
    ab_i = mag * jnp.sin(ang)
    den = lr * lr + li * li
    nr = ab_r - 1.0
    ni = ab_i
    cr = (nr * lr + ni * li) / den
    ci = (ni * lr - nr * li) / den
    abr_ref[...] = ab_r
    abi_ref[...] = ab_i
    br = br_ref[...]
    bi = bi_ref[...]
    bbr_ref[...] = cr * br - ci * bi
    bbi_ref[...] = cr * bi + ci * br


def _s5_prep(a_re, a_im, log_dt, b_re, b_im):
    g, p = a_re.shape
    h = b_re.shape[-1]
    n = g * p
    col = lambda t: t.reshape(n, 1).astype(F32)
    ldt = jnp.broadcast_to(log_dt.astype(F32)[:, None], (g, p))
    out = pl.pallas_call(
        _s5_prep_kernel,
        out_shape=(jax.ShapeDtypeStruct((n, 1), F32), jax.ShapeDtypeStruct((n, 1), F32),
                   jax.ShapeDtypeStruct((n, h), F32), jax.ShapeDtypeStruct((n, h), F32)),
        name="s5_prep",
    )(col(a_re), col(a_im), col(ldt), b_re.reshape(n, h).astype(F32), b_im.reshape(n, h).astype(F32))
    ab_r, ab_i, bb_r, bb_i = out
    return ab_r.reshape(g, p), ab_i.reshape(g, p), bb_r.reshape(g, p, h), bb_i.reshape(g, p, h)


def _s5_layout(ab_r, ab_i, bb_r, bb_i, c_re, c_im, bsz):
    g, p = ab_r.shape
    h = bb_r.shape[-1]
    gpb = SSM_LANE_BLOCK // h
    nb = g // gpb
    eye = jnp.eye(gpb, dtype=F32)

    def in_w(bb):
        t = bb.reshape(nb, gpb, p, h)
        return jnp.einsum('jgph,gk->jghkp', t, eye).reshape(nb, gpb * h, gpb * p)

    def out_w(c):
        t = c.astype(F32).reshape(nb, gpb, h, p)
        return jnp.einsum('jghp,gk->jgpkh', t, eye).reshape(nb, gpb * p, gpb * h)

    bw = jnp.concatenate([in_w(bb_r), in_w(bb_i)], axis=2).astype(BF16)
    cw = jnp.concatenate([out_w(c_re), -out_w(c_im)], axis=1).astype(BF16)
    a_r = jnp.broadcast_to(ab_r.reshape(1, g * p), (bsz, g * p))
    a_i = jnp.broadcast_to(ab_i.reshape(1, g * p), (bsz, g * p))
    return bw, cw, a_r, a_i


def _in_proj_kernel(x_ref, g_ref, wm_ref, wba_ref, us_ref, zs_ref, qkv_ref, zd_ref, ba_ref, *, d_ssm, d_dn):
    a = _rms(x_ref[0], g_ref[...]).astype(BF16)
    o = 0
    us_ref[...] = jnp.dot(a, wm_ref[:, o:o + d_ssm], preferred_element_type=F32)
    o += d_ssm
    zs_ref[...] = jnp.dot(a, wm_ref[:, o:o + d_ssm], preferred_element_type=F32)
    o += d_ssm
    qkv_ref[0] = jnp.dot(a, wm_ref[:, o:o + 3 * d_dn], preferred_element_type=F32)
    o += 3 * d_dn
    zd_ref[0] = jnp.dot(a, wm_ref[:, o:o + d_dn], preferred_element_type=F32)
    ba = jnp.dot(a, wba_ref[...], preferred_element_type=F32)
    ba_ref[0] = ba.T[:ba_ref.shape[1]]


def _in_proj(x, g, w_main, w_ba, d_ssm, d_dn):
    bsz, seq, dm = x.shape
    tm = min(PROJ_TILE, seq)
    nrow = 2 * (d_dn // DN_HEAD_DIM)
    grid = (bsz, seq // tm)
    full = lambda shape: pl.BlockSpec(shape, lambda b, l: (0,) * len(shape))
    return pl.pallas_call(
        functools.partial(_in_proj_kernel, d_ssm=d_ssm, d_dn=d_dn),
        grid=grid,
        in_specs=[pl.BlockSpec((1, tm, dm), lambda b, l: (b, l, 0)),
                  full((1, dm)), full(w_main.shape), full(w_ba.shape)],
        out_specs=[pl.BlockSpec((tm, d_ssm), lambda b, l: (l, b)),
                   pl.BlockSpec((tm, d_ssm), lambda b, l: (l, b)),
                   pl.BlockSpec((1, tm, 3 * d_dn), lambda b, l: (b, l, 0)),
                   pl.BlockSpec((1, tm, d_dn), lambda b, l: (b, l, 0)),
                   pl.BlockSpec((1, nrow, tm), lambda b, l: (b, 0, l))],
        out_shape=(jax.ShapeDtypeStruct((seq, bsz * d_ssm), F32),
                   jax.ShapeDtypeStruct((seq, bsz * d_ssm), F32),
                   jax.ShapeDtypeStruct((bsz, seq, 3 * d_dn), F32),
                   jax.ShapeDtypeStruct((bsz, seq, d_dn), F32),
                   jax.ShapeDtypeStruct((bsz, nrow, seq), F32)),
        compiler_params=pltpu.CompilerParams(dimension_semantics=("arbitrary", "arbitrary"),
                                             vmem_limit_bytes=VMEM_LIMIT),
        name="in_proj",
    )(x, g, w_main, w_ba)


def _s5_kernel(u_ref, z_ref, bw_ref, cw_ref, ar_ref, ai_ref, d_ref, wglu_ref, bglu_ref, y_ref,
               bu_ref, s_ref, carry_ref, *, tt, bsz, nb):
    sb = SSM_STATE_BLOCK
    rows = tt * bsz

    @pl.when(pl.program_id(0) == 0)
    def _():
        carry_ref[...] = jnp.zeros_like(carry_ref)

    u = u_ref[...].reshape(rows, nb * SSM_LANE_BLOCK)
    u16 = u.astype(BF16)
    for j in range(nb):
        bu_ref[:, 2 * sb * j:2 * sb * (j + 1)] = jnp.dot(
            u16[:, SSM_LANE_BLOCK * j:SSM_LANE_BLOCK * (j + 1)], bw_ref[j], preferred_element_type=F32)

    for j in range(nb):
        cr = 2 * sb * j
        ci = cr + sb
        a_r = ar_ref[:, sb * j:sb * (j + 1)]
        a_i = ai_ref[:, sb * j:sb * (j + 1)]

        def step(t, s, cr=cr, ci=ci, a_r=a_r, a_i=a_i):
            s_r, s_i = s
            r0 = pl.multiple_of(t * bsz, bsz)
            n_r = a_r * s_r - a_i * s_i + bu_ref[pl.ds(r0, bsz), cr:cr + sb]
            n_i = a_r * s_i + a_i * s_r + bu_ref[pl.ds(r0, bsz), ci:ci + sb]
            s_ref[pl.ds(r0, bsz), cr:cr + sb] = n_r.astype(BF16)
            s_ref[pl.ds(r0, bsz), ci:ci + sb] = n_i.astype(BF16)
            return n_r, n_i

        s_r, s_i = lax.fori_loop(0, tt, step, (carry_ref[:, cr:cr + sb], carry_ref[:, ci:ci + sb]), unroll=4)
        carry_ref[:, cr:cr + sb] = s_r
        carry_ref[:, ci:ci + sb] = s_i

    y = jnp.concatenate(
        [jnp.dot(s_ref[:, 2 * sb * j:2 * sb * (j + 1)], cw_ref[j], preferred_element_type=F32) for j in range(nb)],
        axis=1)
    y = y + d_ref[...] * u
    y = jax.nn.gelu(y)
    y = y * jax.nn.sigmoid(_dot(y, wglu_ref[...]) + bglu_ref[...])
    y = y * jax.nn.silu(z_ref[...].reshape(rows, nb * SSM_LANE_BLOCK))
    y_ref[...] = y.reshape(y_ref.shape)


def _s5(us_t, zs_t, bw, cw, a_r, a_i, d_vec, w_glu, b_glu):
    seq, bsz, d_ssm = us_t.shape
    nb = bw.shape[0]
    tt = min(S5_TIME_TILE, seq)
    rows = tt * bsz
    n_state = 2 * SSM_STATE_BLOCK * nb
    full = lambda shape: pl.BlockSpec(shape, lambda t: (0,) * len(shape))
    blk = pl.BlockSpec((tt, bsz, d_ssm), lambda t: (t, 0, 0))
    return pl.pallas_call(
        functools.partial(_s5_kernel, tt=tt, bsz=bsz, nb=nb),
        grid=(seq // tt,),
        in_specs=[blk, blk, full(bw.shape), full(cw.shape), full(a_r.shape), full(a_i.shape),
                  full(d_vec.shape), full(w_glu.shape), full(b_glu.shape)],
        out_specs=blk,
        out_shape=jax.ShapeDtypeStruct((seq, bsz, d_ssm), F32),
        scratch_shapes=[pltpu.VMEM((rows, n_state), F32),
                        pltpu.VMEM((rows, n_state), BF16),
                        pltpu.VMEM((bsz, n_state), F32)],
        compiler_params=pltpu.CompilerParams(dimension_semantics=("arbitrary",),
                                             vmem_limit_bytes=VMEM_LIMIT),
        name="s5",
    )(us_t, zs_t, bw, cw, a_r, a_i, d_vec, w_glu, b_glu)


def _chunk_scan(x, lane, forward):
    n = x.shape[-1]
    pos = lane % DN_CHUNK
    sh = 1
    while sh < DN_CHUNK:
        if forward:
            x = x + jnp.where(pos >= sh, pltpu.roll(x, sh, axis=1), 0.0)
        else:
            x = x + jnp.where(pos < DN_CHUNK - sh, pltpu.roll(x, n - sh, axis=1), 0.0)
        sh *= 2
    return x


def _gdn_kernel(qkv_ref, zd_ref, ba_ref, cw_ref, alog_ref, dtb_ref, ng_ref, y_ref,
                state_ref, tail_ref, xpad_ref, vnew_ref, *, tc, heads):
    hd = DN_HEAD_DIM
    d_dn = heads * hd
    nchunk = tc // DN_CHUNK

    @pl.when(pl.program_id(1) == 0)
    def _():
        state_ref[...] = jnp.zeros_like(state_ref)
        tail_ref[...] = jnp.zeros_like(tail_ref)

    x = qkv_ref[0]
    xpad_ref[0:8, :] = tail_ref[...]
    xpad_ref[8:, :] = x
    tail_ref[...] = x[tc - 8:, :]
    acc = cw_ref[DN_CONV - 1:DN_CONV, :] * x
    for s in range(1, DN_CONV):
        acc = acc + cw_ref[DN_CONV - 1 - s:DN_CONV - s, :] * xpad_ref[8 - s:8 - s + tc, :]
    c = jax.nn.silu(acc)

    ba = ba_ref[0]
    lane = lax.broadcasted_iota(jnp.int32, ba.shape, 1)
    beta_r = jax.nn.sigmoid(ba)
    g_r = -jnp.exp(alog_ref[...]) * jax.nn.softplus(ba + dtb_ref[...])
    gc_r = _chunk_scan(g_r, lane, True)
    gs_r = _chunk_scan(g_r, lane, False) - g_r
    gl_r = gc_r + gs_r
    egs_r = jnp.exp(gs_r)
    pad = jnp.zeros((LANES - 3 * 2 * heads, tc), F32)
    cols = jnp.concatenate([beta_r, gc_r, jnp.exp(gl_r), pad], axis=0).T

    ri = lax.broadcasted_iota(jnp.int32, (tc, tc), 0)
    ci = lax.broadcasted_iota(jnp.int32, (tc, tc), 1)
    same = (ri // DN_CHUNK) == (ci // DN_CHUNK)
    causal = same & (ri >= ci)
    strict = same & (ri > ci)
    eye = (ri == ci).astype(F32)
    lane_c = lax.broadcasted_iota(jnp.int32, (1, tc), 1) // DN_CHUNK

    for h in range(heads):
        q = c[:, hd * h:hd * (h + 1)]
        k = c[:, d_dn + hd * h:d_dn + hd * (h + 1)]
        v = c[:, 2 * d_dn + hd * h:2 * d_dn + hd * (h + 1)]
        q = q * lax.rsqrt(jnp.sum(q * q, axis=-1, keepdims=True) + NORM_EPS) * (hd ** -0.5)
        k = k * lax.rsqrt(jnp.sum(k * k, axis=-1, keepdims=True) + NORM_EPS)
        beta_c = cols[:, h:h + 1]
        gc_c = cols[:, 3 * heads + h:3 * heads + h + 1]
        egl_c = cols[:, 5 * heads + h:5 * heads + h + 1]
        gc_row = gc_r[heads + h:heads + h + 1, :]
        egs_row = egs_r[heads + h:heads + h + 1, :]
        eg_c = jnp.exp(gc_c)
        kb = k * beta_c
        vb = v * beta_c

        decay = jnp.exp(jnp.where(causal, gc_c - gc_row, 0.0))
        qk_kk = _dot_nt(jnp.concatenate([q, kb], axis=0), k)
        attn = jnp.where(causal, qk_kk[:tc] * decay, 0.0)
        n_mat = jnp.where(strict, -(qk_kk[tc:] * decay), 0.0)
        t_mat = eye + n_mat
        m = n_mat
        p = 2
        while p < DN_CHUNK:
            m = _dot(m, m)
            t_mat = t_mat + _dot(m, t_mat)
            p *= 2
        sol = _dot(t_mat, jnp.concatenate([vb, kb * eg_c], axis=1))
        u_all = sol[:, :hd]
        w_all = sol[:, hd:]
        qg = q * eg_c
        k_t = k.T

        vnew_ref[h] = jnp.zeros(vnew_ref.shape[1:], BF16)
        s_h = state_ref[h]
        outs = []
        for n in range(nchunk):
            r0 = n * DN_CHUNK
            r1 = r0 + DN_CHUNK
            wq = jnp.concatenate([w_all[r0:r1], qg[r0:r1]], axis=0)
            ws_qs = _dot(wq, s_h)
            vnew_ref[h, r0:r1, :] = (u_all[r0:r1] - ws_qs[:DN_CHUNK]).astype(BF16)
            kdec_t = k_t * jnp.where(lane_c == n, egs_row, 0.0)
            av_kv = jnp.dot(jnp.concatenate([attn[r0:r1].astype(BF16), kdec_t.astype(BF16)], axis=0),
                            vnew_ref[h], preferred_element_type=F32)
            outs.append(ws_qs[DN_CHUNK:] + av_kv[:DN_CHUNK])
            egl = egl_c[r0:r1]
            s_h = jnp.concatenate([s_h[i:i + DN_CHUNK] * egl for i in range(0, hd, DN_CHUNK)], axis=0) \
                + av_kv[DN_CHUNK:]
        state_ref[h] = s_h
        o = jnp.concatenate(outs, axis=0)
        z = zd_ref[0, :, hd * h:hd * (h + 1)]
        y_ref[0, :, hd * h:hd * (h + 1)] = _rms(o, ng_ref[...]) * jax.nn.silu(z)


def _gdn(qkv, zd, ba_t, conv_w, a_log, dt_bias, norm_g):
    bsz, seq, d3 = qkv.shape
    d_dn = d3 // 3
    heads = d_dn // DN_HEAD_DIM
    tc = min(GDN_TILE, seq)
    nrow = ba_t.shape[1]
    col = lambda t: jnp.concatenate([jnp.zeros((heads,), F32), t.astype(F32)]).reshape(nrow, 1)
    full = lambda shape: pl.BlockSpec(shape, lambda b, l: (0,) * len(shape))
    return pl.pallas_call(
        functools.partial(_gdn_kernel, tc=tc, heads=heads),
        grid=(bsz, seq // tc),
        in_specs=[pl.BlockSpec((1, tc, d3), lambda b, l: (b, l, 0)),
                  pl.BlockSpec((1, tc, d_dn), lambda b, l: (b, l, 0)),
                  pl.BlockSpec((1, nrow, tc), lambda b, l: (b, 0, l)),
                  full(conv_w.shape), full((nrow, 1)), full((nrow, 1)), full((1, DN_HEAD_DIM))],
        out_specs=pl.BlockSpec((1, tc, d_dn), lambda b, l: (b, l, 0)),
        out_shape=jax.ShapeDtypeStruct((bsz, seq, d_dn), F32),
        scratch_shapes=[pltpu.VMEM((heads, DN_HEAD_DIM, DN_HEAD_DIM), F32),
                        pltpu.VMEM((8, d3), F32),
                        pltpu.VMEM((tc + 8, d3), F32),
                        pltpu.VMEM((heads, tc, DN_HEAD_DIM), BF16)],
        compiler_params=pltpu.CompilerParams(dimension_semantics=("arbitrary", "arbitrary"),
                                             vmem_limit_bytes=VMEM_LIMIT),
        name="gdn",
    )(qkv, zd, ba_t, conv_w.astype(F32), col(a_log), col(dt_bias), norm_g.astype(F32).reshape(1, DN_HEAD_DIM))


def _out_kernel(ys_ref, yd_ref, x_ref, p_ref, wos_ref, wod_ref, wp_ref, pg_ref, wg_ref, fg_ref, o_ref, *, final):
    h = x_ref[0] + _dot(ys_ref[...], wos_ref[...]) + _dot(yd_ref[0], wod_ref[...])
    e = _rms(_dot(p_ref[0], wp_ref[...]), pg_ref[...])
    h = h + jax.nn.sigmoid(_dot(h, wg_ref[...])) * e
    o_ref[0] = _rms(h, fg_ref[...]) if final else h


def _out_proj(ys_t, yd, x, p, w_out_s, w_out_d, w_ple, ple_g, w_gate, final_g, final):
    bsz, seq, dm = x.shape
    d_ssm = w_out_s.shape[0]
    d_dn = w_out_d.shape[0]
    tm = min(PROJ_TILE, seq)
    full = lambda shape: pl.BlockSpec(shape, lambda b, l: (0,) * len(shape))
    tok = lambda c: pl.BlockSpec((1, tm, c), lambda b, l: (b, l, 0))
    return pl.pallas_call(
        functools.partial(_out_kernel, final=final),
        grid=(bsz, seq // tm),
        in_specs=[pl.BlockSpec((tm, d_ssm), lambda b, l: (l, b)), tok(d_dn), tok(dm), tok(p.shape[-1]),
                  full(w_out_s.shape), full(w_out_d.shape), full(w_ple.shape), full((1, dm)),
                  full(w_gate.shape), full((1, dm))],
        out_specs=tok(dm),
        out_shape=jax.ShapeDtypeStruct((bsz, seq, dm), F32),
        compiler_params=pltpu.CompilerParams(dimension_semantics=("arbitrary", "arbitrary"),
                                             vmem_limit_bytes=VMEM_LIMIT),
        name="out_proj",
    )(ys_t, yd, x, p, w_out_s, w_out_d, w_ple, ple_g, w_gate, final_g)


def kernel(x, p, norm_mix_g, w_in, ssm_A_re, ssm_A_im, ssm_B_re, ssm_B_im, ssm_C_re, ssm_C_im, ssm_D, ssm_log_dt, ssm_w_glu, ssm_b_glu, dn_conv_w, dn_A_log, dn_dt_bias, dn_norm_g, w_out, w_ple_proj, ple_norm_g, w_ple_gate, final_norm_g):
    bsz, seq, dm = x.shape
    depth = w_in.shape[0]
    d_ssm = ssm_D.shape[-1]
    heads = dn_A_log.shape[-1]
    d_dn = heads * DN_HEAD_DIM
    n_main = 2 * d_ssm + 4 * d_dn
    row = lambda t: t.astype(F32).reshape(1, -1)
    h = x
    for i in range(depth):
        w_main = w_in[i, :, :n_main].astype(BF16)
        w_ba = jnp.pad(w_in[i, :, n_main:], ((0, 0), (0, LANES - 2 * heads))).astype(BF16)
        us_t, zs_t, qkv, zd, ba_t = _in_proj(h, row(norm_mix_g[i]), w_main, w_ba, d_ssm, d_dn)

        ab_r, ab_i, bb_r, bb_i = _s5_prep(ssm_A_re[i], ssm_A_im[i], ssm_log_dt[i], ssm_B_re[i], ssm_B_im[i])
        bw, cw, a_r, a_i = _s5_layout(ab_r, ab_i, bb_r, bb_i, ssm_C_re[i], ssm_C_im[i], bsz)
        ys_t = _s5(us_t.reshape(seq, bsz, d_ssm), zs_t.reshape(seq, bsz, d_ssm), bw, cw, a_r, a_i,
                   row(ssm_D[i]), ssm_w_glu[i].astype(BF16), row(ssm_b_glu[i]))

        yd = _gdn(qkv, zd, ba_t, dn_conv_w[i], dn_A_log[i], dn_dt_bias[i], dn_norm_g[i])

        h = _out_proj(ys_t.reshape(seq, bsz * d_ssm), yd, h, p[i],
                      w_out[i, :d_ssm].astype(BF16), w_out[i, d_ssm:].astype(BF16),
                      w_ple_proj[i].astype(BF16), row(ple_norm_g[i]), w_ple_gate[i].astype(BF16),
                      row(final_norm_g), final=(i == depth - 1))
    return h
```

```python
import functools

import jax
import jax.numpy as jnp
from jax import lax
from jax.experimental import pallas as pl
from jax.experimental.pallas import tpu as pltpu

F32 = jnp.float32
BF16 = jnp.bfloat16

NORM_EPS = 1e-6
SSM_GROUP = 16
SSM_STATE = 64
DN_HEAD_DIM = 128
DN_CONV = 4
DN_CHUNK = 128

LANES = 128
SSM_LANE_BLOCK = 128
SSM_STATE_BLOCK = (SSM_LANE_BLOCK // SSM_GROUP) * SSM_STATE

PROJ_TILE = 512
S5_TIME_TILE = 32
GDN_TILE = 256
VMEM_LIMIT = 48 * 1024 * 1024


def _dot(a, b):
    return jnp.dot(a.astype(BF16), b.astype(BF16), preferred_element_type=F32)


def _dot_nt(a, b):
    return lax.dot_general(a.astype(BF16), b.astype(BF16), (((1,), (1,)), ((), ())),
                           preferred_element_type=F32)


def _rms(x, g):
    return x * lax.rsqrt(jnp.mean(x * x, axis=-1, keepdims=True) + NORM_EPS) * g


def _s5_prep_kernel(lr_ref, li_ref, ldt_ref, br_ref, bi_ref, abr_ref, abi_ref, bbr_ref, bbi_ref):
    lr = lr_ref[...]
    li = li_ref[...]
    dt = jnp.exp(ldt_ref[...])
    mag = jnp.exp(lr * dt)
    ang = li * dt
    ab_r = mag * jnp.cos(ang)
    ab_i = mag * jnp.sin(ang)
    den = lr * lr + li * li
    nr = ab_r - 1.0
    ni = ab_i
    cr = (nr * lr + ni * li) / den
    ci = (ni * lr - nr * li) / den
    abr_ref[...] = ab_r
    abi_ref[...] = ab_i
    br = br_ref[...]
    bi = bi_ref[...]
    bbr_ref[...] = cr * br - ci * bi
    bbi_ref[...] = cr * bi + ci * br


def _s5_prep(a_re, a_im, log_dt, b_re, b_im):
    g, p = a_re.shape
    h = b_re.shape[-1]
    n = g * p
    col = lambda t: t.reshape(n, 1).astype(F32)
    ldt = jnp.broadcast_to(log_dt.astype(F32)[:, None], (g, p))
    out = pl.pallas_call(
        _s5_prep_kernel,
        out_shape=(jax.ShapeDtypeStruct((n, 1), F32), jax.ShapeDtypeStruct((n, 1), F32),
                   jax.ShapeDtypeStruct((n, h), F32), jax.ShapeDtypeStruct((n, h), F32)),
        name="s5_prep",
    )(col(a_re), col(a_im), col(ldt), b_re.reshape(n, h).astype(F32), b_im.reshape(n, h).astype(F32))
    ab_r, ab_i, bb_r, bb_i = out
    return ab_r.reshape(g, p), ab_i.reshape(g, p), bb_r.reshape(g, p, h), bb_i.reshape(g, p, h)


def _s5_layout(ab_r, ab_i, bb_r, bb_i, c_re, c_im, bsz):
    g, p = ab_r.shape
    h = bb_r.shape[-1]
    gpb = SSM_LANE_BLOCK // h
    nb = g // gpb
    eye = jnp.eye(gpb, dtype=F32)

    def in_w(bb):
        t = bb.reshape(nb, gpb, p, h)
        return jnp.einsum('jgph,gk->jghkp', t, eye).reshape(nb, gpb * h, gpb * p)

    def out_w(c):
        t = c.astype(F32).reshape(nb, gpb, h, p)
        return jnp.einsum('jghp,gk->jgpkh', t, eye).reshape(nb, gpb * p, gpb * h)

    bw = jnp.concatenate([in_w(bb_r), in_w(bb_i)], axis=2).astype(BF16)
    cw = jnp.concatenate([out_w(c_re), -out_w(c_im)], axis=1).astype(BF16)
    a_r = jnp.broadcast_to(ab_r.reshape(1, g * p), (bsz, g * p))
    a_i = jnp.broadcast_to(ab_i.reshape(1, g * p), (bsz, g * p))
    return bw, cw, a_r, a_i


def _in_proj_kernel(x_ref, g_ref, wm_ref, wba_ref, cw_ref, us_ref, zs_ref, qkv_ref, zd_ref, ba_ref,
                    tail_ref, xpad_ref, *, d_ssm, d_dn):
    tm = x_ref.shape[1]
    hd = DN_HEAD_DIM

    @pl.when(pl.program_id(1) == 0)
    def _():
        tail_ref[...] = jnp.zeros_like(tail_ref)

    a = _rms(x_ref[0], g_ref[...]).astype(BF16)
    o = 0
    us_ref[...] = jnp.dot(a, wm_ref[:, o:o + d_ssm], preferred_element_type=F32)
    o += d_ssm
    zs_ref[...] = jnp.dot(a, wm_ref[:, o:o + d_ssm], preferred_element_type=F32)
    o += d_ssm
    qkv = jnp.dot(a, wm_ref[:, o:o + 3 * d_dn], preferred_element_type=F32)
    o += 3 * d_dn
    zd_ref[0] = jnp.dot(a, wm_ref[:, o:o + d_dn], preferred_element_type=F32)
    ba = jnp.dot(a, wba_ref[...], preferred_element_type=F32)
    ba_ref[0] = ba.T[:ba_ref.shape[1]]

    xpad_ref[0:8, :] = tail_ref[...]
    xpad_ref[8:, :] = qkv
    tail_ref[...] = qkv[tm - 8:, :]
    acc = cw_ref[DN_CONV - 1:DN_CONV, :] * qkv
    for s in range(1, DN_CONV):
        acc = acc + cw_ref[DN_CONV - 1 - s:DN_CONV - s, :] * xpad_ref[8 - s:8 - s + tm, :]
    c = jax.nn.silu(acc)
    for i in range(2 * d_dn // hd):
        t = c[:, hd * i:hd * (i + 1)]
        t = t * lax.rsqrt(jnp.sum(t * t, axis=-1, keepdims=True) + NORM_EPS)
        qkv_ref[0, :, hd * i:hd * (i + 1)] = t * (hd ** -0.5) if hd * i < d_dn else t
    qkv_ref[0, :, 2 * d_dn:] = c[:, 2 * d_dn:]


def _in_proj(x, g, w_main, w_ba, conv_w, d_ssm, d_dn):
    bsz, seq, dm = x.shape
    tm = min(PROJ_TILE, seq)
    nrow = 2 * (d_dn // DN_HEAD_DIM)
    grid = (bsz, seq // tm)
    full = lambda shape: pl.BlockSpec(shape, lambda b, l: (0,) * len(shape))
    return pl.pallas_call(
        functools.partial(_in_proj_kernel, d_ssm=d_ssm, d_dn=d_dn),
        grid=grid,
        in_specs=[pl.BlockSpec((1, tm, dm), lambda b, l: (b, l, 0)),
                  full((1, dm)), full(w_main.shape), full(w_ba.shape), full(conv_w.shape)],
        out_specs=[pl.BlockSpec((tm, d_ssm), lambda b, l: (l, b)),
                   pl.BlockSpec((tm, d_ssm), lambda b, l: (l, b)),
                   pl.BlockSpec((1, tm, 3 * d_dn), lambda b, l: (b, l, 0)),
                   pl.BlockSpec((1, tm, d_dn), lambda b, l: (b, l, 0)),
                   pl.BlockSpec((1, nrow, tm), lambda b, l: (b, 0, l))],
        out_shape=(jax.ShapeDtypeStruct((seq, bsz * d_ssm), F32),
                   jax.ShapeDtypeStruct((seq, bsz * d_ssm), F32),
                   jax.ShapeDtypeStruct((bsz, seq, 3 * d_dn), F32),
                   jax.ShapeDtypeStruct((bsz, seq, d_dn), F32),
                   jax.ShapeDtypeStruct((bsz, nrow, seq), F32)),
        scratch_shapes=[pltpu.VMEM((8, 3 * d_dn), F32), pltpu.VMEM((tm + 8, 3 * d_dn), F32)],
        compiler_params=pltpu.CompilerParams(dimension_semantics=("arbitrary", "arbitrary"),
                                             vmem_limit_bytes=VMEM_LIMIT),
        name="in_proj",
    )(x, g, w_main, w_ba, conv_w)


def _s5_kernel(u_ref, z_ref, bw_ref, cw_ref, ar_ref, ai_ref, d_ref, wglu_ref, bglu_ref, y_ref,
               bu_ref, s_ref, carry_ref, *, tt, bsz, nb):
    sb = SSM_STATE_BLOCK
    rows = tt * bsz

    @pl.when(pl.program_id(0) == 0)
    def _():
        carry_ref[...] = jnp.zeros_like(carry_ref)

    u = u_ref[...].reshape(rows, nb * SSM_LANE_BLOCK)
    u16 = u.astype(BF16)
    for j in range(nb):
        bu_ref[:, 2 * sb * j:2 * sb * (j + 1)] = jnp.dot(
            u16[:, SSM_LANE_BLOCK * j:SSM_LANE_BLOCK * (j + 1)], bw_ref[j], preferred_element_type=F32)

    for j in range(nb):
        cr = 2 * sb * j
        ci = cr + sb
        a_r = ar_ref[:, sb * j:sb * (j + 1)]
        a_i = ai_ref[:, sb * j:sb * (j + 1)]

        def step(t, s, cr=cr, ci=ci, a_r=a_r, a_i=a_i):
            s_r, s_i = s
            r0 = pl.multiple_of(t * bsz, bsz)
            n_r = a_r * s_r - a_i * s_i + bu_ref[pl.ds(r0, bsz), cr:cr + sb]
            n_i = a_r * s_i + a_i * s_r + bu_ref[pl.ds(r0, bsz), ci:ci + sb]
            s_ref[pl.ds(r0, bsz), cr:cr + sb] = n_r.astype(BF16)
            s_ref[pl.ds(r0, bsz), ci:ci + sb] = n_i.astype(BF16)
            return n_r, n_i

        s_r, s_i = lax.fori_loop(0, tt, step, (carry_ref[:, cr:cr + sb], carry_ref[:, ci:ci + sb]), unroll=4)
        carry_ref[:, cr:cr + sb] = s_r
        carry_ref[:, ci:ci + sb] = s_i

    y = jnp.concatenate(
        [jnp.dot(s_ref[:, 2 * sb * j:2 * sb * (j + 1)], cw_ref[j], preferred_element_type=F32) for j in range(nb)],
        axis=1)
    y = y + d_ref[...] * u
    y = jax.nn.gelu(y)
    y = y * jax.nn.sigmoid(_dot(y, wglu_ref[...]) + bglu_ref[...])
    y = y * jax.nn.silu(z_ref[...].reshape(rows, nb * SSM_LANE_BLOCK))
    y_ref[...] = y.reshape(y_ref.shape)


def _s5(us_t, zs_t, bw, cw, a_r, a_i, d_vec, w_glu, b_glu):
    seq, bsz, d_ssm = us_t.shape
    nb = bw.shape[0]
    tt = min(S5_TIME_TILE, seq)
    rows = tt * bsz
    n_state = 2 * SSM_STATE_BLOCK * nb
    full = lambda shape: pl.BlockSpec(shape, lambda t: (0,) * len(shape))
    blk = pl.BlockSpec((tt, bsz, d_ssm), lambda t: (t, 0, 0))
    return pl.pallas_call(
        functools.partial(_s5_kernel, tt=tt, bsz=bsz, nb=nb),
        grid=(seq // tt,),
        in_specs=[blk, blk, full(bw.shape), full(cw.shape), full(a_r.shape), full(a_i.shape),
                  full(d_vec.shape), full(w_glu.shape), full(b_glu.shape)],
        out_specs=blk,
        out_shape=jax.ShapeDtypeStruct((seq, bsz, d_ssm), F32),
        scratch_shapes=[pltpu.VMEM((rows, n_state), F32),
                        pltpu.VMEM((rows, n_state), BF16),
                        pltpu.VMEM((bsz, n_state), F32)],
        compiler_params=pltpu.CompilerParams(dimension_semantics=("arbitrary",),
                                             vmem_limit_bytes=VMEM_LIMIT),
        name="s5",
    )(us_t, zs_t, bw, cw, a_r, a_i, d_vec, w_glu, b_glu)


def _chunk_scan(x, pos, forward):
    n = x.shape[-1]
    sh = 1
    while sh < DN_CHUNK:
        if forward:
            x = x + jnp.where(pos >= sh, pltpu.roll(x, sh, axis=1), 0.0)
        else:
            x = x + jnp.where(pos < DN_CHUNK - sh, pltpu.roll(x, n - sh, axis=1), 0.0)
        sh *= 2
    return x


_M_CAUSAL, _M_STRICT, _M_EYE, _M_LEVEL0 = 0, 1, 2, 3


def _gdn_fill_masks(mask_ref, tc):
    ri = lax.broadcasted_iota(jnp.int32, (tc, tc), 0)
    ci = lax.broadcasted_iota(jnp.int32, (tc, tc), 1)
    same = (ri // DN_CHUNK) == (ci // DN_CHUNK)
    mask_ref[_M_CAUSAL] = (same & (ri >= ci)).astype(F32)
    mask_ref[_M_STRICT] = (same & (ri > ci)).astype(F32)
    mask_ref[_M_EYE] = (ri == ci).astype(F32)
    s = 1
    lvl = 0
    while s < DN_CHUNK:
        m = ((ri // (2 * s)) == (ci // (2 * s))) & (((ri // s) % 2) == 1) & (((ci // s) % 2) == 0)
        mask_ref[_M_LEVEL0 + lvl] = m.astype(F32)
        s *= 2
        lvl += 1


def _gdn_kernel(qkv_ref, zd_ref, ba_ref, alog_ref, dtb_ref, ng_ref, y_ref, state_ref, mask_ref, *, tc, heads):
    hd = DN_HEAD_DIM
    d_dn = heads * hd
    nchunk = tc // DN_CHUNK
    hs = range(heads)

    @pl.when(pl.program_id(1) == 0)
    def _():
        state_ref[...] = jnp.zeros_like(state_ref)
        _gdn_fill_masks(mask_ref, tc)

    ba = ba_ref[0]
    pos = lax.broadcasted_iota(jnp.int32, ba.shape, 1) % DN_CHUNK
    beta_r = jax.nn.sigmoid(ba)
    g_r = -jnp.exp(alog_ref[...]) * jax.nn.softplus(ba + dtb_ref[...])
    gc_r = _chunk_scan(g_r, pos, True)
    gs_r = _chunk_scan(g_r, pos, False) - g_r
    egs_r = jnp.exp(gs_r)
    egl_r = jnp.exp(gc_r + gs_r)
    pad = jnp.zeros((LANES - 3 * 2 * heads, tc), F32)
    cols = jnp.concatenate([beta_r, gc_r, egl_r, pad], axis=0).T

    attn16, a16, t_mat, u_all, w_all, qg, kdec16, egl_c = [], [], [], [], [], [], [], []
    rhs16 = []
    for h in hs:
        q = qkv_ref[0, :, hd * h:hd * (h + 1)]
        k = qkv_ref[0, :, d_dn + hd * h:d_dn + hd * (h + 1)]
        v = qkv_ref[0, :, 2 * d_dn + hd * h:2 * d_dn + hd * (h + 1)]
        beta_c = cols[:, h:h + 1]
        gc_c = cols[:, 3 * heads + h:3 * heads + h + 1]
        egl_c.append(cols[:, 5 * heads + h:5 * heads + h + 1])
        gc_row = gc_r[heads + h:heads + h + 1, :]
        eg_c = jnp.exp(gc_c)
        kb = k * beta_c
        qg.append(q * eg_c)
        rhs16.append(jnp.concatenate([v * beta_c, kb * eg_c], axis=1).astype(BF16))
        kdec16.append((k.T * egs_r[heads + h:heads + h + 1, :]).astype(BF16))
        decay = jnp.exp(jnp.minimum(gc_c - gc_row, 0.0))
        qk_kk = _dot_nt(jnp.concatenate([q, kb], axis=0), k)
        attn16.append((qk_kk[:tc] * decay * mask_ref[_M_CAUSAL]).astype(BF16))
        a = qk_kk[tc:] * decay * mask_ref[_M_STRICT]
        a16.append(a.astype(BF16))
        t_mat.append(mask_ref[_M_EYE] - a * mask_ref[_M_LEVEL0])

    t16 = [t.astype(BF16) for t in t_mat]
    lvl = 1
    s = 2
    while s < DN_CHUNK:
        x16 = [jnp.dot(a16[h], t16[h], preferred_element_type=F32).astype(BF16) for h in hs]
        y = [jnp.dot(t16[h], x16[h], preferred_element_type=F32) for h in hs]
        t_mat = [t_mat[h] - y[h] * mask_ref[_M_LEVEL0 + lvl] for h in hs]
        t16 = [t.astype(BF16) for t in t_mat]
        s *= 2
        lvl += 1
    for h in hs:
        sol = jnp.dot(t16[h], rhs16[h], preferred_element_type=F32)
        u_all.append(sol[:, :hd])
        w_all.append(sol[:, hd:])

    st = [state_ref[h] for h in hs]
    outs = [[] for _ in hs]
    for n in range(nchunk):
        r0 = n * DN_CHUNK
        r1 = r0 + DN_CHUNK
        ws_qs = [_dot(jnp.concatenate([w_all[h][r0:r1], qg[h][r0:r1]], axis=0), st[h]) for h in hs]
        v_new = [(u_all[h][r0:r1] - ws_qs[h][:DN_CHUNK]).astype(BF16) for h in hs]
        av_kv = [jnp.dot(jnp.concatenate([attn16[h][r0:r1, r0:r1], kdec16[h][:, r0:r1]], axis=0), v_new[h],
                         preferred_element_type=F32) for h in hs]
        for h in hs:
            outs[h].append(ws_qs[h][DN_CHUNK:] + av_kv[h][:DN_CHUNK])
            st[h] = st[h] * egl_c[h][r0:r1] + av_kv[h][DN_CHUNK:]
    for h in hs:
        state_ref[h] = st[h]
        o = jnp.concatenate(outs[h], axis=0)
        z = zd_ref[0, :, hd * h:hd * (h + 1)]
        y_ref[0, :, hd * h:hd * (h + 1)] = _rms(o, ng_ref[...]) * jax.nn.silu(z)


def _gdn(qkv, zd, ba_t, a_log, dt_bias, norm_g):
    bsz, seq, d3 = qkv.shape
    d_dn = d3 // 3
    heads = d_dn // DN_HEAD_DIM
    tc = min(GDN_TILE, seq)
    nrow = ba_t.shape[1]
    assert DN_CHUNK == DN_HEAD_DIM and tc % DN_CHUNK == 0
    n_mask = _M_LEVEL0 + DN_CHUNK.bit_length() - 1
    col = lambda t: jnp.concatenate([jnp.zeros((heads,), F32), t.astype(F32)]).reshape(nrow, 1)
    full = lambda shape: pl.BlockSpec(shape, lambda b, l: (0,) * len(shape))
    return pl.pallas_call(
        functools.partial(_gdn_kernel, tc=tc, heads=heads),
        grid=(bsz, seq // tc),
        in_specs=[pl.BlockSpec((1, tc, d3), lambda b, l: (b, l, 0)),
                  pl.BlockSpec((1, tc, d_dn), lambda b, l: (b, l, 0)),
                  pl.BlockSpec((1, nrow, tc), lambda b, l: (b, 0, l)),
                  full((nrow, 1)), full((nrow, 1)), full((1, DN_HEAD_DIM))],
        out_specs=pl.BlockSpec((1, tc, d_dn), lambda b, l: (b, l, 0)),
        out_shape=jax.ShapeDtypeStruct((bsz, seq, d_dn), F32),
        scratch_shapes=[pltpu.VMEM((heads, DN_HEAD_DIM, DN_HEAD_DIM), F32),
                        pltpu.VMEM((n_mask, tc, tc), F32)],
        compiler_params=pltpu.CompilerParams(dimension_semantics=("arbitrary", "arbitrary"),
                                             vmem_limit_bytes=VMEM_LIMIT),
        name="gdn",
    )(qkv, zd, ba_t, col(a_log), col(dt_bias), norm_g.astype(F32).reshape(1, DN_HEAD_DIM))


def _out_kernel(ys_ref, yd_ref, x_ref, p_ref, wos_ref, wod_ref, wp_ref, pg_ref, wg_ref, fg_ref, o_ref, *, final):
    h = x_ref[0] + _dot(ys_ref[...], wos_ref[...]) + _dot(yd_ref[0], wod_ref[...])
    e = _rms(_dot(p_ref[0], wp_ref[...]), pg_ref[...])
    h = h + jax.nn.sigmoid(_dot(h, wg_ref[...])) * e
    o_ref[0] = _rms(h, fg_ref[...]) if final else h


def _out_proj(ys_t, yd, x, p, w_out_s, w_out_d, w_ple, ple_g, w_gate, final_g, final):
    bsz, seq, dm = x.shape
    d_ssm = w_out_s.shape[0]
    d_dn = w_out_d.shape[0]
    tm = min(PROJ_TILE, seq)
    full = lambda shape: pl.BlockSpec(shape, lambda b, l: (0,) * len(shape))
    tok = lambda c: pl.BlockSpec((1, tm, c), lambda b, l: (b, l, 0))
    return pl.pallas_call(
        functools.partial(_out_kernel, final=final),
        grid=(bsz, seq // tm),
        in_specs=[pl.BlockSpec((tm, d_ssm), lambda b, l: (l, b)), tok(d_dn), tok(dm), tok(p.shape[-1]),
                  full(w_out_s.shape), full(w_out_d.shape), full(w_ple.shape), full((1, dm)),
                  full(w_gate.shape), full((1, dm))],
        out_specs=tok(dm),
        out_shape=jax.ShapeDtypeStruct((bsz, seq, dm), F32),
        compiler_params=pltpu.CompilerParams(dimension_semantics=("arbitrary", "arbitrary"),
                                             vmem_limit_bytes=VMEM_LIMIT),
        name="out_proj",
    )(ys_t, yd, x, p, w_out_s, w_out_d, w_ple, ple_g, w_gate, final_g)


def kernel(x, p, norm_mix_g, w_in, ssm_A_re, ssm_A_im, ssm_B_re, ssm_B_im, ssm_C_re, ssm_C_im, ssm_D, ssm_log_dt, ssm_w_glu, ssm_b_glu, dn_conv_w, dn_A_log, dn_dt_bias, dn_norm_g, w_out, w_ple_proj, ple_norm_g, w_ple_gate, final_norm_g):
    bsz, seq, dm = x.shape
    depth = w_in.shape[0]
    d_ssm = ssm_D.shape[-1]
    heads = dn_A_log.shape[-1]
    d_dn = heads * DN_HEAD_DIM
    n_main = 2 * d_ssm + 4 * d_dn
    row = lambda t: t.astype(F32).reshape(1, -1)
    h = x
    for i in range(depth):
        w_main = w_in[i, :, :n_main].astype(BF16)
        w_ba = jnp.pad(w_in[i, :, n_main:], ((0, 0), (0, LANES - 2 * heads))).astype(BF16)
        us_t, zs_t, qkv, zd, ba_t = _in_proj(h, row(norm_mix_g[i]), w_main, w_ba, dn_conv_w[i].astype(F32),
                                             d_ssm, d_dn)

        ab_r, ab_i, bb_r, bb_i = _s5_prep(ssm_A_re[i], ssm_A_im[i], ssm_log_dt[i], ssm_B_re[i], ssm_B_im[i])
        bw, cw, a_r, a_i = _s5_layout(ab_r, ab_i, bb_r, bb_i, ssm_C_re[i], ssm_C_im[i], bsz)
        ys_t = _s5(us_t.reshape(seq, bsz, d_ssm), zs_t.reshape(seq, bsz, d_ssm), bw, cw, a_r, a_i,
                   row(ssm_D[i]), ssm_w_glu[i].astype(BF16), row(ssm_b_glu[i]))

        yd = _gdn(qkv, zd, ba_t, dn_A_log[i], dn_dt_bias[i], dn_norm_g[i])

        h = _out_proj(ys_t.reshape(seq, bsz * d_ssm), yd, h, p[i],
                      w_out[i, :d_ssm].astype(BF16), w_out[i, d_ssm:].astype(BF16),
                      w_ple_proj[i].astype(BF16), row(ple_norm_g[i]), w_ple_gate[i].astype(BF16),
                      row(final_norm_g), final=(i == depth - 1))
    return h
```

```python
import functools

import jax
import jax.numpy as jnp
from jax import lax
from jax.experimental import pallas as pl
from jax.experimental.pallas import tpu as pltpu

F32 = jnp.float32
BF16 = jnp.bfloat16

NORM_EPS = 1e-6
SSM_GROUP = 16
SSM_STATE = 64
DN_HEAD_DIM = 128
DN_CONV = 4
DN_CHUNK = 128

LANES = 128
SSM_LANE_BLOCK = 128
SSM_STATE_BLOCK = (SSM_LANE_BLOCK // SSM_GROUP) * SSM_STATE

PROJ_TILE = 512
S5_TIME_TILE = 32
GDN_TILE = 256
VMEM_LIMIT = 48 * 1024 * 1024


def _dot(a, b):
    return jnp.dot(a.astype(BF16), b.astype(BF16), preferred_element_type=F32)


def _dot_nt(a, b):
    return lax.dot_general(a.astype(BF16), b.astype(BF16), (((1,), (1,)), ((), ())),
                           preferred_element_type=F32)


def _rms(x, g):
    return x * lax.rsqrt(jnp.mean(x * x, axis=-1, keepdims=True) + NORM_EPS) * g


def _s5_prep_kernel(lr_ref, li_ref, ldt_ref, br_ref, bi_ref, abr_ref, abi_ref, bbr_ref, bbi_ref):
    lr = lr_ref[...]
    li = li_ref[...]
    dt = jnp.exp(ldt_ref[...])
    mag = jnp.exp(lr * dt)
    ang = li * dt
    ab_r = mag * jnp.cos(ang)
    ab_i = mag * jnp.sin(ang)
    den = lr * lr + li * li
    nr = ab_r - 1.0
    ni = ab_i
    cr = (nr * lr + ni * li) / den
    ci = (ni * lr - nr * li) / den
    abr_ref[...] = ab_r
    abi_ref[...] = ab_i
    br = br_ref[...]
    bi = bi_ref[...]
    bbr_ref[...] = cr * br - ci * bi
    bbi_ref[...] = cr * bi + ci * br


def _s5_prep(a_re, a_im, log_dt, b_re, b_im):
    g, p = a_re.shape
    h = b_re.shape[-1]
    n = g * p
    col = lambda t: t.reshape(n, 1).astype(F32)
    ldt = jnp.broadcast_to(log_dt.astype(F32)[:, None], (g, p))
    out = pl.pallas_call(
        _s5_prep_kernel,
        out_shape=(jax.ShapeDtypeStruct((n, 1), F32), jax.ShapeDtypeStruct((n, 1), F32),
                   jax.ShapeDtypeStruct((n, h), F32), jax.ShapeDtypeStruct((n, h), F32)),
        name="s5_prep",
    )(col(a_re), col(a_im), col(ldt), b_re.reshape(n, h).astype(F32), b_im.reshape(n, h).astype(F32))
    ab_r, ab_i, bb_r, bb_i = out
    return ab_r.reshape(g, p), ab_i.reshape(g, p), bb_r.reshape(g, p, h), bb_i.reshape(g, p, h)


def _s5_layout(ab_r, ab_i, bb_r, bb_i, c_re, c_im, bsz):
    g, p = ab_r.shape
    h = bb_r.shape[-1]
    gpb = SSM_LANE_BLOCK // h
    nb = g // gpb
    eye = jnp.eye(gpb, dtype=F32)

    def in_w(bb):
        t = bb.reshape(nb, gpb, p, h)
        return jnp.einsum('jgph,gk->jghkp', t, eye).reshape(nb, gpb * h, gpb * p)

    def out_w(c):
        t = c.astype(F32).reshape(nb, gpb, h, p)
        return jnp.einsum('jghp,gk->jgpkh', t, eye).reshape(nb, gpb * p, gpb * h)

    bw = jnp.concatenate([in_w(bb_r), in_w(bb_i)], axis=2).astype(BF16)
    cw = jnp.concatenate([out_w(c_re), -out_w(c_im)], axis=1).astype(BF16)
    a_r = jnp.broadcast_to(ab_r.reshape(1, g * p), (bsz, g * p))
    a_i = jnp.broadcast_to(ab_i.reshape(1, g * p), (bsz, g * p))
    return bw, cw, a_r, a_i


def _in_proj_kernel(x_ref, g_ref, wm_ref, wba_ref, cw_ref, us_ref, zs_ref, qkv_ref, zd_ref, ba_ref,
                    tail_ref, *, d_ssm, d_dn):
    tm = x_ref.shape[1]
    hd = DN_HEAD_DIM

    @pl.when(pl.program_id(1) == 0)
    def _():
        tail_ref[...] = jnp.zeros_like(tail_ref)

    a = _rms(x_ref[0], g_ref[...]).astype(BF16)
    qkv = jnp.dot(a, wm_ref[:, 2 * d_ssm:2 * d_ssm + 3 * d_dn], preferred_element_type=F32)
    us_ref[...] = jnp.dot(a, wm_ref[:, :d_ssm], preferred_element_type=F32)
    zs_ref[...] = jnp.dot(a, wm_ref[:, d_ssm:2 * d_ssm], preferred_element_type=F32)
    zd_ref[0] = jnp.dot(a, wm_ref[:, 2 * d_ssm + 3 * d_dn:], preferred_element_type=F32)
    ba = jnp.dot(a, wba_ref[...], preferred_element_type=F32)
    ba_ref[0] = ba.T[:ba_ref.shape[1]]

    xp = jnp.concatenate([tail_ref[...], qkv], axis=0)
    tail_ref[...] = qkv[tm - 8:, :]
    acc = cw_ref[0:1, :] * xp
    for j in range(1, DN_CONV):
        acc = cw_ref[j:j + 1, :] * xp + pltpu.roll(acc, 1, axis=0)
    c = jax.nn.silu(acc[8:])
    for i in range(2 * d_dn // hd):
        t = c[:, hd * i:hd * (i + 1)]
        t = t * lax.rsqrt(jnp.sum(t * t, axis=-1, keepdims=True) + NORM_EPS)
        qkv_ref[0, :, hd * i:hd * (i + 1)] = t * (hd ** -0.5) if hd * i < d_dn else t
    qkv_ref[0, :, 2 * d_dn:] = c[:, 2 * d_dn:]


def _in_proj(x, g, w_main, w_ba, conv_w, d_ssm, d_dn):
    bsz, seq, dm = x.shape
    tm = min(PROJ_TILE, seq)
    nrow = 2 * (d_dn // DN_HEAD_DIM)
    grid = (bsz, seq // tm)
    full = lambda shape: pl.BlockSpec(shape, lambda b, l: (0,) * len(shape))
    return pl.pallas_call(
        functools.partial(_in_proj_kernel, d_ssm=d_ssm, d_dn=d_dn),
        grid=grid,
        in_specs=[pl.BlockSpec((1, tm, dm), lambda b, l: (b, l, 0)),
                  full((1, dm)), full(w_main.shape), full(w_ba.shape), full(conv_w.shape)],
        out_specs=[pl.BlockSpec((tm, d_ssm), lambda b, l: (l, b)),
                   pl.BlockSpec((tm, d_ssm), lambda b, l: (l, b)),
                   pl.BlockSpec((1, tm, 3 * d_dn), lambda b, l: (b, l, 0)),
                   pl.BlockSpec((1, tm, d_dn), lambda b, l: (b, l, 0)),
                   pl.BlockSpec((1, nrow, tm), lambda b, l: (b, 0, l))],
        out_shape=(jax.ShapeDtypeStruct((seq, bsz * d_ssm), F32),
                   jax.ShapeDtypeStruct((seq, bsz * d_ssm), F32),
                   jax.ShapeDtypeStruct((bsz, seq, 3 * d_dn), F32),
                   jax.ShapeDtypeStruct((bsz, seq, d_dn), F32),
                   jax.ShapeDtypeStruct((bsz, nrow, seq), F32)),
        scratch_shapes=[pltpu.VMEM((8, 3 * d_dn), F32)],
        compiler_params=pltpu.CompilerParams(dimension_semantics=("arbitrary", "arbitrary"),
                                             vmem_limit_bytes=VMEM_LIMIT),
        name="in_proj",
    )(x, g, w_main, w_ba, conv_w)


def _s5_kernel(u_ref, z_ref, bw_ref, cw_ref, ar_ref, ai_ref, d_ref, wglu_ref, bglu_ref, y_ref,
               bu_ref, s_ref, carry_ref, *, tt, bsz, nb):
    sb = SSM_STATE_BLOCK
    rows = tt * bsz

    @pl.when(pl.program_id(0) == 0)
    def _():
        carry_ref[...] = jnp.zeros_like(carry_ref)

    u = u_ref[...].reshape(rows, nb * SSM_LANE_BLOCK)
    u16 = u.astype(BF16)
    for j in range(nb):
        bu_ref[:, 2 * sb * j:2 * sb * (j + 1)] = jnp.dot(
            u16[:, SSM_LANE_BLOCK * j:SSM_LANE_BLOCK * (j + 1)], bw_ref[j], preferred_element_type=F32)

    for j in range(nb):
        cr = 2 * sb * j
        ci = cr + sb
        a_r = ar_ref[:, sb * j:sb * (j + 1)]
        a_i = ai_ref[:, sb * j:sb * (j + 1)]

        def step(t, s, cr=cr, ci=ci, a_r=a_r, a_i=a_i):
            s_r, s_i = s
            r0 = pl.multiple_of(t * bsz, bsz)
            n_r = a_r * s_r - a_i * s_i + bu_ref[pl.ds(r0, bsz), cr:cr + sb]
            n_i = a_r * s_i + a_i * s_r + bu_ref[pl.ds(r0, bsz), ci:ci + sb]
            s_ref[pl.ds(r0, bsz), cr:cr + sb] = n_r.astype(BF16)
            s_ref[pl.ds(r0, bsz), ci:ci + sb] = n_i.astype(BF16)
            return n_r, n_i

        s_r, s_i = lax.fori_loop(0, tt, step, (carry_ref[:, cr:cr + sb], carry_ref[:, ci:ci + sb]), unroll=True)
        carry_ref[:, cr:cr + sb] = s_r
        carry_ref[:, ci:ci + sb] = s_i

    y = jnp.concatenate(
        [jnp.dot(s_ref[:, 2 * sb * j:2 * sb * (j + 1)], cw_ref[j], preferred_element_type=F32) for j in range(nb)],
        axis=1)
    y = y + d_ref[...] * u
    y = jax.nn.gelu(y)
    y = y * jax.nn.sigmoid(_dot(y, wglu_ref[...]) + bglu_ref[...])
    y = y * jax.nn.silu(z_ref[...].reshape(rows, nb * SSM_LANE_BLOCK))
    y_ref[...] = y.reshape(y_ref.shape)


def _s5(us_t, zs_t, bw, cw, a_r, a_i, d_vec, w_glu, b_glu):
    seq, bsz, d_ssm = us_t.shape
    nb = bw.shape[0]
    tt = min(S5_TIME_TILE, seq)
    rows = tt * bsz
    n_state = 2 * SSM_STATE_BLOCK * nb
    full = lambda shape: pl.BlockSpec(shape, lambda t: (0,) * len(shape))
    blk = pl.BlockSpec((tt, bsz, d_ssm), lambda t: (t, 0, 0))
    return pl.pallas_call(
        functools.partial(_s5_kernel, tt=tt, bsz=bsz, nb=nb),
        grid=(seq // tt,),
        in_specs=[blk, blk, full(bw.shape), full(cw.shape), full(a_r.shape), full(a_i.shape),
                  full(d_vec.shape), full(w_glu.shape), full(b_glu.shape)],
        out_specs=blk,
        out_shape=jax.ShapeDtypeStruct((seq, bsz, d_ssm), F32),
        scratch_shapes=[pltpu.VMEM((rows, n_state), F32),
                        pltpu.VMEM((rows, n_state), BF16),
                        pltpu.VMEM((bsz, n_state), F32)],
        compiler_params=pltpu.CompilerParams(dimension_semantics=("arbitrary",),
                                             vmem_limit_bytes=VMEM_LIMIT),
        name="s5",
    )(us_t, zs_t, bw, cw, a_r, a_i, d_vec, w_glu, b_glu)


def _chunk_sums(g, sum_ref):
    n = g.shape[0]
    hi = g.astype(BF16).astype(F32)
    mid = (g - hi).astype(BF16).astype(F32)
    lo = g - hi - mid
    parts = jnp.concatenate([hi, mid, lo], axis=0).astype(BF16)
    s = jnp.dot(parts, sum_ref[...], preferred_element_type=F32)
    s = s[:n] + s[n:2 * n] + s[2 * n:]
    tc = g.shape[1]
    return s[:, :tc], s[:, tc:]


_M_CAUSAL, _M_STRICT, _M_EYE, _M_LEVEL0 = 0, 1, 2, 3


def _gdn_fill_masks(mask_ref, sum_ref, tc):
    ri = lax.broadcasted_iota(jnp.int32, (tc, tc), 0)
    ci = lax.broadcasted_iota(jnp.int32, (tc, tc), 1)
    same = (ri // DN_CHUNK) == (ci // DN_CHUNK)
    mask_ref[_M_CAUSAL] = (same & (ri >= ci)).astype(F32)
    mask_ref[_M_STRICT] = (same & (ri > ci)).astype(F32)
    mask_ref[_M_EYE] = (ri == ci).astype(F32)
    sum_ref[:, :tc] = (same & (ri <= ci)).astype(F32).astype(BF16)
    sum_ref[:, tc:] = (same & (ri > ci)).astype(F32).astype(BF16)
    s = 1
    lvl = 0
    while s < DN_CHUNK:
        m = ((ri // (2 * s)) == (ci // (2 * s))) & (((ri // s) % 2) == 1) & (((ci // s) % 2) == 0)
        mask_ref[_M_LEVEL0 + lvl] = m.astype(F32)
        s *= 2
        lvl += 1


def _gdn_kernel(qkv_ref, zd_ref, ba_ref, alog_ref, dtb_ref, ng_ref, y_ref, state_ref, mask_ref, sum_ref,
                *, tc, heads):
    hd = DN_HEAD_DIM
    d_dn = heads * hd
    nchunk = tc // DN_CHUNK
    hs = range(heads)

    @pl.when(pl.program_id(1) == 0)
    def _():
        state_ref[...] = jnp.zeros_like(state_ref)
        _gdn_fill_masks(mask_ref, sum_ref, tc)

    ba = ba_ref[0]
    beta_r = jax.nn.sigmoid(ba)
    g_r = -jnp.exp(alog_ref[...]) * jax.nn.softplus(ba + dtb_ref[...])
    gc_r, gs_r = _chunk_sums(g_r, sum_ref)
    egs_r = jnp.exp(gs_r)
    egl_r = jnp.exp(gc_r + gs_r)
    pad = jnp.zeros((LANES - 3 * 2 * heads, tc), F32)
    cols = jnp.concatenate([beta_r, gc_r, egl_r, pad], axis=0).T

    attn16, a16, t_mat, u_all, w_all, qg, kdec16, egl_c = [], [], [], [], [], [], [], []
    rhs16 = []
    for h in hs:
        q = qkv_ref[0, :, hd * h:hd * (h + 1)]
        k = qkv_ref[0, :, d_dn + hd * h:d_dn + hd * (h + 1)]
        v = qkv_ref[0, :, 2 * d_dn + hd * h:2 * d_dn + hd * (h + 1)]
        beta_c = cols[:, h:h + 1]
        gc_c = cols[:, 3 * heads + h:3 * heads + h + 1]
        egl_c.append(cols[:, 5 * heads + h:5 * heads + h + 1])
        gc_row = gc_r[heads + h:heads + h + 1, :]
        eg_c = jnp.exp(gc_c)
        kb = k * beta_c
        qg.append(q * eg_c)
        rhs16.append(jnp.concatenate([v * beta_c, kb * eg_c], axis=1).astype(BF16))
        kdec16.append((k.T * egs_r[heads + h:heads + h + 1, :]).astype(BF16))
        decay = jnp.exp(jnp.minimum(gc_c - gc_row, 0.0))
        qk_kk = _dot_nt(jnp.concatenate([q, k], axis=0), k)
        attn16.append((qk_kk[:tc] * decay * mask_ref[_M_CAUSAL]).astype(BF16))
        a = qk_kk[tc:] * (beta_c * decay) * mask_ref[_M_STRICT]
        a16.append(a.astype(BF16))
        t_mat.append(mask_ref[_M_EYE] - a * mask_ref[_M_LEVEL0])

    t16 = [t.astype(BF16) for t in t_mat]
    lvl = 1
    s = 2
    while s < DN_CHUNK:
        x16 = [jnp.dot(a16[h], t16[h], preferred_element_type=F32).astype(BF16) for h in hs]
        y = [jnp.dot(t16[h], x16[h], preferred_element_type=F32) for h in hs]
        t_mat = [t_mat[h] - y[h] * mask_ref[_M_LEVEL0 + lvl] for h in hs]
        t16 = [t.astype(BF16) for t in t_mat]
        s *= 2
        lvl += 1
    for h in hs:
        sol = jnp.dot(t16[h], rhs16[h], preferred_element_type=F32)
        u_all.append(sol[:, :hd])
        w_all.append(sol[:, hd:])

    st = [state_ref[h] for h in hs]
    outs = [[] for _ in hs]
    for n in range(nchunk):
        r0 = n * DN_CHUNK
        r1 = r0 + DN_CHUNK
        ws_qs = [_dot(jnp.concatenate([w_all[h][r0:r1], qg[h][r0:r1]], axis=0), st[h]) for h in hs]
        v_new = [(u_all[h][r0:r1] - ws_qs[h][:DN_CHUNK]).astype(BF16) for h in hs]
        av_kv = [jnp.dot(jnp.concatenate([attn16[h][r0:r1, r0:r1], kdec16[h][:, r0:r1]], axis=0), v_new[h],
                         preferred_element_type=F32) for h in hs]
        for h in hs:
            outs[h].append(ws_qs[h][DN_CHUNK:] + av_kv[h][:DN_CHUNK])
            st[h] = st[h] * egl_c[h][r0:r1] + av_kv[h][DN_CHUNK:]
    for h in hs:
        state_ref[h] = st[h]
        o = jnp.concatenate(outs[h], axis=0)
        z = zd_ref[0, :, hd * h:hd * (h + 1)]
        y_ref[0, :, hd * h:hd * (h + 1)] = _rms(o, ng_ref[...]) * jax.nn.silu(z)


def _gdn(qkv, zd, ba_t, a_log, dt_bias, norm_g):
    bsz, seq, d3 = qkv.shape
    d_dn = d3 // 3
    heads = d_dn // DN_HEAD_DIM
    tc = min(GDN_TILE, seq)
    nrow = ba_t.shape[1]
    assert DN_CHUNK == DN_HEAD_DIM and tc % DN_CHUNK == 0
    n_mask = _M_LEVEL0 + DN_CHUNK.bit_length() - 1
    col = lambda t: jnp.concatenate([jnp.zeros((heads,), F32), t.astype(F32)]).reshape(nrow, 1)
    full = lambda shape: pl.BlockSpec(shape, lambda b, l: (0,) * len(shape))
    return pl.pallas_call(
        functools.partial(_gdn_kernel, tc=tc, heads=heads),
        grid=(bsz, seq // tc),
        in_specs=[pl.BlockSpec((1, tc, d3), lambda b, l: (b, l, 0)),
                  pl.BlockSpec((1, tc, d_dn), lambda b, l: (b, l, 0)),
                  pl.BlockSpec((1, nrow, tc), lambda b, l: (b, 0, l)),
                  full((nrow, 1)), full((nrow, 1)), full((1, DN_HEAD_DIM))],
        out_specs=pl.BlockSpec((1, tc, d_dn), lambda b, l: (b, l, 0)),
        out_shape=jax.ShapeDtypeStruct((bsz, seq, d_dn), F32),
        scratch_shapes=[pltpu.VMEM((heads, DN_HEAD_DIM, DN_HEAD_DIM), F32),
                        pltpu.VMEM((n_mask, tc, tc), F32),
                        pltpu.VMEM((tc, 2 * tc), BF16)],
        compiler_params=pltpu.CompilerParams(dimension_semantics=("arbitrary", "arbitrary"),
                                             vmem_limit_bytes=VMEM_LIMIT),
        name="gdn",
    )(qkv, zd, ba_t, col(a_log), col(dt_bias), norm_g.astype(F32).reshape(1, DN_HEAD_DIM))


def _out_kernel(ys_ref, yd_ref, x_ref, p_ref, wos_ref, wod_ref, wp_ref, pg_ref, wg_ref, fg_ref, o_ref, *, final):
    h = x_ref[0] + _dot(ys_ref[...], wos_ref[...]) + _dot(yd_ref[0], wod_ref[...])
    e = _rms(_dot(p_ref[0], wp_ref[...]), pg_ref[...])
    h = h + jax.nn.sigmoid(_dot(h, wg_ref[...])) * e
    o_ref[0] = _rms(h, fg_ref[...]) if final else h


def _out_proj(ys_t, yd, x, p, w_out_s, w_out_d, w_ple, ple_g, w_gate, final_g, final):
    bsz, seq, dm = x.shape
    d_ssm = w_out_s.shape[0]
    d_dn = w_out_d.shape[0]
    tm = min(PROJ_TILE, seq)
    full = lambda shape: pl.BlockSpec(shape, lambda b, l: (0,) * len(shape))
    tok = lambda c: pl.BlockSpec((1, tm, c), lambda b, l: (b, l, 0))
    return pl.pallas_call(
        functools.partial(_out_kernel, final=final),
        grid=(bsz, seq // tm),
        in_specs=[pl.BlockSpec((tm, d_ssm), lambda b, l: (l, b)), tok(d_dn), tok(dm), tok(p.shape[-1]),
                  full(w_out_s.shape), full(w_out_d.shape), full(w_ple.shape), full((1, dm)),
                  full(w_gate.shape), full((1, dm))],
        out_specs=tok(dm),
        out_shape=jax.ShapeDtypeStruct((bsz, seq, dm), F32),
        compiler_params=pltpu.CompilerParams(dimension_semantics=("arbitrary", "arbitrary"),
                                             vmem_limit_bytes=VMEM_LIMIT),
        name="out_proj",
    )(ys_t, yd, x, p, w_out_s, w_out_d, w_ple, ple_g, w_gate, final_g)


def kernel(x, p, norm_mix_g, w_in, ssm_A_re, ssm_A_im, ssm_B_re, ssm_B_im, ssm_C_re, ssm_C_im, ssm_D, ssm_log_dt, ssm_w_glu, ssm_b_glu, dn_conv_w, dn_A_log, dn_dt_bias, dn_norm_g, w_out, w_ple_proj, ple_norm_g, w_ple_gate, final_norm_g):
    bsz, seq, dm = x.shape
    depth = w_in.shape[0]
    d_ssm = ssm_D.shape[-1]
    heads = dn_A_log.shape[-1]
    d_dn = heads * DN_HEAD_DIM
    n_main = 2 * d_ssm + 4 * d_dn
    row = lambda t: t.astype(F32).reshape(1, -1)
    h = x
    for i in range(depth):
        w_main = w_in[i, :, :n_main].astype(BF16)
        w_ba = jnp.pad(w_in[i, :, n_main:], ((0, 0), (0, LANES - 2 * heads))).astype(BF16)
        us_t, zs_t, qkv, zd, ba_t = _in_proj(h, row(norm_mix_g[i]), w_main, w_ba, dn_conv_w[i].astype(F32),
                                             d_ssm, d_dn)

        yd = _gdn(qkv, zd, ba_t, dn_A_log[i], dn_dt_bias[i], dn_norm_g[i])

        ab_r, ab_i, bb_r, bb_i = _s5_prep(ssm_A_re[i], ssm_A_im[i], ssm_log_dt[i], ssm_B_re[i], ssm_B_im[i])
        bw, cw, a_r, a_i = _s5_layout(ab_r, ab_i, bb_r, bb_i, ssm_C_re[i], ssm_C_im[i], bsz)
        ys_t = _s5(us_t.reshape(seq, bsz, d_ssm), zs_t.reshape(seq, bsz, d_ssm), bw, cw, a_r, a_i,
                   row(ssm_D[i]), ssm_w_glu[i].astype(BF16), row(ssm_b_glu[i]))

        h = _out_proj(ys_t.reshape(seq, bsz * d_ssm), yd, h, p[i],
                      w_out[i, :d_ssm].astype(BF16), w_out[i, d_ssm:].astype(BF16),
                      w_ple_proj[i].astype(BF16), row(ple_norm_g[i]), w_ple_gate[i].astype(BF16),
                      row(final_norm_g), final=(i == depth - 1))
    return h
```

```python
import functools

import jax
import jax.numpy as jnp
from jax import lax
from jax.experimental import pallas as pl
from jax.experimental.pallas import tpu as pltpu

F32 = jnp.float32
BF16 = jnp.bfloat16

NORM_EPS = 1e-6
SSM_GROUP = 16
SSM_STATE = 64
DN_HEAD_DIM = 128
DN_CONV = 4
DN_CHUNK = 128

LANES = 128
SSM_LANE_BLOCK = 128
SSM_STATE_BLOCK = (SSM_LANE_BLOCK // SSM_GROUP) * SSM_STATE

PROJ_TILE = 512
PROJ_SUBTILES = 2
S5_TIME_TILE = 32
GDN_TILE = 256
GDN_BATCH_BLOCK = 2
VMEM_LIMIT = 48 * 1024 * 1024


def _dot(a, b):
    return jnp.dot(a.astype(BF16), b.astype(BF16), preferred_element_type=F32)


def _dot_nt(a, b):
    return lax.dot_general(a.astype(BF16), b.astype(BF16), (((1,), (1,)), ((), ())),
                           preferred_element_type=F32)


def _rms(x, g):
    return x * lax.rsqrt(jnp.mean(x * x, axis=-1, keepdims=True) + NORM_EPS) * g


def _s5_prep_kernel(lr_ref, li_ref, ldt_ref, br_ref, bi_ref, abr_ref, abi_ref, bbr_ref, bbi_ref):
    lr = lr_ref[...]
    li = li_ref[...]
    dt = jnp.exp(ldt_ref[...])
    mag = jnp.exp(lr * dt)
    ang = li * dt
    ab_r = mag * jnp.cos(ang)
    ab_i = mag * jnp.sin(ang)
    den = lr * lr + li * li
    nr = ab_r - 1.0
    ni = ab_i
    cr = (nr * lr + ni * li) / den
    ci = (ni * lr - nr * li) / den
    abr_ref[...] = ab_r
    abi_ref[...] = ab_i
    br = br_ref[...]
    bi = bi_ref[...]
    bbr_ref[...] = cr * br - ci * bi
    bbi_ref[...] = cr * bi + ci * br


def _s5_prep(a_re, a_im, log_dt, b_re, b_im):
    g, p = a_re.shape
    h = b_re.shape[-1]
    n = g * p
    row = lambda t: t.reshape(1, n).astype(F32)
    chan = lambda t: t.astype(F32).transpose(2, 0, 1).reshape(h, n)
    ldt = jnp.broadcast_to(log_dt.astype(F32)[:, None], (g, p))
    return pl.pallas_call(
        _s5_prep_kernel,
        out_shape=(jax.ShapeDtypeStruct((1, n), F32), jax.ShapeDtypeStruct((1, n), F32),
                   jax.ShapeDtypeStruct((h, n), F32), jax.ShapeDtypeStruct((h, n), F32)),
        name="s5_prep",
    )(row(a_re), row(a_im), row(ldt), chan(b_re), chan(b_im))


def _s5_layout(bb_r, bb_i, c_re, c_im):
    g, h, p = c_re.shape
    gpb = SSM_LANE_BLOCK // h
    nb = g // gpb
    eye = jnp.eye(gpb, dtype=F32)

    def in_w(bb):
        t = bb.reshape(h, nb, gpb, p)
        return jnp.einsum('hjkp,gk->jghkp', t, eye).reshape(nb, gpb * h, gpb * p)

    def out_w(c):
        t = c.astype(F32).reshape(nb, gpb, h, p)
        return jnp.einsum('jghp,gk->jgpkh', t, eye).reshape(nb, gpb * p, gpb * h)

    bw = jnp.concatenate([in_w(bb_r), in_w(bb_i)], axis=2).astype(BF16)
    cw = jnp.concatenate([out_w(c_re), -out_w(c_im)], axis=1).astype(BF16)
    return bw, cw


def _in_proj_kernel(x_ref, g_ref, wm_ref, wba_ref, cw_ref, us_ref, zs_ref, qkv_ref, zd_ref, ba_ref,
                    tail_ref, *, d_ssm, d_dn):
    tm = x_ref.shape[1]
    hd = DN_HEAD_DIM

    @pl.when(pl.program_id(1) == 0)
    def _():
        tail_ref[...] = jnp.zeros_like(tail_ref)

    rows = tm // PROJ_SUBTILES
    tail = tail_ref[...]
    for s in range(PROJ_SUBTILES):
        r = slice(s * rows, (s + 1) * rows)
        a = _rms(x_ref[0, r], g_ref[...]).astype(BF16)
        qkv = jnp.dot(a, wm_ref[:, 2 * d_ssm:2 * d_ssm + 3 * d_dn], preferred_element_type=F32)
        us_ref[r] = jnp.dot(a, wm_ref[:, :d_ssm], preferred_element_type=F32)
        zs_ref[r] = jnp.dot(a, wm_ref[:, d_ssm:2 * d_ssm], preferred_element_type=F32)
        zd_ref[0, r] = jnp.dot(a, wm_ref[:, 2 * d_ssm + 3 * d_dn:], preferred_element_type=F32)
        ba = jnp.dot(a, wba_ref[...], preferred_element_type=F32)
        ba_ref[0, :, r] = ba.T[:ba_ref.shape[1]]

        xp = jnp.concatenate([tail, qkv], axis=0)
        tail = qkv[rows - 8:, :]
        acc = cw_ref[0:1, :] * xp
        for j in range(1, DN_CONV):
            acc = cw_ref[j:j + 1, :] * xp + pltpu.roll(acc, 1, axis=0)
        c = jax.nn.silu(acc[8:])
        for i in range(2 * d_dn // hd):
            t = c[:, hd * i:hd * (i + 1)]
            t = t * lax.rsqrt(jnp.sum(t * t, axis=-1, keepdims=True) + NORM_EPS)
            qkv_ref[0, r, hd * i:hd * (i + 1)] = t * (hd ** -0.5) if hd * i < d_dn else t
        qkv_ref[0, r, 2 * d_dn:] = c[:, 2 * d_dn:]
    tail_ref[...] = tail


def _in_proj(x, g, w_main, w_ba, conv_w, d_ssm, d_dn):
    bsz, seq, dm = x.shape
    tm = min(PROJ_TILE, seq)
    nrow = 2 * (d_dn // DN_HEAD_DIM)
    grid = (bsz, seq // tm)
    full = lambda shape: pl.BlockSpec(shape, lambda b, l: (0,) * len(shape))
    return pl.pallas_call(
        functools.partial(_in_proj_kernel, d_ssm=d_ssm, d_dn=d_dn),
        grid=grid,
        in_specs=[pl.BlockSpec((1, tm, dm), lambda b, l: (b, l, 0)),
                  full((1, dm)), full(w_main.shape), full(w_ba.shape), full(conv_w.shape)],
        out_specs=[pl.BlockSpec((tm, d_ssm), lambda b, l: (l, b)),
                   pl.BlockSpec((tm, d_ssm), lambda b, l: (l, b)),
                   pl.BlockSpec((1, tm, 3 * d_dn), lambda b, l: (b, l, 0)),
                   pl.BlockSpec((1, tm, d_dn), lambda b, l: (b, l, 0)),
                   pl.BlockSpec((1, nrow, tm), lambda b, l: (b, 0, l))],
        out_shape=(jax.ShapeDtypeStruct((seq, bsz * d_ssm), F32),
                   jax.ShapeDtypeStruct((seq, bsz * d_ssm), F32),
                   jax.ShapeDtypeStruct((bsz, seq, 3 * d_dn), F32),
                   jax.ShapeDtypeStruct((bsz, seq, d_dn), F32),
                   jax.ShapeDtypeStruct((bsz, nrow, seq), F32)),
        scratch_shapes=[pltpu.VMEM((8, 3 * d_dn), F32)],
        compiler_params=pltpu.CompilerParams(dimension_semantics=("arbitrary", "arbitrary"),
                                             vmem_limit_bytes=VMEM_LIMIT),
        name="in_proj",
    )(x, g, w_main, w_ba, conv_w)


def _s5_kernel(u_ref, z_ref, bw_ref, cw_ref, ar_ref, ai_ref, d_ref, wglu_ref, bglu_ref, y_ref,
               bu_ref, s_ref, carry_ref, *, tt, bsz, nb):
    sb = SSM_STATE_BLOCK
    rows = tt * bsz

    @pl.when(pl.program_id(0) == 0)
    def _():
        carry_ref[...] = jnp.zeros_like(carry_ref)

    u = u_ref[...].reshape(rows, nb * SSM_LANE_BLOCK)
    u16 = u.astype(BF16)
    for j in range(nb):
        bu_ref[:, 2 * sb * j:2 * sb * (j + 1)] = jnp.dot(
            u16[:, SSM_LANE_BLOCK * j:SSM_LANE_BLOCK * (j + 1)], bw_ref[j], preferred_element_type=F32)

    for j in range(nb):
        cr = 2 * sb * j
        ci = cr + sb
        a_r = jnp.broadcast_to(ar_ref[:, sb * j:sb * (j + 1)], (bsz, sb))
        a_i = jnp.broadcast_to(ai_ref[:, sb * j:sb * (j + 1)], (bsz, sb))

        def step(t, s, cr=cr, ci=ci, a_r=a_r, a_i=a_i):
            s_r, s_i = s
            r0 = pl.multiple_of(t * bsz, bsz)
            n_r = a_r * s_r - a_i * s_i + bu_ref[pl.ds(r0, bsz), cr:cr + sb]
            n_i = a_r * s_i + a_i * s_r + bu_ref[pl.ds(r0, bsz), ci:ci + sb]
            s_ref[pl.ds(r0, bsz), cr:cr + sb] = n_r.astype(BF16)
            s_ref[pl.ds(r0, bsz), ci:ci + sb] = n_i.astype(BF16)
            return n_r, n_i

        s_r, s_i = lax.fori_loop(0, tt, step, (carry_ref[:, cr:cr + sb], carry_ref[:, ci:ci + sb]), unroll=True)
        carry_ref[:, cr:cr + sb] = s_r
        carry_ref[:, ci:ci + sb] = s_i

    y = jnp.concatenate(
        [jnp.dot(s_ref[:, 2 * sb * j:2 * sb * (j + 1)], cw_ref[j], preferred_element_type=F32) for j in range(nb)],
        axis=1)
    y = y + d_ref[...] * u
    y = jax.nn.gelu(y)
    y = y * jax.nn.sigmoid(_dot(y, wglu_ref[...]) + bglu_ref[...])
    y = y * jax.nn.silu(z_ref[...].reshape(rows, nb * SSM_LANE_BLOCK))
    y_ref[...] = y.reshape(y_ref.shape)


def _s5(us_t, zs_t, bw, cw, a_r, a_i, d_vec, w_glu, b_glu):
    seq, bsz, d_ssm = us_t.shape
    nb = bw.shape[0]
    tt = min(S5_TIME_TILE, seq)
    rows = tt * bsz
    n_state = 2 * SSM_STATE_BLOCK * nb
    full = lambda shape: pl.BlockSpec(shape, lambda t: (0,) * len(shape))
    blk = pl.BlockSpec((tt, bsz, d_ssm), lambda t: (t, 0, 0))
    return pl.pallas_call(
        functools.partial(_s5_kernel, tt=tt, bsz=bsz, nb=nb),
        grid=(seq // tt,),
        in_specs=[blk, blk, full(bw.shape), full(cw.shape), full(a_r.shape), full(a_i.shape),
                  full(d_vec.shape), full(w_glu.shape), full(b_glu.shape)],
        out_specs=blk,
        out_shape=jax.ShapeDtypeStruct((seq, bsz, d_ssm), F32),
        scratch_shapes=[pltpu.VMEM((rows, n_state), F32),
                        pltpu.VMEM((rows, n_state), BF16),
                        pltpu.VMEM((bsz, n_state), F32)],
        compiler_params=pltpu.CompilerParams(dimension_semantics=("arbitrary",),
                                             vmem_limit_bytes=VMEM_LIMIT),
        name="s5",
    )(us_t, zs_t, bw, cw, a_r, a_i, d_vec, w_glu, b_glu)


def _chunk_sums(g, sum_ref):
    n = g.shape[0]
    hi = g.astype(BF16).astype(F32)
    mid = (g - hi).astype(BF16).astype(F32)
    lo = g - hi - mid
    parts = jnp.concatenate([hi, mid, lo], axis=0).astype(BF16)
    s = jnp.dot(parts, sum_ref[...], preferred_element_type=F32)
    s = s[:n] + s[n:2 * n] + s[2 * n:]
    tc = g.shape[1]
    return s[:, :tc], s[:, tc:]


_M_CAUSAL, _M_STRICT, _M_EYE, _M_PAIR = 0, 1, 2, 3


def _gdn_fill_masks(mask_ref, lvl_ref, sum_ref, tc):
    ri = lax.broadcasted_iota(jnp.int32, (tc, tc), 0)
    ci = lax.broadcasted_iota(jnp.int32, (tc, tc), 1)
    same = (ri // DN_CHUNK) == (ci // DN_CHUNK)
    mask_ref[_M_CAUSAL] = (same & (ri >= ci)).astype(F32)
    mask_ref[_M_STRICT] = (same & (ri > ci)).astype(F32)
    mask_ref[_M_EYE] = (ri == ci).astype(F32)
    sum_ref[:, :tc] = (same & (ri <= ci)).astype(F32).astype(BF16)
    sum_ref[:, tc:] = (same & (ri > ci)).astype(F32).astype(BF16)
    s = 1
    lvl = 0
    while s < DN_CHUNK:
        m = (((ri // (2 * s)) == (ci // (2 * s))) & (((ri // s) % 2) == 1) & (((ci // s) % 2) == 0)).astype(F32)
        if lvl == 0:
            mask_ref[_M_PAIR] = m
        else:
            lvl_ref[lvl - 1] = m.astype(BF16)
        s *= 2
        lvl += 1


def _gdn_prep(bb, qkv_ref, ba_ref, alog_ref, dtb_ref, mask_ref, sum_ref, tc, heads):
    hd = DN_HEAD_DIM
    d_dn = heads * hd
    ba = ba_ref[bb]
    beta_r = jax.nn.sigmoid(ba)
    g_r = -jnp.exp(alog_ref[...]) * jax.nn.softplus(ba + dtb_ref[...])
    gc_r, gs_r = _chunk_sums(g_r, sum_ref)
    egs_r = jnp.exp(gs_r)
    egl_r = jnp.exp(gc_r + gs_r)
    pad = jnp.zeros((LANES - 3 * 2 * heads, tc), F32)
    cols = jnp.concatenate([beta_r, gc_r, egl_r, pad], axis=0).T
    out = []
    for h in range(heads):
        q = qkv_ref[bb, :, hd * h:hd * (h + 1)]
        k = qkv_ref[bb, :, d_dn + hd * h:d_dn + hd * (h + 1)]
        v = qkv_ref[bb, :, 2 * d_dn + hd * h:2 * d_dn + hd * (h + 1)]
        beta_c = cols[:, h:h + 1]
        gc_c = cols[:, 3 * heads + h:3 * heads + h + 1]
        gc_row = gc_r[heads + h:heads + h + 1, :]
        eg_c = jnp.exp(gc_c)
        kb = k * beta_c
        decay = jnp.exp(jnp.minimum(gc_c - gc_row, 0.0))
        qk_kk = _dot_nt(jnp.concatenate([q, k], axis=0), k)
        a = qk_kk[tc:] * (beta_c * decay) * mask_ref[_M_STRICT]
        out.append(dict(
            egl_c=cols[:, 5 * heads + h:5 * heads + h + 1],
            qg16=(q * eg_c).astype(BF16),
            rhs16=jnp.concatenate([v * beta_c, kb * eg_c], axis=1).astype(BF16),
            kdec16=(k.T * egs_r[heads + h:heads + h + 1, :]).astype(BF16),
            attn16=(qk_kk[:tc] * decay * mask_ref[_M_CAUSAL]).astype(BF16),
            a16=a.astype(BF16),
            t16=(mask_ref[_M_EYE] - a * mask_ref[_M_PAIR]).astype(BF16)))
    return out


def _gdn_kernel(qkv_ref, zd_ref, ba_ref, alog_ref, dtb_ref, ng_ref, y_ref, state_ref, mask_ref, lvl_ref, sum_ref,
                *, tc, heads, nbb):
    hd = DN_HEAD_DIM
    nchunk = tc // DN_CHUNK

    @pl.when(pl.program_id(1) == 0)
    def _():
        state_ref[...] = jnp.zeros_like(state_ref)
        _gdn_fill_masks(mask_ref, lvl_ref, sum_ref, tc)

    units = []
    for bb in range(nbb):
        for h, d in enumerate(_gdn_prep(bb, qkv_ref, ba_ref, alog_ref, dtb_ref, mask_ref, sum_ref, tc, heads)):
            units.append(dict(d, bb=bb, h=h))

    for lvl in range(lvl_ref.shape[0]):
        for u in units:
            u["x16"] = jnp.dot(u["a16"], u["t16"], preferred_element_type=F32).astype(BF16)
        for u in units:
            y16 = jnp.dot(u["t16"], u["x16"], preferred_element_type=F32).astype(BF16)
            u["t16"] = u["t16"] - y16 * lvl_ref[lvl]
    for u in units:
        sol = jnp.dot(u["t16"], u["rhs16"], preferred_element_type=F32)
        u["u"] = sol[:, :hd]
        u["w16"] = sol[:, hd:].astype(BF16)
        u["st"] = state_ref[u["bb"] * heads + u["h"]]
        u["outs"] = []

    for n in range(nchunk):
        r0 = n * DN_CHUNK
        r1 = r0 + DN_CHUNK
        for u in units:
            u["ws_qs"] = jnp.dot(jnp.concatenate([u["w16"][r0:r1], u["qg16"][r0:r1]], axis=0),
                                 u["st"].astype(BF16), preferred_element_type=F32)
        for u in units:
            v_new = (u["u"][r0:r1] - u["ws_qs"][:DN_CHUNK]).astype(BF16)
            u["av_kv"] = jnp.dot(jnp.concatenate([u["attn16"][r0:r1, r0:r1], u["kdec16"][:, r0:r1]], axis=0),
                                 v_new, preferred_element_type=F32)
        for u in units:
            u["outs"].append(u["ws_qs"][DN_CHUNK:] + u["av_kv"][:DN_CHUNK])
            u["st"] = u["st"] * u["egl_c"][r0:r1] + u["av_kv"][DN_CHUNK:]
    for u in units:
        bb, h = u["bb"], u["h"]
        state_ref[bb * heads + h] = u["st"]
        o = jnp.concatenate(u["outs"], axis=0)
        z = zd_ref[bb, :, hd * h:hd * (h + 1)]
        y_ref[bb, :, hd * h:hd * (h + 1)] = _rms(o, ng_ref[...]) * jax.nn.silu(z)


def _gdn(qkv, zd, ba_t, a_log, dt_bias, norm_g):
    bsz, seq, d3 = qkv.shape
    d_dn = d3 // 3
    heads = d_dn // DN_HEAD_DIM
    tc = min(GDN_TILE, seq)
    nbb = GDN_BATCH_BLOCK
    nrow = ba_t.shape[1]
    assert DN_CHUNK == DN_HEAD_DIM and tc % DN_CHUNK == 0 and bsz % nbb == 0
    n_lvl = DN_CHUNK.bit_length() - 2
    col = lambda t: jnp.concatenate([jnp.zeros((heads,), F32), t.astype(F32)]).reshape(nrow, 1)
    full = lambda shape: pl.BlockSpec(shape, lambda b, l: (0,) * len(shape))
    return pl.pallas_call(
        functools.partial(_gdn_kernel, tc=tc, heads=heads, nbb=nbb),
        grid=(bsz // nbb, seq // tc),
        in_specs=[pl.BlockSpec((nbb, tc, d3), lambda b, l: (b, l, 0)),
                  pl.BlockSpec((nbb, tc, d_dn), lambda b, l: (b, l, 0)),
                  pl.BlockSpec((nbb, nrow, tc), lambda b, l: (b, 0, l)),
                  full((nrow, 1)), full((nrow, 1)), full((1, DN_HEAD_DIM))],
        out_specs=pl.BlockSpec((nbb, tc, d_dn), lambda b, l: (b, l, 0)),
        out_shape=jax.ShapeDtypeStruct((bsz, seq, d_dn), F32),
        scratch_shapes=[pltpu.VMEM((nbb * heads, DN_HEAD_DIM, DN_HEAD_DIM), F32),
                        pltpu.VMEM((_M_PAIR + 1, tc, tc), F32),
                        pltpu.VMEM((n_lvl, tc, tc), BF16),
                        pltpu.VMEM((tc, 2 * tc), BF16)],
        compiler_params=pltpu.CompilerParams(dimension_semantics=("arbitrary", "arbitrary"),
                                             vmem_limit_bytes=VMEM_LIMIT),
        name="gdn",
    )(qkv, zd, ba_t, col(a_log), col(dt_bias), norm_g.astype(F32).reshape(1, DN_HEAD_DIM))


def _out_kernel(ys_ref, yd_ref, x_ref, p_ref, wos_ref, wod_ref, wp_ref, pg_ref, wg_ref, fg_ref, o_ref, *, final):
    tm = x_ref.shape[1]
    rows = tm // PROJ_SUBTILES
    for s in range(PROJ_SUBTILES):
        r = slice(s * rows, (s + 1) * rows)
        h = x_ref[0, r] + _dot(ys_ref[r], wos_ref[...]) + _dot(yd_ref[0, r], wod_ref[...])
        e = _rms(_dot(p_ref[0, r], wp_ref[...]), pg_ref[...])
        h = h + jax.nn.sigmoid(_dot(h, wg_ref[...])) * e
        o_ref[0, r] = _rms(h, fg_ref[...]) if final else h


def _out_proj(ys_t, yd, x, p, w_out_s, w_out_d, w_ple, ple_g, w_gate, final_g, final):
    bsz, seq, dm = x.shape
    d_ssm = w_out_s.shape[0]
    d_dn = w_out_d.shape[0]
    tm = min(PROJ_TILE, seq)
    full = lambda shape: pl.BlockSpec(shape, lambda b, l: (0,) * len(shape))
    tok = lambda c: pl.BlockSpec((1, tm, c), lambda b, l: (b, l, 0))
    return pl.pallas_call(
        functools.partial(_out_kernel, final=final),
        grid=(bsz, seq // tm),
        in_specs=[pl.BlockSpec((tm, d_ssm), lambda b, l: (l, b)), tok(d_dn), tok(dm), tok(p.shape[-1]),
                  full(w_out_s.shape), full(w_out_d.shape), full(w_ple.shape), full((1, dm)),
                  full(w_gate.shape), full((1, dm))],
        out_specs=tok(dm),
        out_shape=jax.ShapeDtypeStruct((bsz, seq, dm), F32),
        compiler_params=pltpu.CompilerParams(dimension_semantics=("arbitrary", "arbitrary"),
                                             vmem_limit_bytes=VMEM_LIMIT),
        name="out_proj",
    )(ys_t, yd, x, p, w_out_s, w_out_d, w_ple, ple_g, w_gate, final_g)


def kernel(x, p, norm_mix_g, w_in, ssm_A_re, ssm_A_im, ssm_B_re, ssm_B_im, ssm_C_re, ssm_C_im, ssm_D, ssm_log_dt, ssm_w_glu, ssm_b_glu, dn_conv_w, dn_A_log, dn_dt_bias, dn_norm_g, w_out, w_ple_proj, ple_norm_g, w_ple_gate, final_norm_g):
    bsz, seq, dm = x.shape
    depth = w_in.shape[0]
    d_ssm = ssm_D.shape[-1]
    heads = dn_A_log.shape[-1]
    d_dn = heads * DN_HEAD_DIM
    n_main = 2 * d_ssm + 4 * d_dn
    row = lambda t: t.astype(F32).reshape(1, -1)
    h = x
    for i in range(depth):
        w_main = w_in[i, :, :n_main].astype(BF16)
        w_ba = jnp.pad(w_in[i, :, n_main:], ((0, 0), (0, LANES - 2 * heads))).astype(BF16)
        us_t, zs_t, qkv, zd, ba_t = _in_proj(h, row(norm_mix_g[i]), w_main, w_ba, dn_conv_w[i].astype(F32),
                                             d_ssm, d_dn)

        yd = _gdn(qkv, zd, ba_t, dn_A_log[i], dn_dt_bias[i], dn_norm_g[i])

        a_r, a_i, bb_r, bb_i = _s5_prep(ssm_A_re[i], ssm_A_im[i], ssm_log_dt[i], ssm_B_re[i], ssm_B_im[i])
        bw, cw = _s5_layout(bb_r, bb_i, ssm_C_re[i], ssm_C_im[i])
        ys_t = _s5(us_t.reshape(seq, bsz, d_ssm), zs_t.reshape(seq, bsz, d_ssm), bw, cw, a_r, a_i,
                   row(ssm_D[i]), ssm_w_glu[i].astype(BF16), row(ssm_b_glu[i]))

        h = _out_proj(ys_t.reshape(seq, bsz * d_ssm), yd, h, p[i],
                      w_out[i, :d_ssm].astype(BF16), w_out[i, d_ssm:].astype(BF16),
                      w_ple_proj[i].astype(BF16), row(ple_norm_g[i]), w_ple_gate[i].astype(BF16),
                      row(final_norm_g), final=(i == depth - 1))
    return h
```

```python
import functools

import jax
import jax.numpy as jnp
from jax import lax
from jax.experimental import pallas as pl
from jax.experimental.pallas import tpu as pltpu

F32 = jnp.float32
BF16 = jnp.bfloat16

NORM_EPS = 1e-6
SSM_GROUP = 16
SSM_STATE = 64
DN_HEAD_DIM = 128
DN_CONV = 4
DN_CHUNK = 128

LANES = 128
SSM_LANE_BLOCK = 128
SSM_STATE_BLOCK = (SSM_LANE_BLOCK // SSM_GROUP) * SSM_STATE

PROJ_TILE = 512
PROJ_SUBTILES = 2
S5_TIME_TILE = 32
GDN_TILE = 256
GDN_BATCH_BLOCK = 2
VMEM_LIMIT = 48 * 1024 * 1024


def _dot(a, b):
    return jnp.dot(a.astype(BF16), b.astype(BF16), preferred_element_type=F32)


def _dot_nt(a, b):
    return lax.dot_general(a.astype(BF16), b.astype(BF16), (((1,), (1,)), ((), ())),
                           preferred_element_type=F32)


def _rms(x, g):
    return x * lax.rsqrt(jnp.mean(x * x, axis=-1, keepdims=True) + NORM_EPS) * g


def _s5_prep_kernel(lr_ref, li_ref, ldt_ref, br_ref, bi_ref, abr_ref, abi_ref, bbr_ref, bbi_ref):
    lr = lr_ref[...]
    li = li_ref[...]
    dt = jnp.exp(ldt_ref[...])
    mag = jnp.exp(lr * dt)
    ang = li * dt
    ab_r = mag * jnp.cos(ang)
    ab_i = mag * jnp.sin(ang)
    den = lr * lr + li * li
    nr = ab_r - 1.0
    ni = ab_i
    cr = (nr * lr + ni * li) / den
    ci = (ni * lr - nr * li) / den
    abr_ref[...] = ab_r
    abi_ref[...] = ab_i
    br = br_ref[...]
    bi = bi_ref[...]
    bbr_ref[...] = cr * br - ci * bi
    bbi_ref[...] = cr * bi + ci * br


def _s5_prep(a_re, a_im, log_dt, b_re, b_im):
    g, p = a_re.shape
    h = b_re.shape[-1]
    n = g * p
    row = lambda t: t.reshape(1, n).astype(F32)
    chan = lambda t: t.astype(F32).transpose(2, 0, 1).reshape(h, n)
    ldt = jnp.broadcast_to(log_dt.astype(F32)[:, None], (g, p))
    return pl.pallas_call(
        _s5_prep_kernel,
        out_shape=(jax.ShapeDtypeStruct((1, n), F32), jax.ShapeDtypeStruct((1, n), F32),
                   jax.ShapeDtypeStruct((h, n), F32), jax.ShapeDtypeStruct((h, n), F32)),
        name="s5_prep",
    )(row(a_re), row(a_im), row(ldt), chan(b_re), chan(b_im))


def _s5_layout(bb_r, bb_i, c_re, c_im):
    g, h, p = c_re.shape
    gpb = SSM_LANE_BLOCK // h
    nb = g // gpb
    eye = jnp.eye(gpb, dtype=F32)

    def in_w(bb):
        t = bb.reshape(h, nb, gpb, p)
        return jnp.einsum('hjkp,gk->jghkp', t, eye).reshape(nb, gpb * h, gpb * p)

    def out_w(c):
        t = c.astype(F32).reshape(nb, gpb, h, p)
        return jnp.einsum('jghp,gk->jgpkh', t, eye).reshape(nb, gpb * p, gpb * h)

    bw = jnp.concatenate([in_w(bb_r), in_w(bb_i)], axis=2).astype(BF16)
    cw = jnp.concatenate([out_w(c_re), -out_w(c_im)], axis=1).astype(BF16)
    return bw, cw


def _in_proj_kernel(x_ref, g_ref, wm_ref, wba_ref, cw_ref, cs_ref, us_ref, zs_ref, qkv_ref, zd_ref, ba_ref,
                    tail_ref, xpad_ref, *, d_ssm, d_dn):
    tm = x_ref.shape[1]
    hd = DN_HEAD_DIM

    @pl.when(pl.program_id(1) == 0)
    def _():
        tail_ref[...] = jnp.zeros_like(tail_ref)

    a = _rms(x_ref[0], g_ref[...]).astype(BF16)
    qkv = jnp.dot(a, wm_ref[:, 2 * d_ssm:2 * d_ssm + 3 * d_dn], preferred_element_type=F32)
    for s in range(3 * d_dn // LANES):
        xpad_ref[s, 0:8, :] = tail_ref[s]
        xpad_ref[s, 8:, :] = qkv[:, LANES * s:LANES * (s + 1)]
        tail_ref[s] = qkv[tm - 8:, LANES * s:LANES * (s + 1)]
    us_ref[0] = jnp.dot(a, wm_ref[:, :d_ssm], preferred_element_type=F32)
    zs_ref[0] = jnp.dot(a, wm_ref[:, d_ssm:2 * d_ssm], preferred_element_type=F32)
    zd_ref[0] = jnp.dot(a, wm_ref[:, 2 * d_ssm + 3 * d_dn:], preferred_element_type=F32)
    ba = jnp.dot(a, wba_ref[...], preferred_element_type=F32)
    ba_ref[0] = ba.T[:ba_ref.shape[1]]

    for s in range(3 * d_dn // LANES):
        lanes = slice(LANES * s, LANES * (s + 1))
        acc = cw_ref[DN_CONV - 1:DN_CONV, lanes] * xpad_ref[s, 8:, :]
        for j in range(DN_CONV - 1):
            acc = acc + cw_ref[j:j + 1, lanes] * xpad_ref[s, pl.ds(cs_ref[j], tm), :]
        t = jax.nn.silu(acc)
        if LANES * s < 2 * d_dn:
            t = t * lax.rsqrt(jnp.sum(t * t, axis=-1, keepdims=True) + NORM_EPS)
            if LANES * s < d_dn:
                t = t * (hd ** -0.5)
        qkv_ref[0, :, lanes] = t


def _in_proj(x, g, w_main, w_ba, conv_w, d_ssm, d_dn):
    bsz, seq, dm = x.shape
    tm = min(PROJ_TILE, seq)
    nrow = 2 * (d_dn // DN_HEAD_DIM)
    grid = (bsz, seq // tm)
    full = lambda shape: pl.BlockSpec(shape, lambda b, l: (0,) * len(shape))
    return pl.pallas_call(
        functools.partial(_in_proj_kernel, d_ssm=d_ssm, d_dn=d_dn),
        grid=grid,
        in_specs=[pl.BlockSpec((1, tm, dm), lambda b, l: (b, l, 0)),
                  full((1, dm)), full(w_main.shape), full(w_ba.shape), full(conv_w.shape),
                  pl.BlockSpec(memory_space=pltpu.SMEM)],
        out_specs=[pl.BlockSpec((1, tm, d_ssm), lambda b, l: (b, l, 0)),
                   pl.BlockSpec((1, tm, d_ssm), lambda b, l: (b, l, 0)),
                   pl.BlockSpec((1, tm, 3 * d_dn), lambda b, l: (b, l, 0)),
                   pl.BlockSpec((1, tm, d_dn), lambda b, l: (b, l, 0)),
                   pl.BlockSpec((1, nrow, tm), lambda b, l: (b, 0, l))],
        out_shape=(jax.ShapeDtypeStruct((bsz, seq, d_ssm), F32),
                   jax.ShapeDtypeStruct((bsz, seq, d_ssm), F32),
                   jax.ShapeDtypeStruct((bsz, seq, 3 * d_dn), F32),
                   jax.ShapeDtypeStruct((bsz, seq, d_dn), F32),
                   jax.ShapeDtypeStruct((bsz, nrow, seq), F32)),
        scratch_shapes=[pltpu.VMEM((3 * d_dn // LANES, 8, LANES), F32),
                        pltpu.VMEM((3 * d_dn // LANES, 8 + tm, LANES), F32)],
        compiler_params=pltpu.CompilerParams(dimension_semantics=("arbitrary", "arbitrary"),
                                             vmem_limit_bytes=VMEM_LIMIT),
        name="in_proj",
    )(x, g, w_main, w_ba, conv_w, 8 - (DN_CONV - 1) + jnp.arange(DN_CONV - 1, dtype=jnp.int32))


def _s5_kernel(u_hbm, z_hbm, bw_ref, cw_ref, ar_ref, ai_ref, d_ref, wglu_ref, bglu_ref, y_hbm,
               ubuf, zbuf, ybuf, sem_in, sem_out, bu_ref, s_ref, carry_ref, *, tt, bsz, nb, nsteps):
    sb = SSM_STATE_BLOCK
    rows = tt * bsz
    t = pl.program_id(0)
    slot = t % 2

    def in_copies(step, sl):
        cps = []
        for b in range(bsz):
            cps.append(pltpu.make_async_copy(u_hbm.at[b, pl.ds(step * tt, tt), :], ubuf.at[sl, :, b, :],
                                             sem_in.at[sl, 0]))
            cps.append(pltpu.make_async_copy(z_hbm.at[b, pl.ds(step * tt, tt), :], zbuf.at[sl, :, b, :],
                                             sem_in.at[sl, 1]))
        return cps

    def out_copies(step, sl):
        return [pltpu.make_async_copy(ybuf.at[sl, :, b, :], y_hbm.at[b, pl.ds(step * tt, tt), :], sem_out.at[sl])
                for b in range(bsz)]

    @pl.when(t == 0)
    def _():
        carry_ref[...] = jnp.zeros_like(carry_ref)
        for c in in_copies(0, 0):
            c.start()

    @pl.when(t + 1 < nsteps)
    def _():
        for c in in_copies(t + 1, 1 - slot):
            c.start()

    for c in in_copies(t, slot):
        c.wait()

    u = ubuf[slot].reshape(rows, nb * SSM_LANE_BLOCK)
    u16 = u.astype(BF16)
    for j in range(nb):
        bu_ref[:, 2 * sb * j:2 * sb * (j + 1)] = jnp.dot(
            u16[:, SSM_LANE_BLOCK * j:SSM_LANE_BLOCK * (j + 1)], bw_ref[j], preferred_element_type=F32)

    for j in range(nb):
        cr = 2 * sb * j
        ci = cr + sb
        a_r = jnp.broadcast_to(ar_ref[:, sb * j:sb * (j + 1)], (bsz, sb))
        a_i = jnp.broadcast_to(ai_ref[:, sb * j:sb * (j + 1)], (bsz, sb))

        def step(i, s, cr=cr, ci=ci, a_r=a_r, a_i=a_i):
            s_r, s_i = s
            r0 = pl.multiple_of(i * bsz, bsz)
            n_r = a_r * s_r - a_i * s_i + bu_ref[pl.ds(r0, bsz), cr:cr + sb]
            n_i = a_r * s_i + a_i * s_r + bu_ref[pl.ds(r0, bsz), ci:ci + sb]
            s_ref[pl.ds(r0, bsz), cr:cr + sb] = n_r.astype(BF16)
            s_ref[pl.ds(r0, bsz), ci:ci + sb] = n_i.astype(BF16)
            return n_r, n_i

        s_r, s_i = lax.fori_loop(0, tt, step, (carry_ref[:, cr:cr + sb], carry_ref[:, ci:ci + sb]), unroll=True)
        carry_ref[:, cr:cr + sb] = s_r
        carry_ref[:, ci:ci + sb] = s_i

    y = jnp.concatenate(
        [jnp.dot(s_ref[:, 2 * sb * j:2 * sb * (j + 1)], cw_ref[j], preferred_element_type=F32) for j in range(nb)],
        axis=1)
    y = y + d_ref[...] * u
    y = jax.nn.gelu(y)
    y = y * jax.nn.sigmoid(_dot(y, wglu_ref[...]) + bglu_ref[...])
    y = y * jax.nn.silu(zbuf[slot].reshape(rows, nb * SSM_LANE_BLOCK))

    @pl.when(t >= 2)
    def _():
        for c in out_copies(t - 2, slot):
            c.wait()

    ybuf[slot] = y.reshape(ybuf.shape[1:])
    for c in out_copies(t, slot):
        c.start()

    @pl.when(t == nsteps - 1)
    def _():
        for c in out_copies(t, slot):
            c.wait()
        if nsteps >= 2:
            for c in out_copies(t - 1, 1 - slot):
                c.wait()


def _s5(us, zs, bw, cw, a_r, a_i, d_vec, w_glu, b_glu):
    bsz, seq, d_ssm = us.shape
    nb = bw.shape[0]
    tt = min(S5_TIME_TILE, seq)
    rows = tt * bsz
    nsteps = seq // tt
    n_state = 2 * SSM_STATE_BLOCK * nb
    full = lambda shape: pl.BlockSpec(shape, lambda t: (0,) * len(shape))
    hbm = pl.BlockSpec(memory_space=pl.ANY)
    return pl.pallas_call(
        functools.partial(_s5_kernel, tt=tt, bsz=bsz, nb=nb, nsteps=nsteps),
        grid=(nsteps,),
        in_specs=[hbm, hbm, full(bw.shape), full(cw.shape), full(a_r.shape), full(a_i.shape),
                  full(d_vec.shape), full(w_glu.shape), full(b_glu.shape)],
        out_specs=hbm,
        out_shape=jax.ShapeDtypeStruct((bsz, seq, d_ssm), F32),
        scratch_shapes=[pltpu.VMEM((2, tt, bsz, d_ssm), F32),
                        pltpu.VMEM((2, tt, bsz, d_ssm), F32),
                        pltpu.VMEM((2, tt, bsz, d_ssm), F32),
                        pltpu.SemaphoreType.DMA((2, 2)),
                        pltpu.SemaphoreType.DMA((2,)),
                        pltpu.VMEM((rows, n_state), F32),
                        pltpu.VMEM((rows, n_state), BF16),
                        pltpu.VMEM((bsz, n_state), F32)],
        compiler_params=pltpu.CompilerParams(dimension_semantics=("arbitrary",),
                                             vmem_limit_bytes=VMEM_LIMIT),
        name="s5",
    )(us, zs, bw, cw, a_r, a_i, d_vec, w_glu, b_glu)


def _chunk_sums(g, sum_ref):
    n = g.shape[0]
    hi = g.astype(BF16).astype(F32)
    mid = (g - hi).astype(BF16).astype(F32)
    lo = g - hi - mid
    parts = jnp.concatenate([hi, mid, lo], axis=0).astype(BF16)
    s = jnp.dot(parts, sum_ref[...], preferred_element_type=F32)
    s = s[:n] + s[n:2 * n] + s[2 * n:]
    tc = g.shape[1]
    return s[:, :tc], s[:, tc:]


_M_CAUSAL, _M_STRICT, _M_EYE, _M_PAIR = 0, 1, 2, 3


def _gdn_fill_masks(mask_ref, lvl_ref, sum_ref, tc):
    ri = lax.broadcasted_iota(jnp.int32, (tc, tc), 0)
    ci = lax.broadcasted_iota(jnp.int32, (tc, tc), 1)
    same = (ri // DN_CHUNK) == (ci // DN_CHUNK)
    mask_ref[_M_CAUSAL] = (same & (ri >= ci)).astype(F32)
    mask_ref[_M_STRICT] = (same & (ri > ci)).astype(F32)
    mask_ref[_M_EYE] = (ri == ci).astype(F32)
    sum_ref[:, :tc] = (same & (ri <= ci)).astype(F32).astype(BF16)
    sum_ref[:, tc:] = (same & (ri > ci)).astype(F32).astype(BF16)
    s = 1
    lvl = 0
    while s < DN_CHUNK:
        m = (((ri // (2 * s)) == (ci // (2 * s))) & (((ri // s) % 2) == 1) & (((ci // s) % 2) == 0)).astype(F32)
        if lvl == 0:
            mask_ref[_M_PAIR] = m
        else:
            lvl_ref[lvl - 1] = m.astype(BF16)
        s *= 2
        lvl += 1


def _gdn_prep(bb, qkv_ref, ba_ref, alog_ref, dtb_ref, mask_ref, sum_ref, tc, heads):
    hd = DN_HEAD_DIM
    d_dn = heads * hd
    ba = ba_ref[bb]
    beta_r = jax.nn.sigmoid(ba)
    g_r = -jnp.exp(alog_ref[...]) * jax.nn.softplus(ba + dtb_ref[...])
    gc_r, gs_r = _chunk_sums(g_r, sum_ref)
    egs_r = jnp.exp(gs_r)
    egl_r = jnp.exp(gc_r + gs_r)
    pad = jnp.zeros((LANES - 3 * 2 * heads, tc), F32)
    cols = jnp.concatenate([beta_r, gc_r, egl_r, pad], axis=0).T
    out = []
    for h in range(heads):
        q = qkv_ref[bb, :, hd * h:hd * (h + 1)]
        k = qkv_ref[bb, :, d_dn + hd * h:d_dn + hd * (h + 1)]
        v = qkv_ref[bb, :, 2 * d_dn + hd * h:2 * d_dn + hd * (h + 1)]
        beta_c = cols[:, h:h + 1]
        gc_c = cols[:, 3 * heads + h:3 * heads + h + 1]
        gc_row = gc_r[heads + h:heads + h + 1, :]
        eg_c = jnp.exp(gc_c)
        kb = k * beta_c
        decay = jnp.exp(jnp.minimum(gc_c - gc_row, 0.0))
        qk_kk = _dot_nt(jnp.concatenate([q, k], axis=0), k)
        a = qk_kk[tc:] * (beta_c * decay) * mask_ref[_M_STRICT]
        out.append(dict(
            egl_c=cols[:, 5 * heads + h:5 * heads + h + 1],
            qg16=(q * eg_c).astype(BF16),
            rhs16=jnp.concatenate([v * beta_c, kb * eg_c], axis=1).astype(BF16),
            kdec16=(k.T * egs_r[heads + h:heads + h + 1, :]).astype(BF16),
            attn16=(qk_kk[:tc] * decay * mask_ref[_M_CAUSAL]).astype(BF16),
            a16=a.astype(BF16),
            t16=(mask_ref[_M_EYE] - a * mask_ref[_M_PAIR]).astype(BF16)))
    return out


def _gdn_kernel(qkv_ref, zd_ref, ba_ref, alog_ref, dtb_ref, ng_ref, y_ref, state_ref, mask_ref, lvl_ref, sum_ref,
                *, tc, heads, nbb):
    hd = DN_HEAD_DIM
    nchunk = tc // DN_CHUNK

    @pl.when(pl.program_id(1) == 0)
    def _():
        state_ref[...] = jnp.zeros_like(state_ref)
        _gdn_fill_masks(mask_ref, lvl_ref, sum_ref, tc)

    units = []
    for bb in range(nbb):
        for h, d in enumerate(_gdn_prep(bb, qkv_ref, ba_ref, alog_ref, dtb_ref, mask_ref, sum_ref, tc, heads)):
            units.append(dict(d, bb=bb, h=h))

    for lvl in range(lvl_ref.shape[0]):
        for u in units:
            u["x16"] = jnp.dot(u["a16"], u["t16"], preferred_element_type=F32).astype(BF16)
        for u in units:
            y16 = jnp.dot(u["t16"], u["x16"], preferred_element_type=F32).astype(BF16)
            u["t16"] = u["t16"] - y16 * lvl_ref[lvl]
    for u in units:
        sol = jnp.dot(u["t16"], u["rhs16"], preferred_element_type=F32)
        u["u"] = sol[:, :hd]
        u["w16"] = sol[:, hd:].astype(BF16)
        u["st"] = state_ref[u["bb"] * heads + u["h"]]
        u["outs"] = []

    for n in range(nchunk):
        r0 = n * DN_CHUNK
        r1 = r0 + DN_CHUNK
        for u in units:
            u["ws_qs"] = jnp.dot(jnp.concatenate([u["w16"][r0:r1], u["qg16"][r0:r1]], axis=0),
                                 u["st"].astype(BF16), preferred_element_type=F32)
        for u in units:
            v_new = (u["u"][r0:r1] - u["ws_qs"][:DN_CHUNK]).astype(BF16)
            u["av_kv"] = jnp.dot(jnp.concatenate([u["attn16"][r0:r1, r0:r1], u["kdec16"][:, r0:r1]], axis=0),
                                 v_new, preferred_element_type=F32)
        for u in units:
            u["outs"].append(u["ws_qs"][DN_CHUNK:] + u["av_kv"][:DN_CHUNK])
            u["st"] = u["st"] * u["egl_c"][r0:r1] + u["av_kv"][DN_CHUNK:]
    for u in units:
        bb, h = u["bb"], u["h"]
        state_ref[bb * heads + h] = u["st"]
        o = jnp.concatenate(u["outs"], axis=0)
        z = zd_ref[bb, :, hd * h:hd * (h + 1)]
        y_ref[bb, :, hd * h:hd * (h + 1)] = _rms(o, ng_ref[...]) * jax.nn.silu(z)


def _gdn(qkv, zd, ba_t, a_log, dt_bias, norm_g):
    bsz, seq, d3 = qkv.shape
    d_dn = d3 // 3
    heads = d_dn // DN_HEAD_DIM
    tc = min(GDN_TILE, seq)
    nbb = GDN_BATCH_BLOCK
    nrow = ba_t.shape[1]
    assert DN_CHUNK == DN_HEAD_DIM and tc % DN_CHUNK == 0 and bsz % nbb == 0
    n_lvl = DN_CHUNK.bit_length() - 2
    col = lambda t: jnp.concatenate([jnp.zeros((heads,), F32), t.astype(F32)]).reshape(nrow, 1)
    full = lambda shape: pl.BlockSpec(shape, lambda b, l: (0,) * len(shape))
    return pl.pallas_call(
        functools.partial(_gdn_kernel, tc=tc, heads=heads, nbb=nbb),
        grid=(bsz // nbb, seq // tc),
        in_specs=[pl.BlockSpec((nbb, tc, d3), lambda b, l: (b, l, 0)),
                  pl.BlockSpec((nbb, tc, d_dn), lambda b, l: (b, l, 0)),
                  pl.BlockSpec((nbb, nrow, tc), lambda b, l: (b, 0, l)),
                  full((nrow, 1)), full((nrow, 1)), full((1, DN_HEAD_DIM))],
        out_specs=pl.BlockSpec((nbb, tc, d_dn), lambda b, l: (b, l, 0)),
        out_shape=jax.ShapeDtypeStruct((bsz, seq, d_dn), F32),
        scratch_shapes=[pltpu.VMEM((nbb * heads, DN_HEAD_DIM, DN_HEAD_DIM), F32),
                        pltpu.VMEM((_M_PAIR + 1, tc, tc), F32),
                        pltpu.VMEM((n_lvl, tc, tc), BF16),
                        pltpu.VMEM((tc, 2 * tc), BF16)],
        compiler_params=pltpu.CompilerParams(dimension_semantics=("arbitrary", "arbitrary"),
                                             vmem_limit_bytes=VMEM_LIMIT),
        name="gdn",
    )(qkv, zd, ba_t, col(a_log), col(dt_bias), norm_g.astype(F32).reshape(1, DN_HEAD_DIM))


def _out_kernel(ys_ref, yd_ref, x_ref, p_ref, wos_ref, wod_ref, wp_ref, pg_ref, wg_ref, fg_ref, o_ref, *, final):
    tm = x_ref.shape[1]
    rows = tm // PROJ_SUBTILES
    for s in range(PROJ_SUBTILES):
        r = slice(s * rows, (s + 1) * rows)
        h = x_ref[0, r] + _dot(ys_ref[0, r], wos_ref[...]) + _dot(yd_ref[0, r], wod_ref[...])
        e = _rms(_dot(p_ref[0, r], wp_ref[...]), pg_ref[...])
        h = h + jax.nn.sigmoid(_dot(h, wg_ref[...])) * e
        o_ref[0, r] = _rms(h, fg_ref[...]) if final else h


def _out_proj(ys, yd, x, p, w_out_s, w_out_d, w_ple, ple_g, w_gate, final_g, final):
    bsz, seq, dm = x.shape
    d_ssm = w_out_s.shape[0]
    d_dn = w_out_d.shape[0]
    tm = min(PROJ_TILE, seq)
    full = lambda shape: pl.BlockSpec(shape, lambda b, l: (0,) * len(shape))
    tok = lambda c: pl.BlockSpec((1, tm, c), lambda b, l: (b, l, 0))
    return pl.pallas_call(
        functools.partial(_out_kernel, final=final),
        grid=(bsz, seq // tm),
        in_specs=[tok(d_ssm), tok(d_dn), tok(dm), tok(p.shape[-1]),
                  full(w_out_s.shape), full(w_out_d.shape), full(w_ple.shape), full((1, dm)),
                  full(w_gate.shape), full((1, dm))],
        out_specs=tok(dm),
        out_shape=jax.ShapeDtypeStruct((bsz, seq, dm), F32),
        compiler_params=pltpu.CompilerParams(dimension_semantics=("arbitrary", "arbitrary"),
                                             vmem_limit_bytes=VMEM_LIMIT),
        name="out_proj",
    )(ys, yd, x, p, w_out_s, w_out_d, w_ple, ple_g, w_gate, final_g)


def kernel(x, p, norm_mix_g, w_in, ssm_A_re, ssm_A_im, ssm_B_re, ssm_B_im, ssm_C_re, ssm_C_im, ssm_D, ssm_log_dt, ssm_w_glu, ssm_b_glu, dn_conv_w, dn_A_log, dn_dt_bias, dn_norm_g, w_out, w_ple_proj, ple_norm_g, w_ple_gate, final_norm_g):
    bsz, seq, dm = x.shape
    depth = w_in.shape[0]
    d_ssm = ssm_D.shape[-1]
    heads = dn_A_log.shape[-1]
    d_dn = heads * DN_HEAD_DIM
    n_main = 2 * d_ssm + 4 * d_dn
    row = lambda t: t.astype(F32).reshape(1, -1)
    h = x
    for i in range(depth):
        w_main = w_in[i, :, :n_main].astype(BF16)
        w_ba = jnp.pad(w_in[i, :, n_main:], ((0, 0), (0, LANES - 2 * heads))).astype(BF16)
        us, zs, qkv, zd, ba_t = _in_proj(h, row(norm_mix_g[i]), w_main, w_ba, dn_conv_w[i].astype(F32),
                                             d_ssm, d_dn)

        yd = _gdn(qkv, zd, ba_t, dn_A_log[i], dn_dt_bias[i], dn_norm_g[i])

        a_r, a_i, bb_r, bb_i = _s5_prep(ssm_A_re[i], ssm_A_im[i], ssm_log_dt[i], ssm_B_re[i], ssm_B_im[i])
        bw, cw = _s5_layout(bb_r, bb_i, ssm_C_re[i], ssm_C_im[i])
        ys = _s5(us, zs, bw, cw, a_r, a_i, row(ssm_D[i]), ssm_w_glu[i].astype(BF16), row(ssm_b_glu[i]))

        h = _out_proj(ys, yd, h, p[i],
                      w_out[i, :d_ssm].astype(BF16), w_out[i, d_ssm:].astype(BF16),
                      w_ple_proj[i].astype(BF16), row(ple_norm_g[i]), w_ple_gate[i].astype(BF16),
                      row(final_norm_g), final=(i == depth - 1))
    return h
```

```python
import functools

import jax
import jax.numpy as jnp
from jax import lax
from jax.experimental import pallas as pl
from jax.experimental.pallas import tpu as pltpu

F32 = jnp.float32
BF16 = jnp.bfloat16

NORM_EPS = 1e-6
SSM_GROUP = 16
SSM_STATE = 64
DN_HEAD_DIM = 128
DN_CONV = 4
DN_CHUNK = 128

LANES = 128
SSM_LANE_BLOCK = 128
SSM_STATE_BLOCK = (SSM_LANE_BLOCK // SSM_GROUP) * SSM_STATE

PROJ_TILE = 512
PROJ_SUBTILES = 2
S5_TIME_TILE = 32
GDN_TILE = 256
GDN_BATCH_BLOCK = 2
VMEM_LIMIT = 48 * 1024 * 1024


def _dot(a, b):
    return jnp.dot(a.astype(BF16), b.astype(BF16), preferred_element_type=F32)


def _dot_nt(a, b):
    return lax.dot_general(a.astype(BF16), b.astype(BF16), (((1,), (1,)), ((), ())),
                           preferred_element_type=F32)


def _rms(x, g):
    return x * lax.rsqrt(jnp.mean(x * x, axis=-1, keepdims=True) + NORM_EPS) * g


def _s5_prep_kernel(lr_ref, li_ref, ldt_ref, br_ref, bi_ref, abr_ref, abi_ref, bbr_ref, bbi_ref):
    lr = lr_ref[...]
    li = li_ref[...]
    dt = jnp.exp(ldt_ref[...])
    mag = jnp.exp(lr * dt)
    ang = li * dt
    ab_r = mag * jnp.cos(ang)
    ab_i = mag * jnp.sin(ang)
    den = lr * lr + li * li
    nr = ab_r - 1.0
    ni = ab_i
    cr = (nr * lr + ni * li) / den
    ci = (ni * lr - nr * li) / den
    abr_ref[...] = ab_r
    abi_ref[...] = ab_i
    br = br_ref[...]
    bi = bi_ref[...]
    bbr_ref[...] = cr * br - ci * bi
    bbi_ref[...] = cr * bi + ci * br


def _s5_prep(a_re, a_im, log_dt, b_re, b_im):
    g, p = a_re.shape
    h = b_re.shape[-1]
    n = g * p
    row = lambda t: t.reshape(1, n).astype(F32)
    chan = lambda t: t.astype(F32).transpose(2, 0, 1).reshape(h, n)
    ldt = jnp.broadcast_to(log_dt.astype(F32)[:, None], (g, p))
    return pl.pallas_call(
        _s5_prep_kernel,
        out_shape=(jax.ShapeDtypeStruct((1, n), F32), jax.ShapeDtypeStruct((1, n), F32),
                   jax.ShapeDtypeStruct((h, n), F32), jax.ShapeDtypeStruct((h, n), F32)),
        name="s5_prep",
    )(row(a_re), row(a_im), row(ldt), chan(b_re), chan(b_im))


def _s5_layout(bb_r, bb_i, c_re, c_im):
    g, h, p = c_re.shape
    gpb = SSM_LANE_BLOCK // h
    nb = g // gpb
    eye = jnp.eye(gpb, dtype=F32)

    def in_w(bb):
        t = bb.reshape(h, nb, gpb, p)
        return jnp.einsum('hjkp,gk->jghkp', t, eye).reshape(nb, gpb * h, gpb * p)

    def out_w(c):
        t = c.astype(F32).reshape(nb, gpb, h, p)
        return jnp.einsum('jghp,gk->jgpkh', t, eye).reshape(nb, gpb * p, gpb * h)

    bw = jnp.concatenate([in_w(bb_r), in_w(bb_i)], axis=2).astype(BF16)
    cw = jnp.concatenate([out_w(c_re), -out_w(c_im)], axis=1).astype(BF16)
    return bw, cw


def _s5_kernel(u_hbm, z_hbm, bw_ref, cw_ref, ar_ref, ai_ref, d_ref, wglu_ref, bglu_ref, y_hbm,
               ubuf, zbuf, ybuf, sem_in, sem_out, bu_ref, s_ref, carry_ref, *, tt, bsz, nb, nsteps):
    sb = SSM_STATE_BLOCK
    rows = tt * bsz
    t = pl.program_id(0)
    slot = t % 2

    def in_copies(step, sl):
        cps = []
        for b in range(bsz):
            cps.append(pltpu.make_async_copy(u_hbm.at[b, pl.ds(step * tt, tt), :], ubuf.at[sl, :, b, :],
                                             sem_in.at[sl, 0]))
            cps.append(pltpu.make_async_copy(z_hbm.at[b, pl.ds(step * tt, tt), :], zbuf.at[sl, :, b, :],
                                             sem_in.at[sl, 1]))
        return cps

    def out_copies(step, sl):
        return [pltpu.make_async_copy(ybuf.at[sl, :, b, :], y_hbm.at[b, pl.ds(step * tt, tt), :], sem_out.at[sl])
                for b in range(bsz)]

    @pl.when(t == 0)
    def _():
        carry_ref[...] = jnp.zeros_like(carry_ref)
        for c in in_copies(0, 0):
            c.start()

    @pl.when(t + 1 < nsteps)
    def _():
        for c in in_copies(t + 1, 1 - slot):
            c.start()

    for c in in_copies(t, slot):
        c.wait()

    u = ubuf[slot].reshape(rows, nb * SSM_LANE_BLOCK)
    u16 = u.astype(BF16)
    for j in range(nb):
        bu_ref[:, 2 * sb * j:2 * sb * (j + 1)] = jnp.dot(
            u16[:, SSM_LANE_BLOCK * j:SSM_LANE_BLOCK * (j + 1)], bw_ref[j], preferred_element_type=F32)

    for j in range(nb):
        cr = 2 * sb * j
        ci = cr + sb
        a_r = jnp.broadcast_to(ar_ref[:, sb * j:sb * (j + 1)], (bsz, sb))
        a_i = jnp.broadcast_to(ai_ref[:, sb * j:sb * (j + 1)], (bsz, sb))

        def step(i, s, cr=cr, ci=ci, a_r=a_r, a_i=a_i):
            s_r, s_i = s
            r0 = pl.multiple_of(i * bsz, bsz)
            n_r = a_r * s_r - a_i * s_i + bu_ref[pl.ds(r0, bsz), cr:cr + sb]
            n_i = a_r * s_i + a_i * s_r + bu_ref[pl.ds(r0, bsz), ci:ci + sb]
            s_ref[pl.ds(r0, bsz), cr:cr + sb] = n_r.astype(BF16)
            s_ref[pl.ds(r0, bsz), ci:ci + sb] = n_i.astype(BF16)
            return n_r, n_i

        s_r, s_i = lax.fori_loop(0, tt, step, (carry_ref[:, cr:cr + sb], carry_ref[:, ci:ci + sb]), unroll=True)
        carry_ref[:, cr:cr + sb] = s_r
        carry_ref[:, ci:ci + sb] = s_i

    y = jnp.concatenate(
        [jnp.dot(s_ref[:, 2 * sb * j:2 * sb * (j + 1)], cw_ref[j], preferred_element_type=F32) for j in range(nb)],
        axis=1)
    y = y + d_ref[...] * u
    y = jax.nn.gelu(y)
    y = y * jax.nn.sigmoid(_dot(y, wglu_ref[...]) + bglu_ref[...])
    y = y * jax.nn.silu(zbuf[slot].reshape(rows, nb * SSM_LANE_BLOCK))

    @pl.when(t >= 2)
    def _():
        for c in out_copies(t - 2, slot):
            c.wait()

    ybuf[slot] = y.reshape(ybuf.shape[1:])
    for c in out_copies(t, slot):
        c.start()

    @pl.when(t == nsteps - 1)
    def _():
        for c in out_copies(t, slot):
            c.wait()
        if nsteps >= 2:
            for c in out_copies(t - 1, 1 - slot):
                c.wait()


def _s5(us, zs, bw, cw, a_r, a_i, d_vec, w_glu, b_glu):
    bsz, seq, d_ssm = us.shape
    nb = bw.shape[0]
    tt = min(S5_TIME_TILE, seq)
    rows = tt * bsz
    nsteps = seq // tt
    n_state = 2 * SSM_STATE_BLOCK * nb
    full = lambda shape: pl.BlockSpec(shape, lambda t: (0,) * len(shape))
    hbm = pl.BlockSpec(memory_space=pl.ANY)
    return pl.pallas_call(
        functools.partial(_s5_kernel, tt=tt, bsz=bsz, nb=nb, nsteps=nsteps),
        grid=(nsteps,),
        in_specs=[hbm, hbm, full(bw.shape), full(cw.shape), full(a_r.shape), full(a_i.shape),
                  full(d_vec.shape), full(w_glu.shape), full(b_glu.shape)],
        out_specs=hbm,
        out_shape=jax.ShapeDtypeStruct((bsz, seq, d_ssm), F32),
        scratch_shapes=[pltpu.VMEM((2, tt, bsz, d_ssm), F32),
                        pltpu.VMEM((2, tt, bsz, d_ssm), F32),
                        pltpu.VMEM((2, tt, bsz, d_ssm), F32),
                        pltpu.SemaphoreType.DMA((2, 2)),
                        pltpu.SemaphoreType.DMA((2,)),
                        pltpu.VMEM((rows, n_state), F32),
                        pltpu.VMEM((rows, n_state), BF16),
                        pltpu.VMEM((bsz, n_state), F32)],
        compiler_params=pltpu.CompilerParams(dimension_semantics=("arbitrary",),
                                             vmem_limit_bytes=VMEM_LIMIT),
        name="s5",
    )(us, zs, bw, cw, a_r, a_i, d_vec, w_glu, b_glu)


def _chunk_sums(g, sum_ref):
    n = g.shape[0]
    hi = g.astype(BF16).astype(F32)
    mid = (g - hi).astype(BF16).astype(F32)
    lo = g - hi - mid
    parts = jnp.concatenate([hi, mid, lo], axis=0).astype(BF16)
    s = jnp.dot(parts, sum_ref[...], preferred_element_type=F32)
    s = s[:n] + s[n:2 * n] + s[2 * n:]
    tc = g.shape[1]
    return s[:, :tc], s[:, tc:]


_M_CAUSAL, _M_STRICT, _M_EYE, _M_PAIR = 0, 1, 2, 3


def _gdn_fill_masks(mask_ref, lvl_ref, sum_ref, tc):
    ri = lax.broadcasted_iota(jnp.int32, (tc, tc), 0)
    ci = lax.broadcasted_iota(jnp.int32, (tc, tc), 1)
    same = (ri // DN_CHUNK) == (ci // DN_CHUNK)
    mask_ref[_M_CAUSAL] = (same & (ri >= ci)).astype(F32)
    mask_ref[_M_STRICT] = (same & (ri > ci)).astype(F32)
    mask_ref[_M_EYE] = (ri == ci).astype(F32)
    sum_ref[:, :tc] = (same & (ri <= ci)).astype(F32).astype(BF16)
    sum_ref[:, tc:] = (same & (ri > ci)).astype(F32).astype(BF16)
    s = 1
    lvl = 0
    while s < DN_CHUNK:
        m = (((ri // (2 * s)) == (ci // (2 * s))) & (((ri // s) % 2) == 1) & (((ci // s) % 2) == 0)).astype(F32)
        if lvl == 0:
            mask_ref[_M_PAIR] = m
        else:
            lvl_ref[lvl - 1] = m.astype(BF16)
        s *= 2
        lvl += 1


def _gdn_prep(bb, qkv_ref, ba_ref, alog_ref, dtb_ref, mask_ref, sum_ref, tc, heads):
    hd = DN_HEAD_DIM
    d_dn = heads * hd
    ba = ba_ref[bb]
    beta_r = jax.nn.sigmoid(ba)
    g_r = -jnp.exp(alog_ref[...]) * jax.nn.softplus(ba + dtb_ref[...])
    gc_r, gs_r = _chunk_sums(g_r, sum_ref)
    egs_r = jnp.exp(gs_r)
    egl_r = jnp.exp(gc_r + gs_r)
    pad = jnp.zeros((LANES - 3 * 2 * heads, tc), F32)
    cols = jnp.concatenate([beta_r, gc_r, egl_r, pad], axis=0).T
    out = []
    for h in range(heads):
        q = qkv_ref[bb, :, hd * h:hd * (h + 1)]
        k = qkv_ref[bb, :, d_dn + hd * h:d_dn + hd * (h + 1)]
        v = qkv_ref[bb, :, 2 * d_dn + hd * h:2 * d_dn + hd * (h + 1)]
        beta_c = cols[:, h:h + 1]
        gc_c = cols[:, 3 * heads + h:3 * heads + h + 1]
        gc_row = gc_r[heads + h:heads + h + 1, :]
        eg_c = jnp.exp(gc_c)
        kb = k * beta_c
        decay = jnp.exp(jnp.minimum(gc_c - gc_row, 0.0))
        qk_kk = _dot_nt(jnp.concatenate([q, k], axis=0), k)
        a = qk_kk[tc:] * (beta_c * decay) * mask_ref[_M_STRICT]
        out.append(dict(
            egl_c=cols[:, 5 * heads + h:5 * heads + h + 1],
            qg16=(q * eg_c).astype(BF16),
            rhs16=jnp.concatenate([v * beta_c, kb * eg_c], axis=1).astype(BF16),
            kdec16=(k.T * egs_r[heads + h:heads + h + 1, :]).astype(BF16),
            attn16=(qk_kk[:tc] * decay * mask_ref[_M_CAUSAL]).astype(BF16),
            a16=a.astype(BF16),
            t16=(mask_ref[_M_EYE] - a * mask_ref[_M_PAIR]).astype(BF16)))
    return out


def _gdn_body(qkv_ref, zd_ref, ba_ref, alog_ref, dtb_ref, ng_ref, y_ref, state_ref, mask_ref, lvl_ref, sum_ref,
              *, tc, heads, nbb):
    hd = DN_HEAD_DIM
    nchunk = tc // DN_CHUNK

    units = []
    for bb in range(nbb):
        for h, d in enumerate(_gdn_prep(bb, qkv_ref, ba_ref, alog_ref, dtb_ref, mask_ref, sum_ref, tc, heads)):
            units.append(dict(d, bb=bb, h=h))

    for lvl in range(lvl_ref.shape[0]):
        for u in units:
            u["x16"] = jnp.dot(u["a16"], u["t16"], preferred_element_type=F32).astype(BF16)
        for u in units:
            y16 = jnp.dot(u["t16"], u["x16"], preferred_element_type=F32).astype(BF16)
            u["t16"] = u["t16"] - y16 * lvl_ref[lvl]
    for u in units:
        sol = jnp.dot(u["t16"], u["rhs16"], preferred_element_type=F32)
        u["u"] = sol[:, :hd]
        u["w16"] = sol[:, hd:].astype(BF16)
        u["st"] = state_ref[u["bb"] * heads + u["h"]]
        u["outs"] = []

    for n in range(nchunk):
        r0 = n * DN_CHUNK
        r1 = r0 + DN_CHUNK
        for u in units:
            u["ws_qs"] = jnp.dot(jnp.concatenate([u["w16"][r0:r1], u["qg16"][r0:r1]], axis=0),
                                 u["st"].astype(BF16), preferred_element_type=F32)
        for u in units:
            v_new = (u["u"][r0:r1] - u["ws_qs"][:DN_CHUNK]).astype(BF16)
            u["av_kv"] = jnp.dot(jnp.concatenate([u["attn16"][r0:r1, r0:r1], u["kdec16"][:, r0:r1]], axis=0),
                                 v_new, preferred_element_type=F32)
        for u in units:
            u["outs"].append(u["ws_qs"][DN_CHUNK:] + u["av_kv"][:DN_CHUNK])
            u["st"] = u["st"] * u["egl_c"][r0:r1] + u["av_kv"][DN_CHUNK:]
    for u in units:
        bb, h = u["bb"], u["h"]
        state_ref[bb * heads + h] = u["st"]
        o = jnp.concatenate(u["outs"], axis=0)
        z = zd_ref[bb, :, hd * h:hd * (h + 1)]
        y_ref[bb, :, hd * h:hd * (h + 1)] = _rms(o, ng_ref[...]) * jax.nn.silu(z)


def _proj_gdn_kernel(x_ref, g_ref, wm_ref, wba_ref, cw_ref, cs_ref, alog_ref, dtb_ref, ng_ref,
                     us_ref, zs_ref, y_ref,
                     tail_ref, xpad_ref, qkv_ref, zd_ref, ba_ref, state_ref, mask_ref, lvl_ref, sum_ref,
                     *, d_ssm, d_dn, tc, heads, nbb):
    hd = DN_HEAD_DIM
    nslab = 3 * d_dn // LANES
    nrow = ba_ref.shape[1]

    @pl.when(pl.program_id(1) == 0)
    def _():
        tail_ref[...] = jnp.zeros_like(tail_ref)
        state_ref[...] = jnp.zeros_like(state_ref)
        _gdn_fill_masks(mask_ref, lvl_ref, sum_ref, tc)

    a = _rms(x_ref[...].reshape(nbb * tc, x_ref.shape[-1]), g_ref[...]).astype(BF16)
    qkv = jnp.dot(a, wm_ref[:, 2 * d_ssm:2 * d_ssm + 3 * d_dn], preferred_element_type=F32)
    us_ref[...] = jnp.dot(a, wm_ref[:, :d_ssm], preferred_element_type=F32).reshape(us_ref.shape)
    zs_ref[...] = jnp.dot(a, wm_ref[:, d_ssm:2 * d_ssm], preferred_element_type=F32).reshape(zs_ref.shape)
    zd_ref[...] = jnp.dot(a, wm_ref[:, 2 * d_ssm + 3 * d_dn:], preferred_element_type=F32).reshape(zd_ref.shape)
    ba = jnp.dot(a, wba_ref[...], preferred_element_type=F32)

    for bb in range(nbb):
        rows = slice(bb * tc, (bb + 1) * tc)
        ba_ref[bb] = ba[rows].T[:nrow]
        for s in range(nslab):
            lanes = slice(LANES * s, LANES * (s + 1))
            xpad_ref[bb, s, 0:8, :] = tail_ref[bb, s]
            xpad_ref[bb, s, 8:, :] = qkv[rows, lanes]
            tail_ref[bb, s] = qkv[bb * tc + tc - 8:(bb + 1) * tc, lanes]
            acc = cw_ref[DN_CONV - 1:DN_CONV, lanes] * qkv[rows, lanes]
            for j in range(DN_CONV - 1):
                acc = acc + cw_ref[j:j + 1, lanes] * xpad_ref[bb, s, pl.ds(cs_ref[j], tc), :]
            t = jax.nn.silu(acc)
            if LANES * s < 2 * d_dn:
                t = t * lax.rsqrt(jnp.sum(t * t, axis=-1, keepdims=True) + NORM_EPS)
                if LANES * s < d_dn:
                    t = t * (hd ** -0.5)
            qkv_ref[bb, :, lanes] = t

    _gdn_body(qkv_ref, zd_ref, ba_ref, alog_ref, dtb_ref, ng_ref, y_ref, state_ref, mask_ref, lvl_ref, sum_ref,
              tc=tc, heads=heads, nbb=nbb)


def _proj_gdn(x, g, w_main, w_ba, conv_w, a_log, dt_bias, norm_g, d_ssm, d_dn):
    bsz, seq, dm = x.shape
    heads = d_dn // DN_HEAD_DIM
    tc = min(GDN_TILE, seq)
    nbb = GDN_BATCH_BLOCK
    nrow = 2 * heads
    nslab = 3 * d_dn // LANES
    assert DN_CHUNK == DN_HEAD_DIM and tc % DN_CHUNK == 0 and bsz % nbb == 0
    n_lvl = DN_CHUNK.bit_length() - 2
    col = lambda t: jnp.concatenate([jnp.zeros((heads,), F32), t.astype(F32)]).reshape(nrow, 1)
    full = lambda shape: pl.BlockSpec(shape, lambda b, l: (0,) * len(shape))
    tok = lambda c: pl.BlockSpec((nbb, tc, c), lambda b, l: (b, l, 0))
    conv_starts = 8 - (DN_CONV - 1) + jnp.arange(DN_CONV - 1, dtype=jnp.int32)
    return pl.pallas_call(
        functools.partial(_proj_gdn_kernel, d_ssm=d_ssm, d_dn=d_dn, tc=tc, heads=heads, nbb=nbb),
        grid=(bsz // nbb, seq // tc),
        in_specs=[tok(dm), full((1, dm)), full(w_main.shape), full(w_ba.shape), full(conv_w.shape),
                  pl.BlockSpec(memory_space=pltpu.SMEM),
                  full((nrow, 1)), full((nrow, 1)), full((1, DN_HEAD_DIM))],
        out_specs=[tok(d_ssm), tok(d_ssm), tok(d_dn)],
        out_shape=(jax.ShapeDtypeStruct((bsz, seq, d_ssm), F32),
                   jax.ShapeDtypeStruct((bsz, seq, d_ssm), F32),
                   jax.ShapeDtypeStruct((bsz, seq, d_dn), F32)),
        scratch_shapes=[pltpu.VMEM((nbb, nslab, 8, LANES), F32),
                        pltpu.VMEM((nbb, nslab, 8 + tc, LANES), F32),
                        pltpu.VMEM((nbb, tc, 3 * d_dn), F32),
                        pltpu.VMEM((nbb, tc, d_dn), F32),
                        pltpu.VMEM((nbb, nrow, tc), F32),
                        pltpu.VMEM((nbb * heads, DN_HEAD_DIM, DN_HEAD_DIM), F32),
                        pltpu.VMEM((_M_PAIR + 1, tc, tc), F32),
                        pltpu.VMEM((n_lvl, tc, tc), BF16),
                        pltpu.VMEM((tc, 2 * tc), BF16)],
        compiler_params=pltpu.CompilerParams(dimension_semantics=("arbitrary", "arbitrary"),
                                             vmem_limit_bytes=VMEM_LIMIT),
        name="proj_gdn",
    )(x, g, w_main, w_ba, conv_w, conv_starts, col(a_log), col(dt_bias),
      norm_g.astype(F32).reshape(1, DN_HEAD_DIM))


def _out_kernel(ys_ref, yd_ref, x_ref, p_ref, wos_ref, wod_ref, wp_ref, pg_ref, wg_ref, fg_ref, o_ref, *, final):
    tm = x_ref.shape[1]
    rows = tm // PROJ_SUBTILES
    for s in range(PROJ_SUBTILES):
        r = slice(s * rows, (s + 1) * rows)
        h = x_ref[0, r] + _dot(ys_ref[0, r], wos_ref[...]) + _dot(yd_ref[0, r], wod_ref[...])
        e = _rms(_dot(p_ref[0, r], wp_ref[...]), pg_ref[...])
        h = h + jax.nn.sigmoid(_dot(h, wg_ref[...])) * e
        o_ref[0, r] = _rms(h, fg_ref[...]) if final else h


def _out_proj(ys, yd, x, p, w_out_s, w_out_d, w_ple, ple_g, w_gate, final_g, final):
    bsz, seq, dm = x.shape
    d_ssm = w_out_s.shape[0]
    d_dn = w_out_d.shape[0]
    tm = min(PROJ_TILE, seq)
    full = lambda shape: pl.BlockSpec(shape, lambda b, l: (0,) * len(shape))
    tok = lambda c: pl.BlockSpec((1, tm, c), lambda b, l: (b, l, 0))
    return pl.pallas_call(
        functools.partial(_out_kernel, final=final),
        grid=(bsz, seq // tm),
        in_specs=[tok(d_ssm), tok(d_dn), tok(dm), tok(p.shape[-1]),
                  full(w_out_s.shape), full(w_out_d.shape), full(w_ple.shape), full((1, dm)),
                  full(w_gate.shape), full((1, dm))],
        out_specs=tok(dm),
        out_shape=jax.ShapeDtypeStruct((bsz, seq, dm), F32),
        compiler_params=pltpu.CompilerParams(dimension_semantics=("arbitrary", "arbitrary"),
                                             vmem_limit_bytes=VMEM_LIMIT),
        name="out_proj",
    )(ys, yd, x, p, w_out_s, w_out_d, w_ple, ple_g, w_gate, final_g)


def kernel(x, p, norm_mix_g, w_in, ssm_A_re, ssm_A_im, ssm_B_re, ssm_B_im, ssm_C_re, ssm_C_im, ssm_D, ssm_log_dt, ssm_w_glu, ssm_b_glu, dn_conv_w, dn_A_log, dn_dt_bias, dn_norm_g, w_out, w_ple_proj, ple_norm_g, w_ple_gate, final_norm_g):
    bsz, seq, dm = x.shape
    depth = w_in.shape[0]
    d_ssm = ssm_D.shape[-1]
    heads = dn_A_log.shape[-1]
    d_dn = heads * DN_HEAD_DIM
    n_main = 2 * d_ssm + 4 * d_dn
    row = lambda t: t.astype(F32).reshape(1, -1)
    h = x
    for i in range(depth):
        w_main = w_in[i, :, :n_main].astype(BF16)
        w_ba = jnp.pad(w_in[i, :, n_main:], ((0, 0), (0, LANES - 2 * heads))).astype(BF16)
        us, zs, yd = _proj_gdn(h, row(norm_mix_g[i]), w_main, w_ba, dn_conv_w[i].astype(F32),
                               dn_A_log[i], dn_dt_bias[i], dn_norm_g[i], d_ssm, d_dn)

        a_r, a_i, bb_r, bb_i = _s5_prep(ssm_A_re[i], ssm_A_im[i], ssm_log_dt[i], ssm_B_re[i], ssm_B_im[i])
        bw, cw = _s5_layout(bb_r, bb_i, ssm_C_re[i], ssm_C_im[i])
        ys = _s5(us, zs, bw, cw, a_r, a_i, row(ssm_D[i]), ssm_w_glu[i].astype(BF16), row(ssm_b_glu[i]))

        h = _out_proj(ys, yd, h, p[i],
                      w_out[i, :d_ssm].astype(BF16), w_out[i, d_ssm:].astype(BF16),
                      w_ple_proj[i].astype(BF16), row(ple_norm_g[i]), w_ple_gate[i].astype(BF16),
                      row(final_norm_g), final=(i == depth - 1))
    return h
```

```python
import functools

import jax
import jax.numpy as jnp
from jax import lax
from jax.experimental import pallas as pl
from jax.experimental.pallas import tpu as pltpu

F32 = jnp.float32
BF16 = jnp.bfloat16

NORM_EPS = 1e-6
SSM_GROUP = 16
SSM_STATE = 64
DN_HEAD_DIM = 128
DN_CONV = 4
DN_CHUNK = 128

LANES = 128
SSM_LANE_BLOCK = 128
SSM_STATE_BLOCK = (SSM_LANE_BLOCK // SSM_GROUP) * SSM_STATE

PROJ_TILE = 1024
PROJ_SUBTILES = 2
S5_TIME_TILE = 64
S5_POST_SPLIT = 4
GDN_TILE = 256
GDN_BATCH_BLOCK = 2
VMEM_LIMIT = 48 * 1024 * 1024


def _dot(a, b):
    return jnp.dot(a.astype(BF16), b.astype(BF16), preferred_element_type=F32)


def _dot_nt(a, b):
    return lax.dot_general(a.astype(BF16), b.astype(BF16), (((1,), (1,)), ((), ())),
                           preferred_element_type=F32)


def _rms(x, g):
    return x * lax.rsqrt(jnp.mean(x * x, axis=-1, keepdims=True) + NORM_EPS) * g


def _s5_prep_kernel(lr_ref, li_ref, ldt_ref, br_ref, bi_ref, abr_ref, abi_ref, bbr_ref, bbi_ref):
    lr = lr_ref[...]
    li = li_ref[...]
    dt = jnp.exp(ldt_ref[...])
    mag = jnp.exp(lr * dt)
    ang = li * dt
    ab_r = mag * jnp.cos(ang)
    ab_i = mag * jnp.sin(ang)
    den = lr * lr + li * li
    nr = ab_r - 1.0
    ni = ab_i
    cr = (nr * lr + ni * li) / den
    ci = (ni * lr - nr * li) / den
    abr_ref[...] = ab_r
    abi_ref[...] = ab_i
    br = br_ref[...]
    bi = bi_ref[...]
    bbr_ref[...] = cr * br - ci * bi
    bbi_ref[...] = cr * bi + ci * br


def _s5_prep(a_re, a_im, log_dt, b_re, b_im):
    g, p = a_re.shape
    h = b_re.shape[-1]
    n = g * p
    row = lambda t: t.reshape(1, n).astype(F32)
    chan = lambda t: t.astype(F32).transpose(2, 0, 1).reshape(h, n)
    ldt = jnp.broadcast_to(log_dt.astype(F32)[:, None], (g, p))
    return pl.pallas_call(
        _s5_prep_kernel,
        out_shape=(jax.ShapeDtypeStruct((1, n), F32), jax.ShapeDtypeStruct((1, n), F32),
                   jax.ShapeDtypeStruct((h, n), F32), jax.ShapeDtypeStruct((h, n), F32)),
        name="s5_prep",
    )(row(a_re), row(a_im), row(ldt), chan(b_re), chan(b_im))


def _s5_layout(bb_r, bb_i, c_re, c_im):
    g, h, p = c_re.shape
    gpb = SSM_LANE_BLOCK // h
    nb = g // gpb
    eye = jnp.eye(gpb, dtype=F32)

    def in_w(bb):
        t = bb.reshape(h, nb, gpb, p)
        return jnp.einsum('hjkp,gk->jghkp', t, eye).reshape(nb, gpb * h, gpb * p)

    def out_w(c):
        t = c.astype(F32).reshape(nb, gpb, h, p)
        return jnp.einsum('jghp,gk->jgpkh', t, eye).reshape(nb, gpb * p, gpb * h)

    bw = jnp.concatenate([in_w(bb_r), in_w(bb_i)], axis=2).astype(BF16)
    cw = jnp.concatenate([out_w(c_re), -out_w(c_im)], axis=1).astype(BF16)
    return bw, cw


def _s5_kernel(u_hbm, z_hbm, bw_ref, cw_ref, ar_ref, ai_ref, d_ref, wglu_ref, bglu_ref, y_hbm,
               ubuf, zbuf, ybuf, sem_in, sem_out, bu_ref, s_ref, carry_ref, *, tt, bsz, nb, nsteps):
    sb = SSM_STATE_BLOCK
    rows = tt * bsz
    t = pl.program_id(0)
    slot = t % 2

    def in_copies(step, sl):
        cps = []
        for b in range(bsz):
            cps.append(pltpu.make_async_copy(u_hbm.at[b, pl.ds(step * tt, tt), :], ubuf.at[sl, :, b, :],
                                             sem_in.at[sl, 0]))
            cps.append(pltpu.make_async_copy(z_hbm.at[b, pl.ds(step * tt, tt), :], zbuf.at[sl, :, b, :],
                                             sem_in.at[sl, 1]))
        return cps

    def out_copies(step, sl):
        return [pltpu.make_async_copy(ybuf.at[sl, :, b, :], y_hbm.at[b, pl.ds(step * tt, tt), :], sem_out.at[sl])
                for b in range(bsz)]

    @pl.when(t == 0)
    def _():
        carry_ref[...] = jnp.zeros_like(carry_ref)
        for c in in_copies(0, 0):
            c.start()

    @pl.when(t >= 2)
    def _():
        for c in out_copies(t - 2, slot):
            c.wait()

    nxt = jnp.minimum(t + 1, nsteps - 1)
    for c in in_copies(nxt, 1 - slot):
        c.start()
    for c in in_copies(t, slot):
        c.wait()

    u = ubuf[slot].reshape(rows, nb * SSM_LANE_BLOCK)
    z = zbuf[slot].reshape(rows, nb * SSM_LANE_BLOCK)
    u16 = u.astype(BF16)
    for j in range(nb):
        bu_ref[:, 2 * sb * j:2 * sb * (j + 1)] = jnp.dot(
            u16[:, SSM_LANE_BLOCK * j:SSM_LANE_BLOCK * (j + 1)], bw_ref[j], preferred_element_type=F32)

    a_re = [jnp.broadcast_to(ar_ref[:, sb * j:sb * (j + 1)], (bsz, sb)) for j in range(nb)]
    a_im = [jnp.broadcast_to(ai_ref[:, sb * j:sb * (j + 1)], (bsz, sb)) for j in range(nb)]
    state = [(carry_ref[:, 2 * sb * j:2 * sb * j + sb], carry_ref[:, 2 * sb * j + sb:2 * sb * (j + 1)])
             for j in range(nb)]
    th = tt // S5_POST_SPLIT
    for part in range(S5_POST_SPLIT):
        for j in range(nb):
            cr = 2 * sb * j
            ci = cr + sb
            s_r, s_i = state[j]
            for i in range(part * th, (part + 1) * th):
                r = slice(i * bsz, (i + 1) * bsz)
                s_r, s_i = (a_re[j] * s_r - a_im[j] * s_i + bu_ref[r, cr:cr + sb],
                            a_re[j] * s_i + a_im[j] * s_r + bu_ref[r, ci:ci + sb])
                s_ref[r, cr:cr + sb] = s_r.astype(BF16)
                s_ref[r, ci:ci + sb] = s_i.astype(BF16)
            state[j] = (s_r, s_i)
        r = slice(part * th * bsz, (part + 1) * th * bsz)
        y = jnp.concatenate(
            [jnp.dot(s_ref[r, 2 * sb * j:2 * sb * (j + 1)], cw_ref[j], preferred_element_type=F32)
             for j in range(nb)], axis=1)
        y = y + d_ref[...] * u[r]
        y = jax.nn.gelu(y)
        y = y * jax.nn.sigmoid(_dot(y, wglu_ref[...]) + bglu_ref[...])
        y = y * jax.nn.silu(z[r])
        ybuf[slot, part * th:(part + 1) * th] = y.reshape(th, bsz, y.shape[-1])
    for j in range(nb):
        carry_ref[:, 2 * sb * j:2 * sb * j + sb] = state[j][0]
        carry_ref[:, 2 * sb * j + sb:2 * sb * (j + 1)] = state[j][1]

    for c in out_copies(t, slot):
        c.start()

    @pl.when(t == nsteps - 1)
    def _():
        for c in in_copies(nxt, 1 - slot):
            c.wait()
        for c in out_copies(t, slot):
            c.wait()
        if nsteps >= 2:
            for c in out_copies(t - 1, 1 - slot):
                c.wait()


def _s5(us, zs, bw, cw, a_r, a_i, d_vec, w_glu, b_glu):
    bsz, seq, d_ssm = us.shape
    nb = bw.shape[0]
    tt = min(S5_TIME_TILE, seq)
    rows = tt * bsz
    nsteps = seq // tt
    n_state = 2 * SSM_STATE_BLOCK * nb
    full = lambda shape: pl.BlockSpec(shape, lambda t: (0,) * len(shape))
    hbm = pl.BlockSpec(memory_space=pl.ANY)
    return pl.pallas_call(
        functools.partial(_s5_kernel, tt=tt, bsz=bsz, nb=nb, nsteps=nsteps),
        grid=(nsteps,),
        in_specs=[hbm, hbm, full(bw.shape), full(cw.shape), full(a_r.shape), full(a_i.shape),
                  full(d_vec.shape), full(w_glu.shape), full(b_glu.shape)],
        out_specs=hbm,
        out_shape=jax.ShapeDtypeStruct((bsz, seq, d_ssm), F32),
        scratch_shapes=[pltpu.VMEM((2, tt, bsz, d_ssm), F32),
                        pltpu.VMEM((2, tt, bsz, d_ssm), F32),
                        pltpu.VMEM((2, tt, bsz, d_ssm), F32),
                        pltpu.SemaphoreType.DMA((2, 2)),
                        pltpu.SemaphoreType.DMA((2,)),
                        pltpu.VMEM((rows, n_state), F32),
                        pltpu.VMEM((rows, n_state), BF16),
                        pltpu.VMEM((bsz, n_state), F32)],
        compiler_params=pltpu.CompilerParams(dimension_semantics=("arbitrary",),
                                             vmem_limit_bytes=VMEM_LIMIT),
        name="s5",
    )(us, zs, bw, cw, a_r, a_i, d_vec, w_glu, b_glu)


def _chunk_sums(g, sum_ref):
    n = g.shape[0]
    hi = g.astype(BF16).astype(F32)
    mid = (g - hi).astype(BF16).astype(F32)
    lo = g - hi - mid
    parts = jnp.concatenate([hi, mid, lo], axis=0).astype(BF16)
    s = jnp.dot(parts, sum_ref[...], preferred_element_type=F32)
    s = s[:n] + s[n:2 * n] + s[2 * n:]
    tc = g.shape[1]
    return s[:, :tc], s[:, tc:]


_M_CAUSAL, _M_STRICT, _M_EYE, _M_PAIR = 0, 1, 2, 3


def _gdn_fill_masks(mask_ref, lvl_ref, sum_ref, tc):
    ri = lax.broadcasted_iota(jnp.int32, (tc, tc), 0)
    ci = lax.broadcasted_iota(jnp.int32, (tc, tc), 1)
    same = (ri // DN_CHUNK) == (ci // DN_CHUNK)
    mask_ref[_M_CAUSAL] = (same & (ri >= ci)).astype(F32)
    mask_ref[_M_STRICT] = (same & (ri > ci)).astype(F32)
    mask_ref[_M_EYE] = (ri == ci).astype(F32)
    sum_ref[:, :tc] = (same & (ri <= ci)).astype(F32).astype(BF16)
    sum_ref[:, tc:] = (same & (ri > ci)).astype(F32).astype(BF16)
    s = 1
    lvl = 0
    while s < DN_CHUNK:
        m = (((ri // (2 * s)) == (ci // (2 * s))) & (((ri // s) % 2) == 1) & (((ci // s) % 2) == 0)).astype(F32)
        if lvl == 0:
            mask_ref[_M_PAIR] = m
        else:
            lvl_ref[lvl - 1] = m.astype(BF16)
        s *= 2
        lvl += 1


def _gdn_prep(bb, qkv_ref, ba_ref, alog_ref, dtb_ref, mask_ref, sum_ref, tc, heads):
    hd = DN_HEAD_DIM
    d_dn = heads * hd
    ba = ba_ref[bb]
    beta_r = jax.nn.sigmoid(ba)
    g_r = -jnp.exp(alog_ref[...]) * jax.nn.softplus(ba + dtb_ref[...])
    gc_r, gs_r = _chunk_sums(g_r, sum_ref)
    egs_r = jnp.exp(gs_r)
    egl_r = jnp.exp(gc_r + gs_r)
    pad = jnp.zeros((LANES - 3 * 2 * heads, tc), F32)
    cols = jnp.concatenate([beta_r, gc_r, egl_r, pad], axis=0).T
    out = []
    for h in range(heads):
        q = qkv_ref[bb, :, hd * h:hd * (h + 1)]
        k = qkv_ref[bb, :, d_dn + hd * h:d_dn + hd * (h + 1)]
        v = qkv_ref[bb, :, 2 * d_dn + hd * h:2 * d_dn + hd * (h + 1)]
        beta_c = cols[:, h:h + 1]
        gc_c = cols[:, 3 * heads + h:3 * heads + h + 1]
        gc_row = gc_r[heads + h:heads + h + 1, :]
        eg_c = jnp.exp(gc_c)
        kb = k * beta_c
        decay = jnp.exp(jnp.minimum(gc_c - gc_row, 0.0))
        qk_kk = _dot_nt(jnp.concatenate([q, k], axis=0), k)
        a = qk_kk[tc:] * (beta_c * decay) * mask_ref[_M_STRICT]
        out.append(dict(
            egl_c=cols[:, 5 * heads + h:5 * heads + h + 1],
            qg16=(q * eg_c).astype(BF16),
            rhs16=jnp.concatenate([v * beta_c, kb * eg_c], axis=1).astype(BF16),
            kdec16=(k.T * egs_r[heads + h:heads + h + 1, :]).astype(BF16),
            attn16=(qk_kk[:tc] * decay * mask_ref[_M_CAUSAL]).astype(BF16),
            a16=a.astype(BF16),
            t16=(mask_ref[_M_EYE] - a * mask_ref[_M_PAIR]).astype(BF16)))
    return out


def _gdn_body(qkv_ref, zd_ref, ba_ref, alog_ref, dtb_ref, ng_ref, y_ref, state_ref, mask_ref, lvl_ref, sum_ref,
              *, tc, heads, nbb):
    hd = DN_HEAD_DIM
    nchunk = tc // DN_CHUNK

    units = []
    for bb in range(nbb):
        for h, d in enumerate(_gdn_prep(bb, qkv_ref, ba_ref, alog_ref, dtb_ref, mask_ref, sum_ref, tc, heads)):
            units.append(dict(d, bb=bb, h=h))

    for lvl in range(lvl_ref.shape[0]):
        for u in units:
            u["x16"] = jnp.dot(u["a16"], u["t16"], preferred_element_type=F32).astype(BF16)
        for u in units:
            y16 = jnp.dot(u["t16"], u["x16"], preferred_element_type=F32).astype(BF16)
            u["t16"] = u["t16"] - y16 * lvl_ref[lvl]
    for u in units:
        sol = jnp.dot(u["t16"], u["rhs16"], preferred_element_type=F32)
        u["u"] = sol[:, :hd]
        u["w16"] = sol[:, hd:].astype(BF16)
        u["st"] = state_ref[u["bb"] * heads + u["h"]]
        u["outs"] = []

    for n in range(nchunk):
        r0 = n * DN_CHUNK
        r1 = r0 + DN_CHUNK
        for u in units:
            u["ws_qs"] = jnp.dot(jnp.concatenate([u["w16"][r0:r1], u["qg16"][r0:r1]], axis=0),
                                 u["st"].astype(BF16), preferred_element_type=F32)
        for u in units:
            v_new = (u["u"][r0:r1] - u["ws_qs"][:DN_CHUNK]).astype(BF16)
            u["av_kv"] = jnp.dot(jnp.concatenate([u["attn16"][r0:r1, r0:r1], u["kdec16"][:, r0:r1]], axis=0),
                                 v_new, preferred_element_type=F32)
        for u in units:
            u["outs"].append(u["ws_qs"][DN_CHUNK:] + u["av_kv"][:DN_CHUNK])
            u["st"] = u["st"] * u["egl_c"][r0:r1] + u["av_kv"][DN_CHUNK:]
    for u in units:
        bb, h = u["bb"], u["h"]
        state_ref[bb * heads + h] = u["st"]
        o = jnp.concatenate(u["outs"], axis=0)
        z = zd_ref[bb, :, hd * h:hd * (h + 1)]
        y_ref[bb, :, hd * h:hd * (h + 1)] = _rms(o, ng_ref[...]) * jax.nn.silu(z)


def _proj_gdn_kernel(x_ref, g_ref, wm_ref, wba_ref, cw_ref, cs_ref, alog_ref, dtb_ref, ng_ref,
                     us_ref, zs_ref, y_ref,
                     tail_ref, xpad_ref, qkv_ref, zd_ref, ba_ref, state_ref, mask_ref, lvl_ref, sum_ref,
                     *, d_ssm, d_dn, tc, heads, nbb):
    hd = DN_HEAD_DIM
    nslab = 3 * d_dn // LANES
    nrow = ba_ref.shape[1]

    @pl.when(pl.program_id(1) == 0)
    def _():
        tail_ref[...] = jnp.zeros_like(tail_ref)
        state_ref[...] = jnp.zeros_like(state_ref)
        _gdn_fill_masks(mask_ref, lvl_ref, sum_ref, tc)

    a = _rms(x_ref[...].reshape(nbb * tc, x_ref.shape[-1]), g_ref[...]).astype(BF16)
    qkv = jnp.dot(a, wm_ref[:, 2 * d_ssm:2 * d_ssm + 3 * d_dn], preferred_element_type=F32)
    us_ref[...] = jnp.dot(a, wm_ref[:, :d_ssm], preferred_element_type=F32).reshape(us_ref.shape)
    zs_ref[...] = jnp.dot(a, wm_ref[:, d_ssm:2 * d_ssm], preferred_element_type=F32).reshape(zs_ref.shape)
    zd_ref[...] = jnp.dot(a, wm_ref[:, 2 * d_ssm + 3 * d_dn:], preferred_element_type=F32).reshape(zd_ref.shape)
    ba = jnp.dot(a, wba_ref[...], preferred_element_type=F32)

    for bb in range(nbb):
        rows = slice(bb * tc, (bb + 1) * tc)
        ba_ref[bb] = ba[rows].T[:nrow]
        for s in range(nslab):
            lanes = slice(LANES * s, LANES * (s + 1))
            xpad_ref[bb, s, 0:8, :] = tail_ref[bb, s]
            xpad_ref[bb, s, 8:, :] = qkv[rows, lanes]
            tail_ref[bb, s] = qkv[bb * tc + tc - 8:(bb + 1) * tc, lanes]
            acc = cw_ref[DN_CONV - 1:DN_CONV, lanes] * qkv[rows, lanes]
            for j in range(DN_CONV - 1):
                acc = acc + cw_ref[j:j + 1, lanes] * xpad_ref[bb, s, pl.ds(cs_ref[j], tc), :]
            t = jax.nn.silu(acc)
            if LANES * s < 2 * d_dn:
                t = t * lax.rsqrt(jnp.sum(t * t, axis=-1, keepdims=True) + NORM_EPS)
                if LANES * s < d_dn:
                    t = t * (hd ** -0.5)
            qkv_ref[bb, :, lanes] = t

    _gdn_body(qkv_ref, zd_ref, ba_ref, alog_ref, dtb_ref, ng_ref, y_ref, state_ref, mask_ref, lvl_ref, sum_ref,
              tc=tc, heads=heads, nbb=nbb)


def _proj_gdn(x, g, w_main, w_ba, conv_w, a_log, dt_bias, norm_g, d_ssm, d_dn):
    bsz, seq, dm = x.shape
    heads = d_dn // DN_HEAD_DIM
    tc = min(GDN_TILE, seq)
    nbb = GDN_BATCH_BLOCK
    nrow = 2 * heads
    nslab = 3 * d_dn // LANES
    assert DN_CHUNK == DN_HEAD_DIM and tc % DN_CHUNK == 0 and bsz % nbb == 0
    n_lvl = DN_CHUNK.bit_length() - 2
    col = lambda t: jnp.concatenate([jnp.zeros((heads,), F32), t.astype(F32)]).reshape(nrow, 1)
    full = lambda shape: pl.BlockSpec(shape, lambda b, l: (0,) * len(shape))
    tok = lambda c: pl.BlockSpec((nbb, tc, c), lambda b, l: (b, l, 0))
    conv_starts = 8 - (DN_CONV - 1) + jnp.arange(DN_CONV - 1, dtype=jnp.int32)
    return pl.pallas_call(
        functools.partial(_proj_gdn_kernel, d_ssm=d_ssm, d_dn=d_dn, tc=tc, heads=heads, nbb=nbb),
        grid=(bsz // nbb, seq // tc),
        in_specs=[tok(dm), full((1, dm)), full(w_main.shape), full(w_ba.shape), full(conv_w.shape),
                  pl.BlockSpec(memory_space=pltpu.SMEM),
                  full((nrow, 1)), full((nrow, 1)), full((1, DN_HEAD_DIM))],
        out_specs=[tok(d_ssm), tok(d_ssm), tok(d_dn)],
        out_shape=(jax.ShapeDtypeStruct((bsz, seq, d_ssm), F32),
                   jax.ShapeDtypeStruct((bsz, seq, d_ssm), F32),
                   jax.ShapeDtypeStruct((bsz, seq, d_dn), F32)),
        scratch_shapes=[pltpu.VMEM((nbb, nslab, 8, LANES), F32),
                        pltpu.VMEM((nbb, nslab, 8 + tc, LANES), F32),
                        pltpu.VMEM((nbb, tc, 3 * d_dn), F32),
                        pltpu.VMEM((nbb, tc, d_dn), F32),
                        pltpu.VMEM((nbb, nrow, tc), F32),
                        pltpu.VMEM((nbb * heads, DN_HEAD_DIM, DN_HEAD_DIM), F32),
                        pltpu.VMEM((_M_PAIR + 1, tc, tc), F32),
                        pltpu.VMEM((n_lvl, tc, tc), BF16),
                        pltpu.VMEM((tc, 2 * tc), BF16)],
        compiler_params=pltpu.CompilerParams(dimension_semantics=("arbitrary", "arbitrary"),
                                             vmem_limit_bytes=VMEM_LIMIT),
        name="proj_gdn",
    )(x, g, w_main, w_ba, conv_w, conv_starts, col(a_log), col(dt_bias),
      norm_g.astype(F32).reshape(1, DN_HEAD_DIM))


def _out_kernel(ys_ref, yd_ref, x_ref, p_ref, wos_ref, wod_ref, wp_ref, pg_ref, wg_ref, fg_ref, o_ref, *, final):
    tm = x_ref.shape[1]
    rows = tm // PROJ_SUBTILES
    for s in range(PROJ_SUBTILES):
        r = slice(s * rows, (s + 1) * rows)
        h = x_ref[0, r] + _dot(ys_ref[0, r], wos_ref[...]) + _dot(yd_ref[0, r], wod_ref[...])
        e = _rms(_dot(p_ref[0, r], wp_ref[...]), pg_ref[...])
        h = h + jax.nn.sigmoid(_dot(h, wg_ref[...])) * e
        o_ref[0, r] = _rms(h, fg_ref[...]) if final else h


def _out_proj(ys, yd, x, p, w_out_s, w_out_d, w_ple, ple_g, w_gate, final_g, final):
    bsz, seq, dm = x.shape
    d_ssm = w_out_s.shape[0]
    d_dn = w_out_d.shape[0]
    tm = min(PROJ_TILE, seq)
    full = lambda shape: pl.BlockSpec(shape, lambda b, l: (0,) * len(shape))
    tok = lambda c: pl.BlockSpec((1, tm, c), lambda b, l: (b, l, 0))
    return pl.pallas_call(
        functools.partial(_out_kernel, final=final),
        grid=(bsz, seq // tm),
        in_specs=[tok(d_ssm), tok(d_dn), tok(dm), tok(p.shape[-1]),
                  full(w_out_s.shape), full(w_out_d.shape), full(w_ple.shape), full((1, dm)),
                  full(w_gate.shape), full((1, dm))],
        out_specs=tok(dm),
        out_shape=jax.ShapeDtypeStruct((bsz, seq, dm), F32),
        compiler_params=pltpu.CompilerParams(dimension_semantics=("arbitrary", "arbitrary"),
                                             vmem_limit_bytes=VMEM_LIMIT),
        name="out_proj",
    )(ys, yd, x, p, w_out_s, w_out_d, w_ple, ple_g, w_gate, final_g)


def kernel(x, p, norm_mix_g, w_in, ssm_A_re, ssm_A_im, ssm_B_re, ssm_B_im, ssm_C_re, ssm_C_im, ssm_D, ssm_log_dt, ssm_w_glu, ssm_b_glu, dn_conv_w, dn_A_log, dn_dt_bias, dn_norm_g, w_out, w_ple_proj, ple_norm_g, w_ple_gate, final_norm_g):
    bsz, seq, dm = x.shape
    depth = w_in.shape[0]
    d_ssm = ssm_D.shape[-1]
    heads = dn_A_log.shape[-1]
    d_dn = heads * DN_HEAD_DIM
    n_main = 2 * d_ssm + 4 * d_dn
    row = lambda t: t.astype(F32).reshape(1, -1)
    h = x
    for i in range(depth):
        w_main = w_in[i, :, :n_main].astype(BF16)
        w_ba = jnp.pad(w_in[i, :, n_main:], ((0, 0), (0, LANES - 2 * heads))).astype(BF16)
        us, zs, yd = _proj_gdn(h, row(norm_mix_g[i]), w_main, w_ba, dn_conv_w[i].astype(F32),
                               dn_A_log[i], dn_dt_bias[i], dn_norm_g[i], d_ssm, d_dn)

        a_r, a_i, bb_r, bb_i = _s5_prep(ssm_A_re[i], ssm_A_im[i], ssm_log_dt[i], ssm_B_re[i], ssm_B_im[i])
        bw, cw = _s5_layout(bb_r, bb_i, ssm_C_re[i], ssm_C_im[i])
        ys = _s5(us, zs, bw, cw, a_r, a_i, row(ssm_D[i]), ssm_w_glu[i].astype(BF16), row(ssm_b_glu[i]))

        h = _out_proj(ys, yd, h, p[i],
                      w_out[i, :d_ssm].astype(BF16), w_out[i, d_ssm:].astype(BF16),
                      w_ple_proj[i].astype(BF16), row(ple_norm_g[i]), w_ple_gate[i].astype(BF16),
                      row(final_norm_g), final=(i == depth - 1))
    return h
```

```python
import functools

import jax
import jax.numpy as jnp
from jax import lax
from jax.experimental import pallas as pl
from jax.experimental.pallas import tpu as pltpu

F32 = jnp.float32
BF16 = jnp.bfloat16

NORM_EPS = 1e-6
SSM_GROUP = 16
SSM_STATE = 64
DN_HEAD_DIM = 128
DN_CONV = 4
DN_CHUNK = 128

LANES = 128
SSM_LANE_BLOCK = 128
SSM_STATE_BLOCK = (SSM_LANE_BLOCK // SSM_GROUP) * SSM_STATE

PROJ_TILE = 1024
PROJ_SUBTILES = 2
S5_TIME_TILE = 64
S5_POST_SPLIT = 4
GDN_TILE = 256
GDN_BATCH_BLOCK = 2
GDN_PREP_ROWS = 64
VMEM_LIMIT = 48 * 1024 * 1024


def _dot(a, b):
    return jnp.dot(a.astype(BF16), b.astype(BF16), preferred_element_type=F32)


def _dot_nt(a, b):
    return lax.dot_general(a.astype(BF16), b.astype(BF16), (((1,), (1,)), ((), ())),
                           preferred_element_type=F32)


def _rms(x, g):
    return x * lax.rsqrt(jnp.mean(x * x, axis=-1, keepdims=True) + NORM_EPS) * g


def _s5_prep_kernel(lr_ref, li_ref, ldt_ref, br_ref, bi_ref, abr_ref, abi_ref, bbr_ref, bbi_ref):
    lr = lr_ref[...]
    li = li_ref[...]
    dt = jnp.exp(ldt_ref[...])
    mag = jnp.exp(lr * dt)
    ang = li * dt
    ab_r = mag * jnp.cos(ang)
    ab_i = mag * jnp.sin(ang)
    den = lr * lr + li * li
    nr = ab_r - 1.0
    ni = ab_i
    cr = (nr * lr + ni * li) / den
    ci = (ni * lr - nr * li) / den
    abr_ref[...] = ab_r
    abi_ref[...] = ab_i
    br = br_ref[...]
    bi = bi_ref[...]
    bbr_ref[...] = cr * br - ci * bi
    bbi_ref[...] = cr * bi + ci * br


def _s5_prep(a_re, a_im, log_dt, b_re, b_im):
    g, p = a_re.shape
    h = b_re.shape[-1]
    n = g * p
    row = lambda t: t.reshape(1, n).astype(F32)
    chan = lambda t: t.astype(F32).transpose(2, 0, 1).reshape(h, n)
    ldt = jnp.broadcast_to(log_dt.astype(F32)[:, None], (g, p))
    return pl.pallas_call(
        _s5_prep_kernel,
        out_shape=(jax.ShapeDtypeStruct((1, n), F32), jax.ShapeDtypeStruct((1, n), F32),
                   jax.ShapeDtypeStruct((h, n), F32), jax.ShapeDtypeStruct((h, n), F32)),
        name="s5_prep",
    )(row(a_re), row(a_im), row(ldt), chan(b_re), chan(b_im))


def _s5_layout(bb_r, bb_i, c_re, c_im):
    g, h, p = c_re.shape
    gpb = SSM_LANE_BLOCK // h
    nb = g // gpb
    eye = jnp.eye(gpb, dtype=F32)

    def in_w(bb):
        t = bb.reshape(h, nb, gpb, p)
        return jnp.einsum('hjkp,gk->jghkp', t, eye).reshape(nb, gpb * h, gpb * p)

    def out_w(c):
        t = c.astype(F32).reshape(nb, gpb, h, p)
        return jnp.einsum('jghp,gk->jgpkh', t, eye).reshape(nb, gpb * p, gpb * h)

    bw = jnp.concatenate([in_w(bb_r), in_w(bb_i)], axis=2).astype(BF16)
    cw = jnp.concatenate([out_w(c_re), -out_w(c_im)], axis=1).astype(BF16)
    return bw, cw


def _s5_kernel(u_hbm, z_hbm, bw_ref, cw_ref, ar_ref, ai_ref, d_ref, wglu_ref, bglu_ref, y_hbm,
               ubuf, zbuf, ybuf, sem_in, sem_out, bu_ref, s_ref, carry_ref, *, tt, bsz, nb, nsteps):
    sb = SSM_STATE_BLOCK
    rows = tt * bsz
    t = pl.program_id(0)
    slot = t % 2

    def in_copies(step, sl):
        cps = []
        for b in range(bsz):
            cps.append(pltpu.make_async_copy(u_hbm.at[b, pl.ds(step * tt, tt), :], ubuf.at[sl, :, b, :],
                                             sem_in.at[sl, 0]))
            cps.append(pltpu.make_async_copy(z_hbm.at[b, pl.ds(step * tt, tt), :], zbuf.at[sl, :, b, :],
                                             sem_in.at[sl, 1]))
        return cps

    def out_copies(step, sl):
        return [pltpu.make_async_copy(ybuf.at[sl, :, b, :], y_hbm.at[b, pl.ds(step * tt, tt), :], sem_out.at[sl])
                for b in range(bsz)]

    @pl.when(t == 0)
    def _():
        carry_ref[...] = jnp.zeros_like(carry_ref)
        for c in in_copies(0, 0):
            c.start()

    @pl.when(t >= 2)
    def _():
        for c in out_copies(t - 2, slot):
            c.wait()

    nxt = jnp.minimum(t + 1, nsteps - 1)
    for c in in_copies(nxt, 1 - slot):
        c.start()
    for c in in_copies(t, slot):
        c.wait()

    u = ubuf[slot].reshape(rows, nb * SSM_LANE_BLOCK)
    z = zbuf[slot].reshape(rows, nb * SSM_LANE_BLOCK)
    u16 = u.astype(BF16)
    for j in range(nb):
        bu_ref[:, 2 * sb * j:2 * sb * (j + 1)] = jnp.dot(
            u16[:, SSM_LANE_BLOCK * j:SSM_LANE_BLOCK * (j + 1)], bw_ref[j], preferred_element_type=F32)

    a_re = [jnp.broadcast_to(ar_ref[:, sb * j:sb * (j + 1)], (bsz, sb)) for j in range(nb)]
    a_im = [jnp.broadcast_to(ai_ref[:, sb * j:sb * (j + 1)], (bsz, sb)) for j in range(nb)]
    state = [(carry_ref[:, 2 * sb * j:2 * sb * j + sb], carry_ref[:, 2 * sb * j + sb:2 * sb * (j + 1)])
             for j in range(nb)]
    th = tt // S5_POST_SPLIT
    for part in range(S5_POST_SPLIT):
        for j in range(nb):
            cr = 2 * sb * j
            ci = cr + sb
            s_r, s_i = state[j]
            for i in range(part * th, (part + 1) * th):
                r = slice(i * bsz, (i + 1) * bsz)
                s_r, s_i = (a_re[j] * s_r - a_im[j] * s_i + bu_ref[r, cr:cr + sb],
                            a_re[j] * s_i + a_im[j] * s_r + bu_ref[r, ci:ci + sb])
                s_ref[r, cr:cr + sb] = s_r.astype(BF16)
                s_ref[r, ci:ci + sb] = s_i.astype(BF16)
            state[j] = (s_r, s_i)
        r = slice(part * th * bsz, (part + 1) * th * bsz)
        y = jnp.concatenate(
            [jnp.dot(s_ref[r, 2 * sb * j:2 * sb * (j + 1)], cw_ref[j], preferred_element_type=F32)
             for j in range(nb)], axis=1)
        y = y + d_ref[...] * u[r]
        y = jax.nn.gelu(y)
        y = y * jax.nn.sigmoid(_dot(y, wglu_ref[...]) + bglu_ref[...])
        y = y * jax.nn.silu(z[r])
        ybuf[slot, part * th:(part + 1) * th] = y.reshape(th, bsz, y.shape[-1])
    for j in range(nb):
        carry_ref[:, 2 * sb * j:2 * sb * j + sb] = state[j][0]
        carry_ref[:, 2 * sb * j + sb:2 * sb * (j + 1)] = state[j][1]

    for c in out_copies(t, slot):
        c.start()

    @pl.when(t == nsteps - 1)
    def _():
        for c in in_copies(nxt, 1 - slot):
            c.wait()
        for c in out_copies(t, slot):
            c.wait()
        if nsteps >= 2:
            for c in out_copies(t - 1, 1 - slot):
                c.wait()


def _s5(us, zs, bw, cw, a_r, a_i, d_vec, w_glu, b_glu):
    bsz, seq, d_ssm = us.shape
    nb = bw.shape[0]
    tt = min(S5_TIME_TILE, seq)
    rows = tt * bsz
    nsteps = seq // tt
    n_state = 2 * SSM_STATE_BLOCK * nb
    full = lambda shape: pl.BlockSpec(shape, lambda t: (0,) * len(shape))
    hbm = pl.BlockSpec(memory_space=pl.ANY)
    return pl.pallas_call(
        functools.partial(_s5_kernel, tt=tt, bsz=bsz, nb=nb, nsteps=nsteps),
        grid=(nsteps,),
        in_specs=[hbm, hbm, full(bw.shape), full(cw.shape), full(a_r.shape), full(a_i.shape),
                  full(d_vec.shape), full(w_glu.shape), full(b_glu.shape)],
        out_specs=hbm,
        out_shape=jax.ShapeDtypeStruct((bsz, seq, d_ssm), F32),
        scratch_shapes=[pltpu.VMEM((2, tt, bsz, d_ssm), F32),
                        pltpu.VMEM((2, tt, bsz, d_ssm), F32),
                        pltpu.VMEM((2, tt, bsz, d_ssm), F32),
                        pltpu.SemaphoreType.DMA((2, 2)),
                        pltpu.SemaphoreType.DMA((2,)),
                        pltpu.VMEM((rows, n_state), F32),
                        pltpu.VMEM((rows, n_state), BF16),
                        pltpu.VMEM((bsz, n_state), F32)],
        compiler_params=pltpu.CompilerParams(dimension_semantics=("arbitrary",),
                                             vmem_limit_bytes=VMEM_LIMIT),
        name="s5",
    )(us, zs, bw, cw, a_r, a_i, d_vec, w_glu, b_glu)


def _chunk_sums(g, sum_ref):
    n = g.shape[0]
    hi = g.astype(BF16).astype(F32)
    mid = (g - hi).astype(BF16).astype(F32)
    lo = g - hi - mid
    parts = jnp.concatenate([hi, mid, lo], axis=0).astype(BF16)
    s = jnp.dot(parts, sum_ref[...], preferred_element_type=F32)
    s = s[:n] + s[n:2 * n] + s[2 * n:]
    tc = g.shape[1]
    return s[:, :tc], s[:, tc:]


_M_CAUSAL, _M_STRICT, _M_EYE, _M_PAIR = 0, 1, 2, 3


def _gdn_fill_masks(mask_ref, lvl_ref, sum_ref, tc):
    ri = lax.broadcasted_iota(jnp.int32, (tc, tc), 0)
    ci = lax.broadcasted_iota(jnp.int32, (tc, tc), 1)
    same = (ri // DN_CHUNK) == (ci // DN_CHUNK)
    mask_ref[_M_CAUSAL] = (same & (ri >= ci)).astype(F32)
    mask_ref[_M_STRICT] = (same & (ri > ci)).astype(F32)
    mask_ref[_M_EYE] = (ri == ci).astype(F32)
    sum_ref[:, :tc] = (same & (ri <= ci)).astype(F32).astype(BF16)
    sum_ref[:, tc:] = (same & (ri > ci)).astype(F32).astype(BF16)
    s = 1
    lvl = 0
    while s < DN_CHUNK:
        m = (((ri // (2 * s)) == (ci // (2 * s))) & (((ri // s) % 2) == 1) & (((ci // s) % 2) == 0)).astype(F32)
        if lvl == 0:
            mask_ref[_M_PAIR] = m
        else:
            lvl_ref[lvl - 1] = m.astype(BF16)
        s *= 2
        lvl += 1


def _gdn_prep(bb, qkv_ref, ba_ref, alog_ref, dtb_ref, mask_ref, sum_ref, tc, heads):
    hd = DN_HEAD_DIM
    d_dn = heads * hd
    ba = ba_ref[bb]
    beta_r = jax.nn.sigmoid(ba)
    g_r = -jnp.exp(alog_ref[...]) * jax.nn.softplus(ba + dtb_ref[...])
    gc_r, gs_r = _chunk_sums(g_r, sum_ref)
    egs_r = jnp.exp(gs_r)
    egl_r = jnp.exp(gc_r + gs_r)
    pad = jnp.zeros((LANES - 3 * 2 * heads, tc), F32)
    cols = jnp.concatenate([beta_r, gc_r, egl_r, pad], axis=0).T
    out = []
    for h in range(heads):
        q = qkv_ref[bb, :, hd * h:hd * (h + 1)]
        k = qkv_ref[bb, :, d_dn + hd * h:d_dn + hd * (h + 1)]
        v = qkv_ref[bb, :, 2 * d_dn + hd * h:2 * d_dn + hd * (h + 1)]
        beta_c = cols[:, h:h + 1]
        gc_c = cols[:, 3 * heads + h:3 * heads + h + 1]
        gc_row = gc_r[heads + h:heads + h + 1, :]
        eg_c = jnp.exp(gc_c)
        kb = k * beta_c
        k16 = k.astype(BF16)
        attn16, a16, t16 = [], [], []
        for r0 in range(0, tc, GDN_PREP_ROWS):
            r = slice(r0, r0 + GDN_PREP_ROWS)
            decay = jnp.exp(jnp.minimum(gc_c[r] - gc_row, 0.0))
            qk_kk = _dot_nt(jnp.concatenate([q[r], k[r]], axis=0), k16)
            a = qk_kk[GDN_PREP_ROWS:] * (beta_c[r] * decay) * mask_ref[_M_STRICT, r]
            attn16.append((qk_kk[:GDN_PREP_ROWS] * decay * mask_ref[_M_CAUSAL, r]).astype(BF16))
            a16.append(a.astype(BF16))
            t16.append((mask_ref[_M_EYE, r] - a * mask_ref[_M_PAIR, r]).astype(BF16))
        out.append(dict(
            egl_c=cols[:, 5 * heads + h:5 * heads + h + 1],
            qg16=(q * eg_c).astype(BF16),
            rhs16=jnp.concatenate([v * beta_c, kb * eg_c], axis=1).astype(BF16),
            kdec16=(k.T * egs_r[heads + h:heads + h + 1, :]).astype(BF16),
            attn16=jnp.concatenate(attn16, axis=0),
            a16=jnp.concatenate(a16, axis=0),
            t16=jnp.concatenate(t16, axis=0)))
    return out


def _gdn_body(units, zd_ref, ng_ref, y_ref, state_ref, lvl_ref, *, tc, heads):
    hd = DN_HEAD_DIM
    nchunk = tc // DN_CHUNK

    for lvl in range(lvl_ref.shape[0]):
        for u in units:
            u["x16"] = jnp.dot(u["a16"], u["t16"], preferred_element_type=F32).astype(BF16)
        for u in units:
            y16 = jnp.dot(u["t16"], u["x16"], preferred_element_type=F32).astype(BF16)
            u["t16"] = u["t16"] - y16 * lvl_ref[lvl]
    for u in units:
        sol = jnp.dot(u["t16"], u["rhs16"], preferred_element_type=F32)
        u["u"] = sol[:, :hd]
        u["w16"] = sol[:, hd:].astype(BF16)
        u["st"] = state_ref[u["bb"] * heads + u["h"]]
        u["outs"] = []

    for n in range(nchunk):
        r0 = n * DN_CHUNK
        r1 = r0 + DN_CHUNK
        for u in units:
            u["ws_qs"] = jnp.dot(jnp.concatenate([u["w16"][r0:r1], u["qg16"][r0:r1]], axis=0),
                                 u["st"].astype(BF16), preferred_element_type=F32)
        for u in units:
            v_new = (u["u"][r0:r1] - u["ws_qs"][:DN_CHUNK]).astype(BF16)
            u["av_kv"] = jnp.dot(jnp.concatenate([u["attn16"][r0:r1, r0:r1], u["kdec16"][:, r0:r1]], axis=0),
                                 v_new, preferred_element_type=F32)
        for u in units:
            u["outs"].append(u["ws_qs"][DN_CHUNK:] + u["av_kv"][:DN_CHUNK])
            u["st"] = u["st"] * u["egl_c"][r0:r1] + u["av_kv"][DN_CHUNK:]
    for u in units:
        bb, h = u["bb"], u["h"]
        state_ref[bb * heads + h] = u["st"]
        o = jnp.concatenate(u["outs"], axis=0)
        z = zd_ref[bb, :, hd * h:hd * (h + 1)]
        y_ref[bb, :, hd * h:hd * (h + 1)] = _rms(o, ng_ref[...]) * jax.nn.silu(z)


def _proj_gdn_kernel(x_ref, g_ref, wm_ref, wba_ref, cw_ref, cs_ref, alog_ref, dtb_ref, ng_ref,
                     us_ref, zs_ref, y_ref,
                     tail_ref, xpad_ref, qkv_ref, zd_ref, ba_ref, state_ref, mask_ref, lvl_ref, sum_ref,
                     *, d_ssm, d_dn, tc, heads, nbb):
    hd = DN_HEAD_DIM
    nslab = 3 * d_dn // LANES
    nrow = ba_ref.shape[1]

    @pl.when(pl.program_id(1) == 0)
    def _():
        tail_ref[...] = jnp.zeros_like(tail_ref)
        state_ref[...] = jnp.zeros_like(state_ref)
        _gdn_fill_masks(mask_ref, lvl_ref, sum_ref, tc)

    a = _rms(x_ref[...].reshape(nbb * tc, x_ref.shape[-1]), g_ref[...]).astype(BF16)
    units = []
    for bb in range(nbb):
        rows = slice(bb * tc, (bb + 1) * tc)
        qkv = jnp.dot(a[rows], wm_ref[:, 2 * d_ssm:2 * d_ssm + 3 * d_dn], preferred_element_type=F32)
        ba_ref[bb] = jnp.dot(a[rows], wba_ref[...], preferred_element_type=F32).T[:nrow]
        for s in range(nslab):
            lanes = slice(LANES * s, LANES * (s + 1))
            xpad_ref[bb, s, 0:8, :] = tail_ref[bb, s]
            xpad_ref[bb, s, 8:, :] = qkv[:, lanes]
            tail_ref[bb, s] = xpad_ref[bb, s, tc:, :]
            for r0 in range(0, tc, GDN_PREP_ROWS):
                acc = cw_ref[DN_CONV - 1:DN_CONV, lanes] * xpad_ref[bb, s, 8 + r0:8 + r0 + GDN_PREP_ROWS, :]
                for j in range(DN_CONV - 1):
                    acc = acc + cw_ref[j:j + 1, lanes] * xpad_ref[bb, s, pl.ds(cs_ref[j] + r0, GDN_PREP_ROWS), :]
                t = jax.nn.silu(acc)
                if LANES * s < 2 * d_dn:
                    t = t * lax.rsqrt(jnp.sum(t * t, axis=-1, keepdims=True) + NORM_EPS)
                    if LANES * s < d_dn:
                        t = t * (hd ** -0.5)
                qkv_ref[bb, r0:r0 + GDN_PREP_ROWS, lanes] = t

        for h, d in enumerate(_gdn_prep(bb, qkv_ref, ba_ref, alog_ref, dtb_ref, mask_ref, sum_ref, tc, heads)):
            units.append(dict(d, bb=bb, h=h))

    us_ref[...] = jnp.dot(a, wm_ref[:, :d_ssm], preferred_element_type=F32).reshape(us_ref.shape)
    zs_ref[...] = jnp.dot(a, wm_ref[:, d_ssm:2 * d_ssm], preferred_element_type=F32).reshape(zs_ref.shape)
    zd_ref[...] = jnp.dot(a, wm_ref[:, 2 * d_ssm + 3 * d_dn:], preferred_element_type=F32).reshape(zd_ref.shape)
    _gdn_body(units, zd_ref, ng_ref, y_ref, state_ref, lvl_ref, tc=tc, heads=heads)


def _proj_gdn(x, g, w_main, w_ba, conv_w, a_log, dt_bias, norm_g, d_ssm, d_dn):
    bsz, seq, dm = x.shape
    heads = d_dn // DN_HEAD_DIM
    tc = min(GDN_TILE, seq)
    nbb = GDN_BATCH_BLOCK
    nrow = 2 * heads
    nslab = 3 * d_dn // LANES
    assert DN_CHUNK == DN_HEAD_DIM and tc % DN_CHUNK == 0 and bsz % nbb == 0
    n_lvl = DN_CHUNK.bit_length() - 2
    col = lambda t: jnp.concatenate([jnp.zeros((heads,), F32), t.astype(F32)]).reshape(nrow, 1)
    full = lambda shape: pl.BlockSpec(shape, lambda b, l: (0,) * len(shape))
    tok = lambda c: pl.BlockSpec((nbb, tc, c), lambda b, l: (b, l, 0))
    conv_starts = 8 - (DN_CONV - 1) + jnp.arange(DN_CONV - 1, dtype=jnp.int32)
    return pl.pallas_call(
        functools.partial(_proj_gdn_kernel, d_ssm=d_ssm, d_dn=d_dn, tc=tc, heads=heads, nbb=nbb),
        grid=(bsz // nbb, seq // tc),
        in_specs=[tok(dm), full((1, dm)), full(w_main.shape), full(w_ba.shape), full(conv_w.shape),
                  pl.BlockSpec(memory_space=pltpu.SMEM),
                  full((nrow, 1)), full((nrow, 1)), full((1, DN_HEAD_DIM))],
        out_specs=[tok(d_ssm), tok(d_ssm), tok(d_dn)],
        out_shape=(jax.ShapeDtypeStruct((bsz, seq, d_ssm), F32),
                   jax.ShapeDtypeStruct((bsz, seq, d_ssm), F32),
                   jax.ShapeDtypeStruct((bsz, seq, d_dn), F32)),
        scratch_shapes=[pltpu.VMEM((nbb, nslab, 8, LANES), F32),
                        pltpu.VMEM((nbb, nslab, 8 + tc, LANES), F32),
                        pltpu.VMEM((nbb, tc, 3 * d_dn), F32),
                        pltpu.VMEM((nbb, tc, d_dn), F32),
                        pltpu.VMEM((nbb, nrow, tc), F32),
                        pltpu.VMEM((nbb * heads, DN_HEAD_DIM, DN_HEAD_DIM), F32),
                        pltpu.VMEM((_M_PAIR + 1, tc, tc), F32),
                        pltpu.VMEM((n_lvl, tc, tc), BF16),
                        pltpu.VMEM((tc, 2 * tc), BF16)],
        compiler_params=pltpu.CompilerParams(dimension_semantics=("arbitrary", "arbitrary"),
                                             vmem_limit_bytes=VMEM_LIMIT),
        name="proj_gdn",
    )(x, g, w_main, w_ba, conv_w, conv_starts, col(a_log), col(dt_bias),
      norm_g.astype(F32).reshape(1, DN_HEAD_DIM))


def _out_kernel(ys_ref, yd_ref, x_ref, p_ref, wos_ref, wod_ref, wp_ref, pg_ref, wg_ref, fg_ref, o_ref, *, final):
    tm = x_ref.shape[1]
    rows = tm // PROJ_SUBTILES
    for s in range(PROJ_SUBTILES):
        r = slice(s * rows, (s + 1) * rows)
        h = x_ref[0, r] + _dot(ys_ref[0, r], wos_ref[...]) + _dot(yd_ref[0, r], wod_ref[...])
        e = _rms(_dot(p_ref[0, r], wp_ref[...]), pg_ref[...])
        h = h + jax.nn.sigmoid(_dot(h, wg_ref[...])) * e
        o_ref[0, r] = _rms(h, fg_ref[...]) if final else h


def _out_proj(ys, yd, x, p, w_out_s, w_out_d, w_ple, ple_g, w_gate, final_g, final):
    bsz, seq, dm = x.shape
    d_ssm = w_out_s.shape[0]
    d_dn = w_out_d.shape[0]
    tm = min(PROJ_TILE, seq)
    full = lambda shape: pl.BlockSpec(shape, lambda b, l: (0,) * len(shape))
    tok = lambda c: pl.BlockSpec((1, tm, c), lambda b, l: (b, l, 0))
    return pl.pallas_call(
        functools.partial(_out_kernel, final=final),
        grid=(bsz, seq // tm),
        in_specs=[tok(d_ssm), tok(d_dn), tok(dm), tok(p.shape[-1]),
                  full(w_out_s.shape), full(w_out_d.shape), full(w_ple.shape), full((1, dm)),
                  full(w_gate.shape), full((1, dm))],
        out_specs=tok(dm),
        out_shape=jax.ShapeDtypeStruct((bsz, seq, dm), F32),
        compiler_params=pltpu.CompilerParams(dimension_semantics=("arbitrary", "arbitrary"),
                                             vmem_limit_bytes=VMEM_LIMIT),
        name="out_proj",
    )(ys, yd, x, p, w_out_s, w_out_d, w_ple, ple_g, w_gate, final_g)


def kernel(x, p, norm_mix_g, w_in, ssm_A_re, ssm_A_im, ssm_B_re, ssm_B_im, ssm_C_re, ssm_C_im, ssm_D, ssm_log_dt, ssm_w_glu, ssm_b_glu, dn_conv_w, dn_A_log, dn_dt_bias, dn_norm_g, w_out, w_ple_proj, ple_norm_g, w_ple_gate, final_norm_g):
    bsz, seq, dm = x.shape
    depth = w_in.shape[0]
    d_ssm = ssm_D.shape[-1]
    heads = dn_A_log.shape[-1]
    d_dn = heads * DN_HEAD_DIM
    n_main = 2 * d_ssm + 4 * d_dn
    row = lambda t: t.astype(F32).reshape(1, -1)
    h = x
    for i in range(depth):
        w_main = w_in[i, :, :n_main].astype(BF16)
        w_ba = jnp.pad(w_in[i, :, n_main:], ((0, 0), (0, LANES - 2 * heads))).astype(BF16)
        us, zs, yd = _proj_gdn(h, row(norm_mix_g[i]), w_main, w_ba, dn_conv_w[i].astype(F32),
                               dn_A_log[i], dn_dt_bias[i], dn_norm_g[i], d_ssm, d_dn)

        a_r, a_i, bb_r, bb_i = _s5_prep(ssm_A_re[i], ssm_A_im[i], ssm_log_dt[i], ssm_B_re[i], ssm_B_im[i])
        bw, cw = _s5_layout(bb_r, bb_i, ssm_C_re[i], ssm_C_im[i])
        ys = _s5(us, zs, bw, cw, a_r, a_i, row(ssm_D[i]), ssm_w_glu[i].astype(BF16), row(ssm_b_glu[i]))

        h = _out_proj(ys, yd, h, p[i],
                      w_out[i, :d_ssm].astype(BF16), w_out[i, d_ssm:].astype(BF16),
                      w_ple_proj[i].astype(BF16), row(ple_norm_g[i]), w_ple_gate[i].astype(BF16),
                      row(final_norm_g), final=(i == depth - 1))
    return h
```

```python
import functools

import jax
import jax.numpy as jnp
from jax import lax
from jax.experimental import pallas as pl
from jax.experimental.pallas import tpu as pltpu

F32 = jnp.float32
BF16 = jnp.bfloat16

NORM_EPS = 1e-6
SSM_GROUP = 16
SSM_STATE = 64
DN_HEAD_DIM = 128
DN_CONV = 4
DN_CHUNK = 128

LANES = 128
SSM_LANE_BLOCK = 128
SSM_STATE_BLOCK = (SSM_LANE_BLOCK // SSM_GROUP) * SSM_STATE

PROJ_TILE = 1024
PROJ_SUBTILES = 2
S5_TIME_TILE = 64
S5_POST_SPLIT = 4
GDN_TILE = 128
GDN_BATCH_BLOCK = 4
GDN_PREP_ROWS = 64
VMEM_LIMIT = 48 * 1024 * 1024


def _dot(a, b):
    return jnp.dot(a.astype(BF16), b.astype(BF16), preferred_element_type=F32)


def _dot_nt(a, b):
    return lax.dot_general(a.astype(BF16), b.astype(BF16), (((1,), (1,)), ((), ())),
                           preferred_element_type=F32)


def _rms(x, g):
    return x * lax.rsqrt(jnp.mean(x * x, axis=-1, keepdims=True) + NORM_EPS) * g


def _s5_prep_kernel(lr_ref, li_ref, ldt_ref, br_ref, bi_ref, abr_ref, abi_ref, bbr_ref, bbi_ref):
    lr = lr_ref[...]
    li = li_ref[...]
    dt = jnp.exp(ldt_ref[...])
    mag = jnp.exp(lr * dt)
    ang = li * dt
    ab_r = mag * jnp.cos(ang)
    ab_i = mag * jnp.sin(ang)
    den = lr * lr + li * li
    nr = ab_r - 1.0
    ni = ab_i
    cr = (nr * lr + ni * li) / den
    ci = (ni * lr - nr * li) / den
    abr_ref[...] = ab_r
    abi_ref[...] = ab_i
    br = br_ref[...]
    bi = bi_ref[...]
    bbr_ref[...] = cr * br - ci * bi
    bbi_ref[...] = cr * bi + ci * br


def _s5_prep(a_re, a_im, log_dt, b_re, b_im):
    g, p = a_re.shape
    h = b_re.shape[-1]
    n = g * p
    row = lambda t: t.reshape(1, n).astype(F32)
    chan = lambda t: t.astype(F32).transpose(2, 0, 1).reshape(h, n)
    ldt = jnp.broadcast_to(log_dt.astype(F32)[:, None], (g, p))
    return pl.pallas_call(
        _s5_prep_kernel,
        out_shape=(jax.ShapeDtypeStruct((1, n), F32), jax.ShapeDtypeStruct((1, n), F32),
                   jax.ShapeDtypeStruct((h, n), F32), jax.ShapeDtypeStruct((h, n), F32)),
        name="s5_prep",
    )(row(a_re), row(a_im), row(ldt), chan(b_re), chan(b_im))


def _s5_layout(bb_r, bb_i, c_re, c_im):
    g, h, p = c_re.shape
    gpb = SSM_LANE_BLOCK // h
    nb = g // gpb
    eye = jnp.eye(gpb, dtype=F32)

    def in_w(bb):
        t = bb.reshape(h, nb, gpb, p)
        return jnp.einsum('hjkp,gk->jghkp', t, eye).reshape(nb, gpb * h, gpb * p)

    def out_w(c):
        t = c.astype(F32).reshape(nb, gpb, h, p)
        return jnp.einsum('jghp,gk->jgpkh', t, eye).reshape(nb, gpb * p, gpb * h)

    bw = jnp.concatenate([in_w(bb_r), in_w(bb_i)], axis=2).astype(BF16)
    cw = jnp.concatenate([out_w(c_re), -out_w(c_im)], axis=1).astype(BF16)
    return bw, cw


def _s5_kernel(u_hbm, z_hbm, bw_ref, cw_ref, ar_ref, ai_ref, d_ref, wglu_ref, bglu_ref, y_hbm,
               ubuf, zbuf, ybuf, sem_in, sem_out, bu_ref, s_ref, carry_ref, *, tt, bsz, nb, nsteps):
    sb = SSM_STATE_BLOCK
    rows = tt * bsz
    t = pl.program_id(0)
    slot = t % 2

    def in_copies(step, sl):
        cps = []
        for b in range(bsz):
            cps.append(pltpu.make_async_copy(u_hbm.at[b, pl.ds(step * tt, tt), :], ubuf.at[sl, :, b, :],
                                             sem_in.at[sl, 0]))
            cps.append(pltpu.make_async_copy(z_hbm.at[b, pl.ds(step * tt, tt), :], zbuf.at[sl, :, b, :],
                                             sem_in.at[sl, 1]))
        return cps

    def out_copies(step, sl):
        return [pltpu.make_async_copy(ybuf.at[sl, :, b, :], y_hbm.at[b, pl.ds(step * tt, tt), :], sem_out.at[sl])
                for b in range(bsz)]

    @pl.when(t == 0)
    def _():
        carry_ref[...] = jnp.zeros_like(carry_ref)
        for c in in_copies(0, 0):
            c.start()

    @pl.when(t >= 2)
    def _():
        for c in out_copies(t - 2, slot):
            c.wait()

    nxt = jnp.minimum(t + 1, nsteps - 1)
    for c in in_copies(nxt, 1 - slot):
        c.start()
    for c in in_copies(t, slot):
        c.wait()

    u = ubuf[slot].reshape(rows, nb * SSM_LANE_BLOCK)
    z = zbuf[slot].reshape(rows, nb * SSM_LANE_BLOCK)
    u16 = u.astype(BF16)
    for j in range(nb):
        bu_ref[:, 2 * sb * j:2 * sb * (j + 1)] = jnp.dot(
            u16[:, SSM_LANE_BLOCK * j:SSM_LANE_BLOCK * (j + 1)], bw_ref[j], preferred_element_type=F32)

    a_re = [jnp.broadcast_to(ar_ref[:, sb * j:sb * (j + 1)], (bsz, sb)) for j in range(nb)]
    a_im = [jnp.broadcast_to(ai_ref[:, sb * j:sb * (j + 1)], (bsz, sb)) for j in range(nb)]
    state = [(carry_ref[:, 2 * sb * j:2 * sb * j + sb], carry_ref[:, 2 * sb * j + sb:2 * sb * (j + 1)])
             for j in range(nb)]
    th = tt // S5_POST_SPLIT
    for part in range(S5_POST_SPLIT):
        for j in range(nb):
            cr = 2 * sb * j
            ci = cr + sb
            s_r, s_i = state[j]
            for i in range(part * th, (part + 1) * th):
                r = slice(i * bsz, (i + 1) * bsz)
                s_r, s_i = (a_re[j] * s_r - a_im[j] * s_i + bu_ref[r, cr:cr + sb],
                            a_re[j] * s_i + a_im[j] * s_r + bu_ref[r, ci:ci + sb])
                s_ref[r, cr:cr + sb] = s_r.astype(BF16)
                s_ref[r, ci:ci + sb] = s_i.astype(BF16)
            state[j] = (s_r, s_i)
        r = slice(part * th * bsz, (part + 1) * th * bsz)
        y = jnp.concatenate(
            [jnp.dot(s_ref[r, 2 * sb * j:2 * sb * (j + 1)], cw_ref[j], preferred_element_type=F32)
             for j in range(nb)], axis=1)
        y = y + d_ref[...] * u[r]
        y = jax.nn.gelu(y)
        y = y * jax.nn.sigmoid(_dot(y, wglu_ref[...]) + bglu_ref[...])
        y = y * jax.nn.silu(z[r])
        ybuf[slot, part * th:(part + 1) * th] = y.reshape(th, bsz, y.shape[-1])
    for j in range(nb):
        carry_ref[:, 2 * sb * j:2 * sb * j + sb] = state[j][0]
        carry_ref[:, 2 * sb * j + sb:2 * sb * (j + 1)] = state[j][1]

    for c in out_copies(t, slot):
        c.start()

    @pl.when(t == nsteps - 1)
    def _():
        for c in in_copies(nxt, 1 - slot):
            c.wait()
        for c in out_copies(t, slot):
            c.wait()
        if nsteps >= 2:
            for c in out_copies(t - 1, 1 - slot):
                c.wait()


def _s5(us, zs, bw, cw, a_r, a_i, d_vec, w_glu, b_glu):
    bsz, seq, d_ssm = us.shape
    nb = bw.shape[0]
    tt = min(S5_TIME_TILE, seq)
    rows = tt * bsz
    nsteps = seq // tt
    n_state = 2 * SSM_STATE_BLOCK * nb
    full = lambda shape: pl.BlockSpec(shape, lambda t: (0,) * len(shape))
    hbm = pl.BlockSpec(memory_space=pl.ANY)
    return pl.pallas_call(
        functools.partial(_s5_kernel, tt=tt, bsz=bsz, nb=nb, nsteps=nsteps),
        grid=(nsteps,),
        in_specs=[hbm, hbm, full(bw.shape), full(cw.shape), full(a_r.shape), full(a_i.shape),
                  full(d_vec.shape), full(w_glu.shape), full(b_glu.shape)],
        out_specs=hbm,
        out_shape=jax.ShapeDtypeStruct((bsz, seq, d_ssm), F32),
        scratch_shapes=[pltpu.VMEM((2, tt, bsz, d_ssm), F32),
                        pltpu.VMEM((2, tt, bsz, d_ssm), F32),
                        pltpu.VMEM((2, tt, bsz, d_ssm), F32),
                        pltpu.SemaphoreType.DMA((2, 2)),
                        pltpu.SemaphoreType.DMA((2,)),
                        pltpu.VMEM((rows, n_state), F32),
                        pltpu.VMEM((rows, n_state), BF16),
                        pltpu.VMEM((bsz, n_state), F32)],
        compiler_params=pltpu.CompilerParams(dimension_semantics=("arbitrary",),
                                             vmem_limit_bytes=VMEM_LIMIT),
        name="s5",
    )(us, zs, bw, cw, a_r, a_i, d_vec, w_glu, b_glu)


def _chunk_sums(g, sum_ref):
    n = g.shape[0]
    hi = g.astype(BF16).astype(F32)
    mid = (g - hi).astype(BF16).astype(F32)
    lo = g - hi - mid
    parts = jnp.concatenate([hi, mid, lo], axis=0).astype(BF16)
    s = jnp.dot(parts, sum_ref[...], preferred_element_type=F32)
    s = s[:n] + s[n:2 * n] + s[2 * n:]
    tc = g.shape[1]
    return s[:, :tc], s[:, tc:]


_M_CAUSAL, _M_STRICT, _M_EYE, _M_PAIR = 0, 1, 2, 3


def _gdn_fill_masks(mask_ref, lvl_ref, sum_ref, tc):
    ri = lax.broadcasted_iota(jnp.int32, (tc, tc), 0)
    ci = lax.broadcasted_iota(jnp.int32, (tc, tc), 1)
    same = (ri // DN_CHUNK) == (ci // DN_CHUNK)
    mask_ref[_M_CAUSAL] = (same & (ri >= ci)).astype(F32)
    mask_ref[_M_STRICT] = (same & (ri > ci)).astype(F32)
    mask_ref[_M_EYE] = (ri == ci).astype(F32)
    sum_ref[:, :tc] = (same & (ri <= ci)).astype(F32).astype(BF16)
    sum_ref[:, tc:] = (same & (ri > ci)).astype(F32).astype(BF16)
    s = 1
    lvl = 0
    while s < DN_CHUNK:
        m = (((ri // (2 * s)) == (ci // (2 * s))) & (((ri // s) % 2) == 1) & (((ci // s) % 2) == 0)).astype(F32)
        if lvl == 0:
            mask_ref[_M_PAIR] = m
        else:
            lvl_ref[lvl - 1] = m.astype(BF16)
        s *= 2
        lvl += 1


def _gdn_prep(bb, qkv_ref, ba_ref, alog_ref, dtb_ref, mask_ref, sum_ref, tc, heads):
    hd = DN_HEAD_DIM
    d_dn = heads * hd
    ba = ba_ref[bb]
    beta_r = jax.nn.sigmoid(ba)
    g_r = -jnp.exp(alog_ref[...]) * jax.nn.softplus(ba + dtb_ref[...])
    gc_r, gs_r = _chunk_sums(g_r, sum_ref)
    egs_r = jnp.exp(gs_r)
    egl_r = jnp.exp(gc_r + gs_r)
    pad = jnp.zeros((LANES - 3 * 2 * heads, tc), F32)
    cols = jnp.concatenate([beta_r, gc_r, egl_r, pad], axis=0).T
    out = []
    for h in range(heads):
        q = qkv_ref[bb, :, hd * h:hd * (h + 1)]
        k = qkv_ref[bb, :, d_dn + hd * h:d_dn + hd * (h + 1)]
        v = qkv_ref[bb, :, 2 * d_dn + hd * h:2 * d_dn + hd * (h + 1)]
        beta_c = cols[:, h:h + 1]
        gc_c = cols[:, 3 * heads + h:3 * heads + h + 1]
        gc_row = gc_r[heads + h:heads + h + 1, :]
        eg_c = jnp.exp(gc_c)
        kb = k * beta_c
        k16 = k.astype(BF16)
        attn16, a16, t16 = [], [], []
        for r0 in range(0, tc, GDN_PREP_ROWS):
            r = slice(r0, r0 + GDN_PREP_ROWS)
            decay = jnp.exp(jnp.minimum(gc_c[r] - gc_row, 0.0))
            qk_kk = _dot_nt(jnp.concatenate([q[r], k[r]], axis=0), k16)
            a = qk_kk[GDN_PREP_ROWS:] * (beta_c[r] * decay) * mask_ref[_M_STRICT, r]
            attn16.append((qk_kk[:GDN_PREP_ROWS] * decay * mask_ref[_M_CAUSAL, r]).astype(BF16))
            a16.append(a.astype(BF16))
            t16.append((mask_ref[_M_EYE, r] - a * mask_ref[_M_PAIR, r]).astype(BF16))
        out.append(dict(
            egl_c=cols[:, 5 * heads + h:5 * heads + h + 1],
            qg16=(q * eg_c).astype(BF16),
            rhs16=jnp.concatenate([v * beta_c, kb * eg_c], axis=1).astype(BF16),
            kdec16=(k.T * egs_r[heads + h:heads + h + 1, :]).astype(BF16),
            attn16=jnp.concatenate(attn16, axis=0),
            a16=jnp.concatenate(a16, axis=0),
            t16=jnp.concatenate(t16, axis=0)))
    return out


def _gdn_body(units, zd_ref, ng_ref, y_ref, state_ref, lvl_ref, *, tc, heads):
    hd = DN_HEAD_DIM
    nchunk = tc // DN_CHUNK

    for lvl in range(lvl_ref.shape[0]):
        for u in units:
            u["x16"] = jnp.dot(u["a16"], u["t16"], preferred_element_type=F32).astype(BF16)
        for u in units:
            y16 = jnp.dot(u["t16"], u["x16"], preferred_element_type=F32).astype(BF16)
            u["t16"] = u["t16"] - y16 * lvl_ref[lvl]
    for u in units:
        sol = jnp.dot(u["t16"], u["rhs16"], preferred_element_type=F32)
        u["u"] = sol[:, :hd]
        u["w16"] = sol[:, hd:].astype(BF16)
        u["st"] = state_ref[u["bb"] * heads + u["h"]]
        u["outs"] = []

    for n in range(nchunk):
        r0 = n * DN_CHUNK
        r1 = r0 + DN_CHUNK
        for u in units:
            u["ws_qs"] = jnp.dot(jnp.concatenate([u["w16"][r0:r1], u["qg16"][r0:r1]], axis=0),
                                 u["st"].astype(BF16), preferred_element_type=F32)
        for u in units:
            v_new = (u["u"][r0:r1] - u["ws_qs"][:DN_CHUNK]).astype(BF16)
            u["av_kv"] = jnp.dot(jnp.concatenate([u["attn16"][r0:r1, r0:r1], u["kdec16"][:, r0:r1]], axis=0),
                                 v_new, preferred_element_type=F32)
        for u in units:
            u["outs"].append(u["ws_qs"][DN_CHUNK:] + u["av_kv"][:DN_CHUNK])
            u["st"] = u["st"] * u["egl_c"][r0:r1] + u["av_kv"][DN_CHUNK:]
    for u in units:
        bb, h = u["bb"], u["h"]
        state_ref[bb * heads + h] = u["st"]
        o = jnp.concatenate(u["outs"], axis=0)
        z = zd_ref[bb, :, hd * h:hd * (h + 1)]
        y_ref[bb, :, hd * h:hd * (h + 1)] = _rms(o, ng_ref[...]) * jax.nn.silu(z)


def _proj_gdn_kernel(x_ref, g_ref, wm_ref, wba_ref, cw_ref, cs_ref, alog_ref, dtb_ref, ng_ref,
                     us_ref, zs_ref, y_ref,
                     tail_ref, xpad_ref, qkv_ref, zd_ref, ba_ref, state_ref, mask_ref, lvl_ref, sum_ref,
                     *, d_ssm, d_dn, tc, heads, nbb):
    hd = DN_HEAD_DIM
    nslab = 3 * d_dn // LANES
    nrow = ba_ref.shape[1]

    @pl.when(pl.program_id(1) == 0)
    def _():
        tail_ref[...] = jnp.zeros_like(tail_ref)
        state_ref[...] = jnp.zeros_like(state_ref)
        _gdn_fill_masks(mask_ref, lvl_ref, sum_ref, tc)

    a = _rms(x_ref[...].reshape(nbb * tc, x_ref.shape[-1]), g_ref[...]).astype(BF16)
    units = []
    for bb in range(nbb):
        rows = slice(bb * tc, (bb + 1) * tc)
        qkv = jnp.dot(a[rows], wm_ref[:, 2 * d_ssm:2 * d_ssm + 3 * d_dn], preferred_element_type=F32)
        ba_ref[bb] = jnp.dot(a[rows], wba_ref[...], preferred_element_type=F32).T[:nrow]
        for s in range(nslab):
            lanes = slice(LANES * s, LANES * (s + 1))
            xpad_ref[bb, s, 0:8, :] = tail_ref[bb, s]
            xpad_ref[bb, s, 8:, :] = qkv[:, lanes]
            tail_ref[bb, s] = xpad_ref[bb, s, tc:, :]
            for r0 in range(0, tc, GDN_PREP_ROWS):
                acc = cw_ref[DN_CONV - 1:DN_CONV, lanes] * xpad_ref[bb, s, 8 + r0:8 + r0 + GDN_PREP_ROWS, :]
                for j in range(DN_CONV - 1):
                    acc = acc + cw_ref[j:j + 1, lanes] * xpad_ref[bb, s, pl.ds(cs_ref[j] + r0, GDN_PREP_ROWS), :]
                t = jax.nn.silu(acc)
                if LANES * s < 2 * d_dn:
                    t = t * lax.rsqrt(jnp.sum(t * t, axis=-1, keepdims=True) + NORM_EPS)
                    if LANES * s < d_dn:
                        t = t * (hd ** -0.5)
                qkv_ref[bb, r0:r0 + GDN_PREP_ROWS, lanes] = t

        for h, d in enumerate(_gdn_prep(bb, qkv_ref, ba_ref, alog_ref, dtb_ref, mask_ref, sum_ref, tc, heads)):
            units.append(dict(d, bb=bb, h=h))

    us_ref[...] = jnp.dot(a, wm_ref[:, :d_ssm], preferred_element_type=F32).reshape(us_ref.shape)
    zs_ref[...] = jnp.dot(a, wm_ref[:, d_ssm:2 * d_ssm], preferred_element_type=F32).reshape(zs_ref.shape)
    zd_ref[...] = jnp.dot(a, wm_ref[:, 2 * d_ssm + 3 * d_dn:], preferred_element_type=F32).reshape(zd_ref.shape)
    _gdn_body(units, zd_ref, ng_ref, y_ref, state_ref, lvl_ref, tc=tc, heads=heads)


def _proj_gdn(x, g, w_main, w_ba, conv_w, a_log, dt_bias, norm_g, d_ssm, d_dn):
    bsz, seq, dm = x.shape
    heads = d_dn // DN_HEAD_DIM
    tc = min(GDN_TILE, seq)
    nbb = GDN_BATCH_BLOCK
    nrow = 2 * heads
    nslab = 3 * d_dn // LANES
    assert DN_CHUNK == DN_HEAD_DIM and tc % DN_CHUNK == 0 and bsz % nbb == 0
    n_lvl = DN_CHUNK.bit_length() - 2
    col = lambda t: jnp.concatenate([jnp.zeros((heads,), F32), t.astype(F32)]).reshape(nrow, 1)
    full = lambda shape: pl.BlockSpec(shape, lambda b, l: (0,) * len(shape))
    tok = lambda c: pl.BlockSpec((nbb, tc, c), lambda b, l: (b, l, 0))
    conv_starts = 8 - (DN_CONV - 1) + jnp.arange(DN_CONV - 1, dtype=jnp.int32)
    return pl.pallas_call(
        functools.partial(_proj_gdn_kernel, d_ssm=d_ssm, d_dn=d_dn, tc=tc, heads=heads, nbb=nbb),
        grid=(bsz // nbb, seq // tc),
        in_specs=[tok(dm), full((1, dm)), full(w_main.shape), full(w_ba.shape), full(conv_w.shape),
                  pl.BlockSpec(memory_space=pltpu.SMEM),
                  full((nrow, 1)), full((nrow, 1)), full((1, DN_HEAD_DIM))],
        out_specs=[tok(d_ssm), tok(d_ssm), tok(d_dn)],
        out_shape=(jax.ShapeDtypeStruct((bsz, seq, d_ssm), F32),
                   jax.ShapeDtypeStruct((bsz, seq, d_ssm), F32),
                   jax.ShapeDtypeStruct((bsz, seq, d_dn), F32)),
        scratch_shapes=[pltpu.VMEM((nbb, nslab, 8, LANES), F32),
                        pltpu.VMEM((nbb, nslab, 8 + tc, LANES), F32),
                        pltpu.VMEM((nbb, tc, 3 * d_dn), F32),
                        pltpu.VMEM((nbb, tc, d_dn), F32),
                        pltpu.VMEM((nbb, nrow, tc), F32),
                        pltpu.VMEM((nbb * heads, DN_HEAD_DIM, DN_HEAD_DIM), F32),
                        pltpu.VMEM((_M_PAIR + 1, tc, tc), F32),
                        pltpu.VMEM((n_lvl, tc, tc), BF16),
                        pltpu.VMEM((tc, 2 * tc), BF16)],
        compiler_params=pltpu.CompilerParams(dimension_semantics=("arbitrary", "arbitrary"),
                                             vmem_limit_bytes=VMEM_LIMIT),
        name="proj_gdn",
    )(x, g, w_main, w_ba, conv_w, conv_starts, col(a_log), col(dt_bias),
      norm_g.astype(F32).reshape(1, DN_HEAD_DIM))


def _out_kernel(ys_ref, yd_ref, x_ref, p_ref, wos_ref, wod_ref, wp_ref, pg_ref, wg_ref, fg_ref, o_ref, *, final):
    tm = x_ref.shape[1]
    rows = tm // PROJ_SUBTILES
    for s in range(PROJ_SUBTILES):
        r = slice(s * rows, (s + 1) * rows)
        h = x_ref[0, r] + _dot(ys_ref[0, r], wos_ref[...]) + _dot(yd_ref[0, r], wod_ref[...])
        e = _rms(_dot(p_ref[0, r], wp_ref[...]), pg_ref[...])
        h = h + jax.nn.sigmoid(_dot(h, wg_ref[...])) * e
        o_ref[0, r] = _rms(h, fg_ref[...]) if final else h


def _out_proj(ys, yd, x, p, w_out_s, w_out_d, w_ple, ple_g, w_gate, final_g, final):
    bsz, seq, dm = x.shape
    d_ssm = w_out_s.shape[0]
    d_dn = w_out_d.shape[0]
    tm = min(PROJ_TILE, seq)
    full = lambda shape: pl.BlockSpec(shape, lambda b, l: (0,) * len(shape))
    tok = lambda c: pl.BlockSpec((1, tm, c), lambda b, l: (b, l, 0))
    return pl.pallas_call(
        functools.partial(_out_kernel, final=final),
        grid=(bsz, seq // tm),
        in_specs=[tok(d_ssm), tok(d_dn), tok(dm), tok(p.shape[-1]),
                  full(w_out_s.shape), full(w_out_d.shape), full(w_ple.shape), full((1, dm)),
                  full(w_gate.shape), full((1, dm))],
        out_specs=tok(dm),
        out_shape=jax.ShapeDtypeStruct((bsz, seq, dm), F32),
        compiler_params=pltpu.CompilerParams(dimension_semantics=("arbitrary", "arbitrary"),
                                             vmem_limit_bytes=VMEM_LIMIT),
        name="out_proj",
    )(ys, yd, x, p, w_out_s, w_out_d, w_ple, ple_g, w_gate, final_g)


def kernel(x, p, norm_mix_g, w_in, ssm_A_re, ssm_A_im, ssm_B_re, ssm_B_im, ssm_C_re, ssm_C_im, ssm_D, ssm_log_dt, ssm_w_glu, ssm_b_glu, dn_conv_w, dn_A_log, dn_dt_bias, dn_norm_g, w_out, w_ple_proj, ple_norm_g, w_ple_gate, final_norm_g):
    bsz, seq, dm = x.shape
    depth = w_in.shape[0]
    d_ssm = ssm_D.shape[-1]
    heads = dn_A_log.shape[-1]
    d_dn = heads * DN_HEAD_DIM
    n_main = 2 * d_ssm + 4 * d_dn
    row = lambda t: t.astype(F32).reshape(1, -1)
    h = x
    for i in range(depth):
        w_main = w_in[i, :, :n_main].astype(BF16)
        w_ba = jnp.pad(w_in[i, :, n_main:], ((0, 0), (0, LANES - 2 * heads))).astype(BF16)
        us, zs, yd = _proj_gdn(h, row(norm_mix_g[i]), w_main, w_ba, dn_conv_w[i].astype(F32),
                               dn_A_log[i], dn_dt_bias[i], dn_norm_g[i], d_ssm, d_dn)

        a_r, a_i, bb_r, bb_i = _s5_prep(ssm_A_re[i], ssm_A_im[i], ssm_log_dt[i], ssm_B_re[i], ssm_B_im[i])
        bw, cw = _s5_layout(bb_r, bb_i, ssm_C_re[i], ssm_C_im[i])
        ys = _s5(us, zs, bw, cw, a_r, a_i, row(ssm_D[i]), ssm_w_glu[i].astype(BF16), row(ssm_b_glu[i]))

        h = _out_proj(ys, yd, h, p[i],
                      w_out[i, :d_ssm].astype(BF16), w_out[i, d_ssm:].astype(BF16),
                      w_ple_proj[i].astype(BF16), row(ple_norm_g[i]), w_ple_gate[i].astype(BF16),
                      row(final_norm_g), final=(i == depth - 1))
    return h
```

```python
import functools

import jax
import jax.numpy as jnp
from jax import lax
from jax.experimental import pallas as pl
from jax.experimental.pallas import tpu as pltpu

F32 = jnp.float32
BF16 = jnp.bfloat16

NORM_EPS = 1e-6
SSM_GROUP = 16
SSM_STATE = 64
DN_HEAD_DIM = 128
DN_CONV = 4
DN_CHUNK = 128

LANES = 128
SSM_LANE_BLOCK = 128
SSM_STATE_BLOCK = (SSM_LANE_BLOCK // SSM_GROUP) * SSM_STATE

PROJ_TILE = 1024
PROJ_SUBTILES = 2
S5_TIME_TILE = 64
S5_POST_SPLIT = 4
GDN_TILE = 128
GDN_BATCH_BLOCK = 4
GDN_PROJ_GROUP = 2
GDN_PREP_ROWS = 64
VMEM_LIMIT = 48 * 1024 * 1024


def _dot(a, b):
    return jnp.dot(a.astype(BF16), b.astype(BF16), preferred_element_type=F32)


def _dot_nt(a, b):
    return lax.dot_general(a.astype(BF16), b.astype(BF16), (((1,), (1,)), ((), ())),
                           preferred_element_type=F32)


def _rms(x, g):
    return x * lax.rsqrt(jnp.mean(x * x, axis=-1, keepdims=True) + NORM_EPS) * g


def _s5_prep_kernel(lr_ref, li_ref, ldt_ref, br_ref, bi_ref, abr_ref, abi_ref, bbr_ref, bbi_ref):
    lr = lr_ref[...]
    li = li_ref[...]
    dt = jnp.exp(ldt_ref[...])
    mag = jnp.exp(lr * dt)
    ang = li * dt
    ab_r = mag * jnp.cos(ang)
    ab_i = mag * jnp.sin(ang)
    den = lr * lr + li * li
    nr = ab_r - 1.0
    ni = ab_i
    cr = (nr * lr + ni * li) / den
    ci = (ni * lr - nr * li) / den
    abr_ref[...] = ab_r
    abi_ref[...] = ab_i
    br = br_ref[...]
    bi = bi_ref[...]
    bbr_ref[...] = cr * br - ci * bi
    bbi_ref[...] = cr * bi + ci * br


def _s5_prep(a_re, a_im, log_dt, b_re, b_im):
    g, p = a_re.shape
    h = b_re.shape[-1]
    n = g * p
    row = lambda t: t.reshape(1, n).astype(F32)
    chan = lambda t: t.astype(F32).transpose(2, 0, 1).reshape(h, n)
    ldt = jnp.broadcast_to(log_dt.astype(F32)[:, None], (g, p))
    return pl.pallas_call(
        _s5_prep_kernel,
        out_shape=(jax.ShapeDtypeStruct((1, n), F32), jax.ShapeDtypeStruct((1, n), F32),
                   jax.ShapeDtypeStruct((h, n), F32), jax.ShapeDtypeStruct((h, n), F32)),
        name="s5_prep",
    )(row(a_re), row(a_im), row(ldt), chan(b_re), chan(b_im))


def _s5_layout(bb_r, bb_i, c_re, c_im):
    g, h, p = c_re.shape
    gpb = SSM_LANE_BLOCK // h
    nb = g // gpb
    eye = jnp.eye(gpb, dtype=F32)

    def in_w(bb):
        t = bb.reshape(h, nb, gpb, p)
        return jnp.einsum('hjkp,gk->jghkp', t, eye).reshape(nb, gpb * h, gpb * p)

    def out_w(c):
        t = c.astype(F32).reshape(nb, gpb, h, p)
        return jnp.einsum('jghp,gk->jgpkh', t, eye).reshape(nb, gpb * p, gpb * h)

    bw = jnp.concatenate([in_w(bb_r), in_w(bb_i)], axis=2).astype(BF16)
    cw = jnp.concatenate([out_w(c_re), -out_w(c_im)], axis=1).astype(BF16)
    return bw, cw


def _s5_kernel(u_hbm, z_hbm, bw_ref, cw_ref, ar_ref, ai_ref, d_ref, wglu_ref, bglu_ref, y_hbm,
               ubuf, zbuf, ybuf, sem_in, sem_out, bu_ref, s_ref, carry_ref, *, tt, bsz, nb, nsteps):
    sb = SSM_STATE_BLOCK
    rows = tt * bsz
    t = pl.program_id(0)
    slot = t % 2

    def in_copies(step, sl):
        cps = []
        for b in range(bsz):
            cps.append(pltpu.make_async_copy(u_hbm.at[b, pl.ds(step * tt, tt), :], ubuf.at[sl, :, b, :],
                                             sem_in.at[sl, 0]))
            cps.append(pltpu.make_async_copy(z_hbm.at[b, pl.ds(step * tt, tt), :], zbuf.at[sl, :, b, :],
                                             sem_in.at[sl, 1]))
        return cps

    def out_copies(step, sl):
        return [pltpu.make_async_copy(ybuf.at[sl, :, b, :], y_hbm.at[b, pl.ds(step * tt, tt), :], sem_out.at[sl])
                for b in range(bsz)]

    @pl.when(t == 0)
    def _():
        carry_ref[...] = jnp.zeros_like(carry_ref)
        for c in in_copies(0, 0):
            c.start()

    @pl.when(t >= 2)
    def _():
        for c in out_copies(t - 2, slot):
            c.wait()

    nxt = jnp.minimum(t + 1, nsteps - 1)
    for c in in_copies(nxt, 1 - slot):
        c.start()
    for c in in_copies(t, slot):
        c.wait()

    u = ubuf[slot].reshape(rows, nb * SSM_LANE_BLOCK)
    z = zbuf[slot].reshape(rows, nb * SSM_LANE_BLOCK)
    u16 = u.astype(BF16)
    for j in range(nb):
        bu_ref[:, 2 * sb * j:2 * sb * (j + 1)] = jnp.dot(
            u16[:, SSM_LANE_BLOCK * j:SSM_LANE_BLOCK * (j + 1)], bw_ref[j], preferred_element_type=F32)

    a_re = [jnp.broadcast_to(ar_ref[:, sb * j:sb * (j + 1)], (bsz, sb)) for j in range(nb)]
    a_im = [jnp.broadcast_to(ai_ref[:, sb * j:sb * (j + 1)], (bsz, sb)) for j in range(nb)]
    state = [(carry_ref[:, 2 * sb * j:2 * sb * j + sb], carry_ref[:, 2 * sb * j + sb:2 * sb * (j + 1)])
             for j in range(nb)]
    th = tt // S5_POST_SPLIT
    for part in range(S5_POST_SPLIT):
        for j in range(nb):
            cr = 2 * sb * j
            ci = cr + sb
            s_r, s_i = state[j]
            for i in range(part * th, (part + 1) * th):
                r = slice(i * bsz, (i + 1) * bsz)
                s_r, s_i = (a_re[j] * s_r - a_im[j] * s_i + bu_ref[r, cr:cr + sb],
                            a_re[j] * s_i + a_im[j] * s_r + bu_ref[r, ci:ci + sb])
                s_ref[r, cr:cr + sb] = s_r.astype(BF16)
                s_ref[r, ci:ci + sb] = s_i.astype(BF16)
            state[j] = (s_r, s_i)
        r = slice(part * th * bsz, (part + 1) * th * bsz)
        y = jnp.concatenate(
            [jnp.dot(s_ref[r, 2 * sb * j:2 * sb * (j + 1)], cw_ref[j], preferred_element_type=F32)
             for j in range(nb)], axis=1)
        y = y + d_ref[...] * u[r]
        y = jax.nn.gelu(y)
        y = y * jax.nn.sigmoid(_dot(y, wglu_ref[...]) + bglu_ref[...])
        y = y * jax.nn.silu(z[r])
        ybuf[slot, part * th:(part + 1) * th] = y.reshape(th, bsz, y.shape[-1])
    for j in range(nb):
        carry_ref[:, 2 * sb * j:2 * sb * j + sb] = state[j][0]
        carry_ref[:, 2 * sb * j + sb:2 * sb * (j + 1)] = state[j][1]

    for c in out_copies(t, slot):
        c.start()

    @pl.when(t == nsteps - 1)
    def _():
        for c in in_copies(nxt, 1 - slot):
            c.wait()
        for c in out_copies(t, slot):
            c.wait()
        if nsteps >= 2:
            for c in out_copies(t - 1, 1 - slot):
                c.wait()


def _s5(us, zs, bw, cw, a_r, a_i, d_vec, w_glu, b_glu):
    bsz, seq, d_ssm = us.shape
    nb = bw.shape[0]
    tt = min(S5_TIME_TILE, seq)
    rows = tt * bsz
    nsteps = seq // tt
    n_state = 2 * SSM_STATE_BLOCK * nb
    full = lambda shape: pl.BlockSpec(shape, lambda t: (0,) * len(shape))
    hbm = pl.BlockSpec(memory_space=pl.ANY)
    return pl.pallas_call(
        functools.partial(_s5_kernel, tt=tt, bsz=bsz, nb=nb, nsteps=nsteps),
        grid=(nsteps,),
        in_specs=[hbm, hbm, full(bw.shape), full(cw.shape), full(a_r.shape), full(a_i.shape),
                  full(d_vec.shape), full(w_glu.shape), full(b_glu.shape)],
        out_specs=hbm,
        out_shape=jax.ShapeDtypeStruct((bsz, seq, d_ssm), F32),
        scratch_shapes=[pltpu.VMEM((2, tt, bsz, d_ssm), F32),
                        pltpu.VMEM((2, tt, bsz, d_ssm), F32),
                        pltpu.VMEM((2, tt, bsz, d_ssm), F32),
                        pltpu.SemaphoreType.DMA((2, 2)),
                        pltpu.SemaphoreType.DMA((2,)),
                        pltpu.VMEM((rows, n_state), F32),
                        pltpu.VMEM((rows, n_state), BF16),
                        pltpu.VMEM((bsz, n_state), F32)],
        compiler_params=pltpu.CompilerParams(dimension_semantics=("arbitrary",),
                                             vmem_limit_bytes=VMEM_LIMIT),
        name="s5",
    )(us, zs, bw, cw, a_r, a_i, d_vec, w_glu, b_glu)


def _chunk_sums(g, sum_ref):
    n = g.shape[0]
    hi = g.astype(BF16).astype(F32)
    mid = (g - hi).astype(BF16).astype(F32)
    lo = g - hi - mid
    parts = jnp.concatenate([hi, mid, lo], axis=0).astype(BF16)
    s = jnp.dot(parts, sum_ref[...], preferred_element_type=F32)
    s = s[:n] + s[n:2 * n] + s[2 * n:]
    tc = g.shape[1]
    return s[:, :tc], s[:, tc:]


_M_CAUSAL, _M_STRICT, _M_EYE, _M_PAIR = 0, 1, 2, 3


def _gdn_fill_masks(mask_ref, lvl_ref, sum_ref, tc):
    ri = lax.broadcasted_iota(jnp.int32, (tc, tc), 0)
    ci = lax.broadcasted_iota(jnp.int32, (tc, tc), 1)
    same = (ri // DN_CHUNK) == (ci // DN_CHUNK)
    mask_ref[_M_CAUSAL] = (same & (ri >= ci)).astype(F32)
    mask_ref[_M_STRICT] = (same & (ri > ci)).astype(F32)
    mask_ref[_M_EYE] = (ri == ci).astype(F32)
    sum_ref[:, :tc] = (same & (ri <= ci)).astype(F32).astype(BF16)
    sum_ref[:, tc:] = (same & (ri > ci)).astype(F32).astype(BF16)
    s = 1
    lvl = 0
    while s < DN_CHUNK:
        m = (((ri // (2 * s)) == (ci // (2 * s))) & (((ri // s) % 2) == 1) & (((ci // s) % 2) == 0)).astype(F32)
        if lvl == 0:
            mask_ref[_M_PAIR] = m
        else:
            lvl_ref[lvl - 1] = m.astype(BF16)
        s *= 2
        lvl += 1


def _gdn_prep(bb, qkv_ref, ba_ref, alog_ref, dtb_ref, mask_ref, sum_ref, tc, heads):
    hd = DN_HEAD_DIM
    d_dn = heads * hd
    ba = ba_ref[bb]
    beta_r = jax.nn.sigmoid(ba)
    g_r = -jnp.exp(alog_ref[...]) * jax.nn.softplus(ba + dtb_ref[...])
    gc_r, gs_r = _chunk_sums(g_r, sum_ref)
    egs_r = jnp.exp(gs_r)
    egl_r = jnp.exp(gc_r + gs_r)
    pad = jnp.zeros((LANES - 3 * 2 * heads, tc), F32)
    cols = jnp.concatenate([beta_r, gc_r, egl_r, pad], axis=0).T
    out = []
    for h in range(heads):
        q = qkv_ref[bb, :, hd * h:hd * (h + 1)]
        k = qkv_ref[bb, :, d_dn + hd * h:d_dn + hd * (h + 1)]
        v = qkv_ref[bb, :, 2 * d_dn + hd * h:2 * d_dn + hd * (h + 1)]
        beta_c = cols[:, h:h + 1]
        gc_c = cols[:, 3 * heads + h:3 * heads + h + 1]
        gc_row = gc_r[heads + h:heads + h + 1, :]
        eg_c = jnp.exp(gc_c)
        kb = k * beta_c
        k16 = k.astype(BF16)
        attn16, a16, t16 = [], [], []
        for r0 in range(0, tc, GDN_PREP_ROWS):
            r = slice(r0, r0 + GDN_PREP_ROWS)
            decay = jnp.exp(jnp.minimum(gc_c[r] - gc_row, 0.0))
            qk_kk = _dot_nt(jnp.concatenate([q[r], k[r]], axis=0), k16)
            a = qk_kk[GDN_PREP_ROWS:] * (beta_c[r] * decay) * mask_ref[_M_STRICT, r]
            attn16.append((qk_kk[:GDN_PREP_ROWS] * decay * mask_ref[_M_CAUSAL, r]).astype(BF16))
            a16.append(a.astype(BF16))
            t16.append((mask_ref[_M_EYE, r] - a * mask_ref[_M_PAIR, r]).astype(BF16))
        out.append(dict(
            egl_c=cols[:, 5 * heads + h:5 * heads + h + 1],
            qg16=(q * eg_c).astype(BF16),
            rhs16=jnp.concatenate([v * beta_c, kb * eg_c], axis=1).astype(BF16),
            kdec16=(k.T * egs_r[heads + h:heads + h + 1, :]).astype(BF16),
            attn16=jnp.concatenate(attn16, axis=0),
            a16=jnp.concatenate(a16, axis=0),
            t16=jnp.concatenate(t16, axis=0)))
    return out


def _gdn_body(units, zd_ref, ng_ref, y_ref, state_ref, lvl_ref, *, tc, heads):
    hd = DN_HEAD_DIM
    nchunk = tc // DN_CHUNK

    for lvl in range(lvl_ref.shape[0]):
        for u in units:
            u["x16"] = jnp.dot(u["a16"], u["t16"], preferred_element_type=F32).astype(BF16)
        for u in units:
            y16 = jnp.dot(u["t16"], u["x16"], preferred_element_type=F32).astype(BF16)
            u["t16"] = u["t16"] - y16 * lvl_ref[lvl]
    for u in units:
        sol = jnp.dot(u["t16"], u["rhs16"], preferred_element_type=F32)
        u["u"] = sol[:, :hd]
        u["w16"] = sol[:, hd:].astype(BF16)
        u["st"] = state_ref[u["bb"] * heads + u["h"]]
        u["outs"] = []

    for n in range(nchunk):
        r0 = n * DN_CHUNK
        r1 = r0 + DN_CHUNK
        for u in units:
            u["ws_qs"] = jnp.dot(jnp.concatenate([u["w16"][r0:r1], u["qg16"][r0:r1]], axis=0),
                                 u["st"].astype(BF16), preferred_element_type=F32)
        for u in units:
            v_new = (u["u"][r0:r1] - u["ws_qs"][:DN_CHUNK]).astype(BF16)
            u["av_kv"] = jnp.dot(jnp.concatenate([u["attn16"][r0:r1, r0:r1], u["kdec16"][:, r0:r1]], axis=0),
                                 v_new, preferred_element_type=F32)
        for u in units:
            u["outs"].append(u["ws_qs"][DN_CHUNK:] + u["av_kv"][:DN_CHUNK])
            u["st"] = u["st"] * u["egl_c"][r0:r1] + u["av_kv"][DN_CHUNK:]
    for u in units:
        bb, h = u["bb"], u["h"]
        state_ref[bb * heads + h] = u["st"]
        o = jnp.concatenate(u["outs"], axis=0)
        z = zd_ref[bb, :, hd * h:hd * (h + 1)]
        y_ref[bb, :, hd * h:hd * (h + 1)] = _rms(o, ng_ref[...]) * jax.nn.silu(z)


def _proj_gdn_kernel(x_ref, g_ref, wm_ref, wba_ref, cw_ref, cs_ref, alog_ref, dtb_ref, ng_ref,
                     us_ref, zs_ref, y_ref,
                     tail_ref, xpad_ref, qkv_ref, zd_ref, ba_ref, state_ref, mask_ref, lvl_ref, sum_ref,
                     *, d_ssm, d_dn, tc, heads, nbb):
    hd = DN_HEAD_DIM
    nslab = 3 * d_dn // LANES
    nrow = ba_ref.shape[1]

    @pl.when(pl.program_id(1) == 0)
    def _():
        tail_ref[...] = jnp.zeros_like(tail_ref)
        state_ref[...] = jnp.zeros_like(state_ref)
        _gdn_fill_masks(mask_ref, lvl_ref, sum_ref, tc)

    a = _rms(x_ref[...].reshape(nbb * tc, x_ref.shape[-1]), g_ref[...]).astype(BF16)
    others = [(us_ref, slice(0, d_ssm)), (zs_ref, slice(d_ssm, 2 * d_ssm)),
              (zd_ref, slice(2 * d_ssm + 3 * d_dn, 2 * d_ssm + 4 * d_dn))]

    def other_projection():
        ref, cols = others.pop(0)
        ref[...] = jnp.dot(a, wm_ref[:, cols], preferred_element_type=F32).reshape(ref.shape)

    units = []
    for bb in range(nbb):
        if bb % GDN_PROJ_GROUP == 0:
            grows = slice(bb * tc, (bb + GDN_PROJ_GROUP) * tc)
            qkv_g = jnp.dot(a[grows], wm_ref[:, 2 * d_ssm:2 * d_ssm + 3 * d_dn], preferred_element_type=F32)
            ba_g = jnp.dot(a[grows], wba_ref[...], preferred_element_type=F32)
            if others:
                other_projection()
        rows = slice((bb % GDN_PROJ_GROUP) * tc, (bb % GDN_PROJ_GROUP + 1) * tc)
        qkv = qkv_g[rows]
        ba_ref[bb] = ba_g[rows].T[:nrow]
        for s in range(nslab):
            lanes = slice(LANES * s, LANES * (s + 1))
            xpad_ref[bb, s, 0:8, :] = tail_ref[bb, s]
            xpad_ref[bb, s, 8:, :] = qkv[:, lanes]
            tail_ref[bb, s] = xpad_ref[bb, s, tc:, :]
            for r0 in range(0, tc, GDN_PREP_ROWS):
                acc = cw_ref[DN_CONV - 1:DN_CONV, lanes] * xpad_ref[bb, s, 8 + r0:8 + r0 + GDN_PREP_ROWS, :]
                for j in range(DN_CONV - 1):
                    acc = acc + cw_ref[j:j + 1, lanes] * xpad_ref[bb, s, pl.ds(cs_ref[j] + r0, GDN_PREP_ROWS), :]
                t = jax.nn.silu(acc)
                if LANES * s < 2 * d_dn:
                    t = t * lax.rsqrt(jnp.sum(t * t, axis=-1, keepdims=True) + NORM_EPS)
                    if LANES * s < d_dn:
                        t = t * (hd ** -0.5)
                qkv_ref[bb, r0:r0 + GDN_PREP_ROWS, lanes] = t

        for h, d in enumerate(_gdn_prep(bb, qkv_ref, ba_ref, alog_ref, dtb_ref, mask_ref, sum_ref, tc, heads)):
            units.append(dict(d, bb=bb, h=h))

    while others:
        other_projection()
    _gdn_body(units, zd_ref, ng_ref, y_ref, state_ref, lvl_ref, tc=tc, heads=heads)


def _proj_gdn(x, g, w_main, w_ba, conv_w, a_log, dt_bias, norm_g, d_ssm, d_dn):
    bsz, seq, dm = x.shape
    heads = d_dn // DN_HEAD_DIM
    tc = min(GDN_TILE, seq)
    nbb = GDN_BATCH_BLOCK
    nrow = 2 * heads
    nslab = 3 * d_dn // LANES
    assert DN_CHUNK == DN_HEAD_DIM and tc % DN_CHUNK == 0 and bsz % nbb == 0
    n_lvl = DN_CHUNK.bit_length() - 2
    col = lambda t: jnp.concatenate([jnp.zeros((heads,), F32), t.astype(F32)]).reshape(nrow, 1)
    full = lambda shape: pl.BlockSpec(shape, lambda b, l: (0,) * len(shape))
    tok = lambda c: pl.BlockSpec((nbb, tc, c), lambda b, l: (b, l, 0))
    conv_starts = 8 - (DN_CONV - 1) + jnp.arange(DN_CONV - 1, dtype=jnp.int32)
    return pl.pallas_call(
        functools.partial(_proj_gdn_kernel, d_ssm=d_ssm, d_dn=d_dn, tc=tc, heads=heads, nbb=nbb),
        grid=(bsz // nbb, seq // tc),
        in_specs=[tok(dm), full((1, dm)), full(w_main.shape), full(w_ba.shape), full(conv_w.shape),
                  pl.BlockSpec(memory_space=pltpu.SMEM),
                  full((nrow, 1)), full((nrow, 1)), full((1, DN_HEAD_DIM))],
        out_specs=[tok(d_ssm), tok(d_ssm), tok(d_dn)],
        out_shape=(jax.ShapeDtypeStruct((bsz, seq, d_ssm), F32),
                   jax.ShapeDtypeStruct((bsz, seq, d_ssm), F32),
                   jax.ShapeDtypeStruct((bsz, seq, d_dn), F32)),
        scratch_shapes=[pltpu.VMEM((nbb, nslab, 8, LANES), F32),
                        pltpu.VMEM((nbb, nslab, 8 + tc, LANES), F32),
                        pltpu.VMEM((nbb, tc, 3 * d_dn), F32),
                        pltpu.VMEM((nbb, tc, d_dn), F32),
                        pltpu.VMEM((nbb, nrow, tc), F32),
                        pltpu.VMEM((nbb * heads, DN_HEAD_DIM, DN_HEAD_DIM), F32),
                        pltpu.VMEM((_M_PAIR + 1, tc, tc), F32),
                        pltpu.VMEM((n_lvl, tc, tc), BF16),
                        pltpu.VMEM((tc, 2 * tc), BF16)],
        compiler_params=pltpu.CompilerParams(dimension_semantics=("arbitrary", "arbitrary"),
                                             vmem_limit_bytes=VMEM_LIMIT),
        name="proj_gdn",
    )(x, g, w_main, w_ba, conv_w, conv_starts, col(a_log), col(dt_bias),
      norm_g.astype(F32).reshape(1, DN_HEAD_DIM))


def _out_kernel(ys_ref, yd_ref, x_ref, p_ref, wos_ref, wod_ref, wp_ref, pg_ref, wg_ref, fg_ref, o_ref, *, final):
    tm = x_ref.shape[1]
    rows = tm // PROJ_SUBTILES
    for s in range(PROJ_SUBTILES):
        r = slice(s * rows, (s + 1) * rows)
        h = x_ref[0, r] + _dot(ys_ref[0, r], wos_ref[...]) + _dot(yd_ref[0, r], wod_ref[...])
        e = _rms(_dot(p_ref[0, r], wp_ref[...]), pg_ref[...])
        h = h + jax.nn.sigmoid(_dot(h, wg_ref[...])) * e
        o_ref[0, r] = _rms(h, fg_ref[...]) if final else h


def _out_proj(ys, yd, x, p, w_out_s, w_out_d, w_ple, ple_g, w_gate, final_g, final):
    bsz, seq, dm = x.shape
    d_ssm = w_out_s.shape[0]
    d_dn = w_out_d.shape[0]
    tm = min(PROJ_TILE, seq)
    full = lambda shape: pl.BlockSpec(shape, lambda b, l: (0,) * len(shape))
    tok = lambda c: pl.BlockSpec((1, tm, c), lambda b, l: (b, l, 0))
    return pl.pallas_call(
        functools.partial(_out_kernel, final=final),
        grid=(bsz, seq // tm),
        in_specs=[tok(d_ssm), tok(d_dn), tok(dm), tok(p.shape[-1]),
                  full(w_out_s.shape), full(w_out_d.shape), full(w_ple.shape), full((1, dm)),
                  full(w_gate.shape), full((1, dm))],
        out_specs=tok(dm),
        out_shape=jax.ShapeDtypeStruct((bsz, seq, dm), F32),
        compiler_params=pltpu.CompilerParams(dimension_semantics=("arbitrary", "arbitrary"),
                                             vmem_limit_bytes=VMEM_LIMIT),
        name="out_proj",
    )(ys, yd, x, p, w_out_s, w_out_d, w_ple, ple_g, w_gate, final_g)


def kernel(x, p, norm_mix_g, w_in, ssm_A_re, ssm_A_im, ssm_B_re, ssm_B_im, ssm_C_re, ssm_C_im, ssm_D, ssm_log_dt, ssm_w_glu, ssm_b_glu, dn_conv_w, dn_A_log, dn_dt_bias, dn_norm_g, w_out, w_ple_proj, ple_norm_g, w_ple_gate, final_norm_g):
    bsz, seq, dm = x.shape
    depth = w_in.shape[0]
    d_ssm = ssm_D.shape[-1]
    heads = dn_A_log.shape[-1]
    d_dn = heads * DN_HEAD_DIM
    n_main = 2 * d_ssm + 4 * d_dn
    row = lambda t: t.astype(F32).reshape(1, -1)
    h = x
    for i in range(depth):
        w_main = w_in[i, :, :n_main].astype(BF16)
        w_ba = jnp.pad(w_in[i, :, n_main:], ((0, 0), (0, LANES - 2 * heads))).astype(BF16)
        us, zs, yd = _proj_gdn(h, row(norm_mix_g[i]), w_main, w_ba, dn_conv_w[i].astype(F32),
                               dn_A_log[i], dn_dt_bias[i], dn_norm_g[i], d_ssm, d_dn)

        a_r, a_i, bb_r, bb_i = _s5_prep(ssm_A_re[i], ssm_A_im[i], ssm_log_dt[i], ssm_B_re[i], ssm_B_im[i])
        bw, cw = _s5_layout(bb_r, bb_i, ssm_C_re[i], ssm_C_im[i])
        ys = _s5(us, zs, bw, cw, a_r, a_i, row(ssm_D[i]), ssm_w_glu[i].astype(BF16), row(ssm_b_glu[i]))

        h = _out_proj(ys, yd, h, p[i],
                      w_out[i, :d_ssm].astype(BF16), w_out[i, d_ssm:].astype(BF16),
                      w_ple_proj[i].astype(BF16), row(ple_norm_g[i]), w_ple_gate[i].astype(BF16),
                      row(final_norm_g), final=(i == depth - 1))
    return h
```

```python
import functools

import jax
import jax.numpy as jnp
from jax import lax
from jax.experimental import pallas as pl
from jax.experimental.pallas import tpu as pltpu

F32 = jnp.float32
BF16 = jnp.bfloat16

NORM_EPS = 1e-6
SSM_GROUP = 16
SSM_STATE = 64
DN_HEAD_DIM = 128
DN_CONV = 4
DN_CHUNK = 128

LANES = 128
SSM_LANE_BLOCK = 128
SSM_STATE_BLOCK = (SSM_LANE_BLOCK // SSM_GROUP) * SSM_STATE

PROJ_TILE = 1024
PROJ_SUBTILES = 2
S5_TIME_TILE = 64
S5_POST_SPLIT = 4
GDN_TILE = 128
GDN_BATCH_BLOCK = 4
GDN_PROJ_GROUP = 2
GDN_PREP_ROWS = 64
VMEM_LIMIT = 48 * 1024 * 1024


def _dot(a, b):
    return jnp.dot(a.astype(BF16), b.astype(BF16), preferred_element_type=F32)


def _dot_nt(a, b):
    return lax.dot_general(a.astype(BF16), b.astype(BF16), (((1,), (1,)), ((), ())),
                           preferred_element_type=F32)


def _rms(x, g):
    return x * lax.rsqrt(jnp.mean(x * x, axis=-1, keepdims=True) + NORM_EPS) * g


def _s5_prep_kernel(lr_ref, li_ref, ldt_ref, br_ref, bi_ref, abr_ref, abi_ref, bbr_ref, bbi_ref):
    lr = lr_ref[...]
    li = li_ref[...]
    dt = jnp.exp(ldt_ref[...])
    mag = jnp.exp(lr * dt)
    ang = li * dt
    ab_r = mag * jnp.cos(ang)
    ab_i = mag * jnp.sin(ang)
    den = lr * lr + li * li
    nr = ab_r - 1.0
    ni = ab_i
    cr = (nr * lr + ni * li) / den
    ci = (ni * lr - nr * li) / den
    abr_ref[...] = ab_r
    abi_ref[...] = ab_i
    br = br_ref[...]
    bi = bi_ref[...]
    bbr_ref[...] = cr * br - ci * bi
    bbi_ref[...] = cr * bi + ci * br


def _s5_prep(a_re, a_im, log_dt, b_re, b_im):
    g, p = a_re.shape
    h = b_re.shape[-1]
    n = g * p
    row = lambda t: t.reshape(1, n).astype(F32)
    chan = lambda t: t.astype(F32).transpose(2, 0, 1).reshape(h, n)
    ldt = jnp.broadcast_to(log_dt.astype(F32)[:, None], (g, p))
    return pl.pallas_call(
        _s5_prep_kernel,
        out_shape=(jax.ShapeDtypeStruct((1, n), F32), jax.ShapeDtypeStruct((1, n), F32),
                   jax.ShapeDtypeStruct((h, n), F32), jax.ShapeDtypeStruct((h, n), F32)),
        name="s5_prep",
    )(row(a_re), row(a_im), row(ldt), chan(b_re), chan(b_im))


def _s5_layout(bb_r, bb_i, c_re, c_im):
    g, h, p = c_re.shape
    gpb = SSM_LANE_BLOCK // h
    nb = g // gpb
    eye = jnp.eye(gpb, dtype=F32)

    def in_w(bb):
        t = bb.reshape(h, nb, gpb, p)
        return jnp.einsum('hjkp,gk->jghkp', t, eye).reshape(nb, gpb * h, gpb * p)

    def out_w(c):
        t = c.astype(F32).reshape(nb, gpb, h, p)
        return jnp.einsum('jghp,gk->jgpkh', t, eye).reshape(nb, gpb * p, gpb * h)

    bw = jnp.concatenate([in_w(bb_r), in_w(bb_i)], axis=2).astype(BF16)
    cw = jnp.concatenate([out_w(c_re), -out_w(c_im)], axis=1).astype(BF16)
    return bw, cw


def _s5_kernel(u_hbm, z_hbm, bw_ref, cw_ref, ar_ref, ai_ref, d_ref, wglu_ref, bglu_ref, y_hbm,
               ubuf, zbuf, ybuf, sem_in, sem_out, carry_ref, *blk_refs, tt, bsz, nb, nsteps):
    sb = SSM_STATE_BLOCK
    rows = tt * bsz
    t = pl.program_id(0)
    slot = t % 2
    bu_refs, s_refs = blk_refs[:nb], blk_refs[nb:]

    def in_copies(step, sl):
        cps = []
        for b in range(bsz):
            cps.append(pltpu.make_async_copy(u_hbm.at[b, pl.ds(step * tt, tt), :], ubuf.at[sl, :, b, :],
                                             sem_in.at[sl, 0]))
            cps.append(pltpu.make_async_copy(z_hbm.at[b, pl.ds(step * tt, tt), :], zbuf.at[sl, :, b, :],
                                             sem_in.at[sl, 1]))
        return cps

    def out_copies(step, sl):
        return [pltpu.make_async_copy(ybuf.at[sl, :, b, :], y_hbm.at[b, pl.ds(step * tt, tt), :], sem_out.at[sl])
                for b in range(bsz)]

    @pl.when(t == 0)
    def _():
        carry_ref[...] = jnp.zeros_like(carry_ref)
        for c in in_copies(0, 0):
            c.start()

    @pl.when(t >= 2)
    def _():
        for c in out_copies(t - 2, slot):
            c.wait()

    nxt = jnp.minimum(t + 1, nsteps - 1)
    for c in in_copies(nxt, 1 - slot):
        c.start()
    for c in in_copies(t, slot):
        c.wait()

    u = ubuf[slot].reshape(rows, nb * SSM_LANE_BLOCK)
    z = zbuf[slot].reshape(rows, nb * SSM_LANE_BLOCK)
    u16 = u.astype(BF16)
    th = tt // S5_POST_SPLIT

    def input_matmuls(part):
        r = slice(part * th * bsz, (part + 1) * th * bsz)
        for j in range(nb):
            bu_refs[j][r] = jnp.dot(u16[r, SSM_LANE_BLOCK * j:SSM_LANE_BLOCK * (j + 1)], bw_ref[j],
                                    preferred_element_type=F32)

    input_matmuls(0)
    a_re = [jnp.broadcast_to(ar_ref[:, sb * j:sb * (j + 1)], (bsz, sb)) for j in range(nb)]
    a_im = [jnp.broadcast_to(ai_ref[:, sb * j:sb * (j + 1)], (bsz, sb)) for j in range(nb)]
    state = [(carry_ref[:, 2 * sb * j:2 * sb * j + sb], carry_ref[:, 2 * sb * j + sb:2 * sb * (j + 1)])
             for j in range(nb)]
    for part in range(S5_POST_SPLIT):
        if part + 1 < S5_POST_SPLIT:
            input_matmuls(part + 1)
        for j in range(nb):
            s_r, s_i = state[j]
            for i in range(part * th, (part + 1) * th):
                r = slice(i * bsz, (i + 1) * bsz)
                s_r, s_i = (a_re[j] * s_r - a_im[j] * s_i + bu_refs[j][r, :sb],
                            a_re[j] * s_i + a_im[j] * s_r + bu_refs[j][r, sb:])
                s_refs[j][r, :sb] = s_r.astype(BF16)
                s_refs[j][r, sb:] = s_i.astype(BF16)
            state[j] = (s_r, s_i)
        r = slice(part * th * bsz, (part + 1) * th * bsz)
        y = jnp.concatenate(
            [jnp.dot(s_refs[j][r], cw_ref[j], preferred_element_type=F32)
             for j in range(nb)], axis=1)
        y = y + d_ref[...] * u[r]
        y = jax.nn.gelu(y)
        y = y * jax.nn.sigmoid(_dot(y, wglu_ref[...]) + bglu_ref[...])
        y = y * jax.nn.silu(z[r])
        ybuf[slot, part * th:(part + 1) * th] = y.reshape(th, bsz, y.shape[-1])
    for j in range(nb):
        carry_ref[:, 2 * sb * j:2 * sb * j + sb] = state[j][0]
        carry_ref[:, 2 * sb * j + sb:2 * sb * (j + 1)] = state[j][1]

    for c in out_copies(t, slot):
        c.start()

    @pl.when(t == nsteps - 1)
    def _():
        for c in in_copies(nxt, 1 - slot):
            c.wait()
        for c in out_copies(t, slot):
            c.wait()
        if nsteps >= 2:
            for c in out_copies(t - 1, 1 - slot):
                c.wait()


def _s5(us, zs, bw, cw, a_r, a_i, d_vec, w_glu, b_glu):
    bsz, seq, d_ssm = us.shape
    nb = bw.shape[0]
    tt = min(S5_TIME_TILE, seq)
    rows = tt * bsz
    nsteps = seq // tt
    n_state = 2 * SSM_STATE_BLOCK * nb
    full = lambda shape: pl.BlockSpec(shape, lambda t: (0,) * len(shape))
    hbm = pl.BlockSpec(memory_space=pl.ANY)
    return pl.pallas_call(
        functools.partial(_s5_kernel, tt=tt, bsz=bsz, nb=nb, nsteps=nsteps),
        grid=(nsteps,),
        in_specs=[hbm, hbm, full(bw.shape), full(cw.shape), full(a_r.shape), full(a_i.shape),
                  full(d_vec.shape), full(w_glu.shape), full(b_glu.shape)],
        out_specs=hbm,
        out_shape=jax.ShapeDtypeStruct((bsz, seq, d_ssm), F32),
        scratch_shapes=[pltpu.VMEM((2, tt, bsz, d_ssm), F32),
                        pltpu.VMEM((2, tt, bsz, d_ssm), F32),
                        pltpu.VMEM((2, tt, bsz, d_ssm), F32),
                        pltpu.SemaphoreType.DMA((2, 2)),
                        pltpu.SemaphoreType.DMA((2,)),
                        pltpu.VMEM((bsz, n_state), F32)]
        + [pltpu.VMEM((rows, 2 * SSM_STATE_BLOCK), F32) for _ in range(nb)]
        + [pltpu.VMEM((rows, 2 * SSM_STATE_BLOCK), BF16) for _ in range(nb)],
        compiler_params=pltpu.CompilerParams(dimension_semantics=("arbitrary",),
                                             vmem_limit_bytes=VMEM_LIMIT),
        name="s5",
    )(us, zs, bw, cw, a_r, a_i, d_vec, w_glu, b_glu)


def _chunk_sums(g, sum_ref):
    n = g.shape[0]
    hi = g.astype(BF16).astype(F32)
    mid = (g - hi).astype(BF16).astype(F32)
    lo = g - hi - mid
    parts = jnp.concatenate([hi, mid, lo], axis=0).astype(BF16)
    s = jnp.dot(parts, sum_ref[...], preferred_element_type=F32)
    s = s[:n] + s[n:2 * n] + s[2 * n:]
    tc = g.shape[1]
    return s[:, :tc], s[:, tc:]


_M_CAUSAL, _M_STRICT, _M_EYE, _M_PAIR = 0, 1, 2, 3


def _gdn_fill_masks(mask_ref, lvl_ref, sum_ref, tc):
    ri = lax.broadcasted_iota(jnp.int32, (tc, tc), 0)
    ci = lax.broadcasted_iota(jnp.int32, (tc, tc), 1)
    same = (ri // DN_CHUNK) == (ci // DN_CHUNK)
    mask_ref[_M_CAUSAL] = (same & (ri >= ci)).astype(F32)
    mask_ref[_M_STRICT] = (same & (ri > ci)).astype(F32)
    mask_ref[_M_EYE] = (ri == ci).astype(F32)
    sum_ref[:, :tc] = (same & (ri <= ci)).astype(F32).astype(BF16)
    sum_ref[:, tc:] = (same & (ri > ci)).astype(F32).astype(BF16)
    s = 1
    lvl = 0
    while s < DN_CHUNK:
        m = (((ri // (2 * s)) == (ci // (2 * s))) & (((ri // s) % 2) == 1) & (((ci // s) % 2) == 0)).astype(F32)
        if lvl == 0:
            mask_ref[_M_PAIR] = m
        else:
            lvl_ref[lvl - 1] = m.astype(BF16)
        s *= 2
        lvl += 1


def _gdn_prep(bb, qkv_ref, ba_ref, alog_ref, dtb_ref, mask_ref, sum_ref, tc, heads):
    hd = DN_HEAD_DIM
    d_dn = heads * hd
    ba = ba_ref[bb]
    beta_r = jax.nn.sigmoid(ba)
    g_r = -jnp.exp(alog_ref[...]) * jax.nn.softplus(ba + dtb_ref[...])
    gc_r, gs_r = _chunk_sums(g_r, sum_ref)
    egs_r = jnp.exp(gs_r)
    egl_r = jnp.exp(gc_r + gs_r)
    pad = jnp.zeros((LANES - 3 * 2 * heads, tc), F32)
    cols = jnp.concatenate([beta_r, gc_r, egl_r, pad], axis=0).T
    out = []
    for h in range(heads):
        q = qkv_ref[bb, :, hd * h:hd * (h + 1)]
        k = qkv_ref[bb, :, d_dn + hd * h:d_dn + hd * (h + 1)]
        v = qkv_ref[bb, :, 2 * d_dn + hd * h:2 * d_dn + hd * (h + 1)]
        beta_c = cols[:, h:h + 1]
        gc_c = cols[:, 3 * heads + h:3 * heads + h + 1]
        gc_row = gc_r[heads + h:heads + h + 1, :]
        eg_c = jnp.exp(gc_c)
        kb = k * beta_c
        k16 = k.astype(BF16)
        attn16, a16, t16 = [], [], []
        for r0 in range(0, tc, GDN_PREP_ROWS):
            r = slice(r0, r0 + GDN_PREP_ROWS)
            decay = jnp.exp(jnp.minimum(gc_c[r] - gc_row, 0.0))
            qk_kk = _dot_nt(jnp.concatenate([q[r], k[r]], axis=0), k16)
            a = qk_kk[GDN_PREP_ROWS:] * (beta_c[r] * decay) * mask_ref[_M_STRICT, r]
            attn16.append((qk_kk[:GDN_PREP_ROWS] * decay * mask_ref[_M_CAUSAL, r]).astype(BF16))
            a16.append(a.astype(BF16))
            t16.append((mask_ref[_M_EYE, r] - a * mask_ref[_M_PAIR, r]).astype(BF16))
        out.append(dict(
            egl_c=cols[:, 5 * heads + h:5 * heads + h + 1],
            qg16=(q * eg_c).astype(BF16),
            rhs16=jnp.concatenate([v * beta_c, kb * eg_c], axis=1).astype(BF16),
            kdec16=(k.T * egs_r[heads + h:heads + h + 1, :]).astype(BF16),
            attn16=jnp.concatenate(attn16, axis=0),
            a16=jnp.concatenate(a16, axis=0),
            t16=jnp.concatenate(t16, axis=0)))
    return out


def _gdn_body(units, zd_ref, ng_ref, y_ref, state_ref, lvl_ref, *, tc, heads):
    hd = DN_HEAD_DIM
    nchunk = tc // DN_CHUNK

    for lvl in range(lvl_ref.shape[0]):
        for u in units:
            u["x16"] = jnp.dot(u["a16"], u["t16"], preferred_element_type=F32).astype(BF16)
        for u in units:
            y16 = jnp.dot(u["t16"], u["x16"], preferred_element_type=F32).astype(BF16)
            u["t16"] = u["t16"] - y16 * lvl_ref[lvl]
    for u in units:
        sol = jnp.dot(u["t16"], u["rhs16"], preferred_element_type=F32)
        u["u"] = sol[:, :hd]
        u["w16"] = sol[:, hd:].astype(BF16)
        u["st"] = state_ref[u["bb"] * heads + u["h"]]
        u["outs"] = []

    for n in range(nchunk):
        r0 = n * DN_CHUNK
        r1 = r0 + DN_CHUNK
        for u in units:
            u["ws_qs"] = jnp.dot(jnp.concatenate([u["w16"][r0:r1], u["qg16"][r0:r1]], axis=0),
                                 u["st"].astype(BF16), preferred_element_type=F32)
        for u in units:
            v_new = (u["u"][r0:r1] - u["ws_qs"][:DN_CHUNK]).astype(BF16)
            u["av_kv"] = jnp.dot(jnp.concatenate([u["attn16"][r0:r1, r0:r1], u["kdec16"][:, r0:r1]], axis=0),
                                 v_new, preferred_element_type=F32)
        for u in units:
            u["outs"].append(u["ws_qs"][DN_CHUNK:] + u["av_kv"][:DN_CHUNK])
            u["st"] = u["st"] * u["egl_c"][r0:r1] + u["av_kv"][DN_CHUNK:]
    for u in units:
        bb, h = u["bb"], u["h"]
        state_ref[bb * heads + h] = u["st"]
        o = jnp.concatenate(u["outs"], axis=0)
        z = zd_ref[bb, :, hd * h:hd * (h + 1)]
        y_ref[bb, :, hd * h:hd * (h + 1)] = _rms(o, ng_ref[...]) * jax.nn.silu(z)


def _proj_gdn_kernel(x_ref, g_ref, wm_ref, wba_ref, cw_ref, cs_ref, alog_ref, dtb_ref, ng_ref,
                     us_ref, zs_ref, y_ref,
                     tail_ref, xpad_ref, qkv_ref, zd_ref, ba_ref, state_ref, mask_ref, lvl_ref, sum_ref,
                     *, d_ssm, d_dn, tc, heads, nbb):
    hd = DN_HEAD_DIM
    nslab = 3 * d_dn // LANES
    nrow = ba_ref.shape[1]

    @pl.when(pl.program_id(1) == 0)
    def _():
        tail_ref[...] = jnp.zeros_like(tail_ref)
        state_ref[...] = jnp.zeros_like(state_ref)
        _gdn_fill_masks(mask_ref, lvl_ref, sum_ref, tc)

    a = _rms(x_ref[...].reshape(nbb * tc, x_ref.shape[-1]), g_ref[...]).astype(BF16)
    others = [(us_ref, slice(0, d_ssm)), (zs_ref, slice(d_ssm, 2 * d_ssm)),
              (zd_ref, slice(2 * d_ssm + 3 * d_dn, 2 * d_ssm + 4 * d_dn))]

    def other_projection():
        ref, cols = others.pop(0)
        ref[...] = jnp.dot(a, wm_ref[:, cols], preferred_element_type=F32).reshape(ref.shape)

    units = []
    for bb in range(nbb):
        if bb % GDN_PROJ_GROUP == 0:
            grows = slice(bb * tc, (bb + GDN_PROJ_GROUP) * tc)
            qkv_g = jnp.dot(a[grows], wm_ref[:, 2 * d_ssm:2 * d_ssm + 3 * d_dn], preferred_element_type=F32)
            ba_g = jnp.dot(a[grows], wba_ref[...], preferred_element_type=F32)
            if others:
                other_projection()
        rows = slice((bb % GDN_PROJ_GROUP) * tc, (bb % GDN_PROJ_GROUP + 1) * tc)
        qkv = qkv_g[rows]
        ba_ref[bb] = ba_g[rows].T[:nrow]
        for s in range(nslab):
            lanes = slice(LANES * s, LANES * (s + 1))
            xpad_ref[bb, s, 0:8, :] = tail_ref[bb, s]
            xpad_ref[bb, s, 8:, :] = qkv[:, lanes]
            tail_ref[bb, s] = xpad_ref[bb, s, tc:, :]
            for r0 in range(0, tc, GDN_PREP_ROWS):
                acc = cw_ref[DN_CONV - 1:DN_CONV, lanes] * xpad_ref[bb, s, 8 + r0:8 + r0 + GDN_PREP_ROWS, :]
                for j in range(DN_CONV - 1):
                    acc = acc + cw_ref[j:j + 1, lanes] * xpad_ref[bb, s, pl.ds(cs_ref[j] + r0, GDN_PREP_ROWS), :]
                t = jax.nn.silu(acc)
                if LANES * s < 2 * d_dn:
                    t = t * lax.rsqrt(jnp.sum(t * t, axis=-1, keepdims=True) + NORM_EPS)
                    if LANES * s < d_dn:
                        t = t * (hd ** -0.5)
                qkv_ref[bb, r0:r0 + GDN_PREP_ROWS, lanes] = t

        for h, d in enumerate(_gdn_prep(bb, qkv_ref, ba_ref, alog_ref, dtb_ref, mask_ref, sum_ref, tc, heads)):
            units.append(dict(d, bb=bb, h=h))

    while others:
        other_projection()
    _gdn_body(units, zd_ref, ng_ref, y_ref, state_ref, lvl_ref, tc=tc, heads=heads)


def _proj_gdn(x, g, w_main, w_ba, conv_w, a_log, dt_bias, norm_g, d_ssm, d_dn):
    bsz, seq, dm = x.shape
    heads = d_dn // DN_HEAD_DIM
    tc = min(GDN_TILE, seq)
    nbb = GDN_BATCH_BLOCK
    nrow = 2 * heads
    nslab = 3 * d_dn // LANES
    assert DN_CHUNK == DN_HEAD_DIM and tc % DN_CHUNK == 0 and bsz % nbb == 0
    n_lvl = DN_CHUNK.bit_length() - 2
    col = lambda t: jnp.concatenate([jnp.zeros((heads,), F32), t.astype(F32)]).reshape(nrow, 1)
    full = lambda shape: pl.BlockSpec(shape, lambda b, l: (0,) * len(shape))
    tok = lambda c: pl.BlockSpec((nbb, tc, c), lambda b, l: (b, l, 0))
    conv_starts = 8 - (DN_CONV - 1) + jnp.arange(DN_CONV - 1, dtype=jnp.int32)
    return pl.pallas_call(
        functools.partial(_proj_gdn_kernel, d_ssm=d_ssm, d_dn=d_dn, tc=tc, heads=heads, nbb=nbb),
        grid=(bsz // nbb, seq // tc),
        in_specs=[tok(dm), full((1, dm)), full(w_main.shape), full(w_ba.shape), full(conv_w.shape),
                  pl.BlockSpec(memory_space=pltpu.SMEM),
                  full((nrow, 1)), full((nrow, 1)), full((1, DN_HEAD_DIM))],
        out_specs=[tok(d_ssm), tok(d_ssm), tok(d_dn)],
        out_shape=(jax.ShapeDtypeStruct((bsz, seq, d_ssm), F32),
                   jax.ShapeDtypeStruct((bsz, seq, d_ssm), F32),
                   jax.ShapeDtypeStruct((bsz, seq, d_dn), F32)),
        scratch_shapes=[pltpu.VMEM((nbb, nslab, 8, LANES), F32),
                        pltpu.VMEM((nbb, nslab, 8 + tc, LANES), F32),
                        pltpu.VMEM((nbb, tc, 3 * d_dn), F32),
                        pltpu.VMEM((nbb, tc, d_dn), F32),
                        pltpu.VMEM((nbb, nrow, tc), F32),
                        pltpu.VMEM((nbb * heads, DN_HEAD_DIM, DN_HEAD_DIM), F32),
                        pltpu.VMEM((_M_PAIR + 1, tc, tc), F32),
                        pltpu.VMEM((n_lvl, tc, tc), BF16),
                        pltpu.VMEM((tc, 2 * tc), BF16)],
        compiler_params=pltpu.CompilerParams(dimension_semantics=("arbitrary", "arbitrary"),
                                             vmem_limit_bytes=VMEM_LIMIT),
        name="proj_gdn",
    )(x, g, w_main, w_ba, conv_w, conv_starts, col(a_log), col(dt_bias),
      norm_g.astype(F32).reshape(1, DN_HEAD_DIM))


def _out_kernel(ys_ref, yd_ref, x_ref, p_ref, wos_ref, wod_ref, wp_ref, pg_ref, wg_ref, fg_ref, o_ref, *, final):
    tm = x_ref.shape[1]
    rows = tm // PROJ_SUBTILES
    for s in range(PROJ_SUBTILES):
        r = slice(s * rows, (s + 1) * rows)
        h = x_ref[0, r] + _dot(ys_ref[0, r], wos_ref[...]) + _dot(yd_ref[0, r], wod_ref[...])
        e = _rms(_dot(p_ref[0, r], wp_ref[...]), pg_ref[...])
        h = h + jax.nn.sigmoid(_dot(h, wg_ref[...])) * e
        o_ref[0, r] = _rms(h, fg_ref[...]) if final else h


def _out_proj(ys, yd, x, p, w_out_s, w_out_d, w_ple, ple_g, w_gate, final_g, final):
    bsz, seq, dm = x.shape
    d_ssm = w_out_s.shape[0]
    d_dn = w_out_d.shape[0]
    tm = min(PROJ_TILE, seq)
    full = lambda shape: pl.BlockSpec(shape, lambda b, l: (0,) * len(shape))
    tok = lambda c: pl.BlockSpec((1, tm, c), lambda b, l: (b, l, 0))
    return pl.pallas_call(
        functools.partial(_out_kernel, final=final),
        grid=(bsz, seq // tm),
        in_specs=[tok(d_ssm), tok(d_dn), tok(dm), tok(p.shape[-1]),
                  full(w_out_s.shape), full(w_out_d.shape), full(w_ple.shape), full((1, dm)),
                  full(w_gate.shape), full((1, dm))],
        out_specs=tok(dm),
        out_shape=jax.ShapeDtypeStruct((bsz, seq, dm), F32),
        compiler_params=pltpu.CompilerParams(dimension_semantics=("arbitrary", "arbitrary"),
                                             vmem_limit_bytes=VMEM_LIMIT),
        name="out_proj",
    )(ys, yd, x, p, w_out_s, w_out_d, w_ple, ple_g, w_gate, final_g)


def kernel(x, p, norm_mix_g, w_in, ssm_A_re, ssm_A_im, ssm_B_re, ssm_B_im, ssm_C_re, ssm_C_im, ssm_D, ssm_log_dt, ssm_w_glu, ssm_b_glu, dn_conv_w, dn_A_log, dn_dt_bias, dn_norm_g, w_out, w_ple_proj, ple_norm_g, w_ple_gate, final_norm_g):
    bsz, seq, dm = x.shape
    depth = w_in.shape[0]
    d_ssm = ssm_D.shape[-1]
    heads = dn_A_log.shape[-1]
    d_dn = heads * DN_HEAD_DIM
    n_main = 2 * d_ssm + 4 * d_dn
    row = lambda t: t.astype(F32).reshape(1, -1)
    h = x
    for i in range(depth):
        w_main = w_in[i, :, :n_main].astype(BF16)
        w_ba = jnp.pad(w_in[i, :, n_main:], ((0, 0), (0, LANES - 2 * heads))).astype(BF16)
        us, zs, yd = _proj_gdn(h, row(norm_mix_g[i]), w_main, w_ba, dn_conv_w[i].astype(F32),
                               dn_A_log[i], dn_dt_bias[i], dn_norm_g[i], d_ssm, d_dn)

        a_r, a_i, bb_r, bb_i = _s5_prep(ssm_A_re[i], ssm_A_im[i], ssm_log_dt[i], ssm_B_re[i], ssm_B_im[i])
        bw, cw = _s5_layout(bb_r, bb_i, ssm_C_re[i], ssm_C_im[i])
        ys = _s5(us, zs, bw, cw, a_r, a_i, row(ssm_D[i]), ssm_w_glu[i].astype(BF16), row(ssm_b_glu[i]))

        h = _out_proj(ys, yd, h, p[i],
                      w_out[i, :d_ssm].astype(BF16), w_out[i, d_ssm:].astype(BF16),
                      w_ple_proj[i].astype(BF16), row(ple_norm_g[i]), w_ple_gate[i].astype(BF16),
                      row(final_norm_g), final=(i == depth - 1))
    return h
```

```python
import functools

import jax
import jax.numpy as jnp
from jax import lax
from jax.experimental import pallas as pl
from jax.experimental.pallas import tpu as pltpu

F32 = jnp.float32
BF16 = jnp.bfloat16

NORM_EPS = 1e-6
SSM_GROUP = 16
SSM_STATE = 64
DN_HEAD_DIM = 128
DN_CONV = 4
DN_CHUNK = 128

LANES = 128
SSM_LANE_BLOCK = 128
SSM_STATE_BLOCK = (SSM_LANE_BLOCK // SSM_GROUP) * SSM_STATE

PROJ_TILE = 1024
PROJ_SUBTILES = 4
S5_TIME_TILE = 64
S5_POST_SPLIT = 4
GDN_TILE = 128
GDN_BATCH_BLOCK = 4
GDN_PROJ_GROUP = 2
GDN_PREP_ROWS = 128
VMEM_LIMIT = 48 * 1024 * 1024


def _dot(a, b):
    return jnp.dot(a.astype(BF16), b.astype(BF16), preferred_element_type=F32)


def _dot_nt(a, b):
    return lax.dot_general(a.astype(BF16), b.astype(BF16), (((1,), (1,)), ((), ())),
                           preferred_element_type=F32)


def _rms(x, g):
    return x * lax.rsqrt(jnp.mean(x * x, axis=-1, keepdims=True) + NORM_EPS) * g


def _s5_prep_kernel(lr_ref, li_ref, ldt_ref, br_ref, bi_ref, abr_ref, abi_ref, bbr_ref, bbi_ref):
    lr = lr_ref[...]
    li = li_ref[...]
    dt = jnp.exp(ldt_ref[...])
    mag = jnp.exp(lr * dt)
    ang = li * dt
    ab_r = mag * jnp.cos(ang)
    ab_i = mag * jnp.sin(ang)
    den = lr * lr + li * li
    nr = ab_r - 1.0
    ni = ab_i
    cr = (nr * lr + ni * li) / den
    ci = (ni * lr - nr * li) / den
    abr_ref[...] = ab_r
    abi_ref[...] = ab_i
    br = br_ref[...]
    bi = bi_ref[...]
    bbr_ref[...] = cr * br - ci * bi
    bbi_ref[...] = cr * bi + ci * br


def _s5_prep(a_re, a_im, log_dt, b_re, b_im):
    g, p = a_re.shape
    h = b_re.shape[-1]
    n = g * p
    row = lambda t: t.reshape(1, n).astype(F32)
    chan = lambda t: t.astype(F32).transpose(2, 0, 1).reshape(h, n)
    ldt = jnp.broadcast_to(log_dt.astype(F32)[:, None], (g, p))
    return pl.pallas_call(
        _s5_prep_kernel,
        out_shape=(jax.ShapeDtypeStruct((1, n), F32), jax.ShapeDtypeStruct((1, n), F32),
                   jax.ShapeDtypeStruct((h, n), F32), jax.ShapeDtypeStruct((h, n), F32)),
        name="s5_prep",
    )(row(a_re), row(a_im), row(ldt), chan(b_re), chan(b_im))


def _s5_layout(bb_r, bb_i, c_re, c_im):
    g, h, p = c_re.shape
    gpb = SSM_LANE_BLOCK // h
    nb = g // gpb
    eye = jnp.eye(gpb, dtype=F32)

    def in_w(bb):
        t = bb.reshape(h, nb, gpb, p)
        return jnp.einsum('hjkp,gk->jghkp', t, eye).reshape(nb, gpb * h, gpb * p)

    def out_w(c):
        t = c.astype(F32).reshape(nb, gpb, h, p)
        return jnp.einsum('jghp,gk->jgpkh', t, eye).reshape(nb, gpb * p, gpb * h)

    bw = jnp.concatenate([in_w(bb_r), in_w(bb_i)], axis=2).astype(BF16)
    cw = jnp.concatenate([out_w(c_re), -out_w(c_im)], axis=1).astype(BF16)
    return bw, cw


def _s5_kernel(u_hbm, z_hbm, bw_ref, cw_ref, ar_ref, ai_ref, d_ref, wglu_ref, bglu_ref, y_hbm,
               ubuf, zbuf, ybuf, sem_in, sem_out, carry_ref, *blk_refs, tt, bsz, nb, nsteps):
    sb = SSM_STATE_BLOCK
    rows = tt * bsz
    t = pl.program_id(0)
    slot = t % 2
    bu_refs, s_refs = blk_refs[:nb], blk_refs[nb:]

    def in_copies(step, sl):
        cps = []
        for b in range(bsz):
            cps.append(pltpu.make_async_copy(u_hbm.at[b, pl.ds(step * tt, tt), :], ubuf.at[sl, :, b, :],
                                             sem_in.at[sl, 0]))
            cps.append(pltpu.make_async_copy(z_hbm.at[b, pl.ds(step * tt, tt), :], zbuf.at[sl, :, b, :],
                                             sem_in.at[sl, 1]))
        return cps

    def out_copies(step, sl):
        return [pltpu.make_async_copy(ybuf.at[sl, :, b, :], y_hbm.at[b, pl.ds(step * tt, tt), :], sem_out.at[sl])
                for b in range(bsz)]

    @pl.when(t == 0)
    def _():
        carry_ref[...] = jnp.zeros_like(carry_ref)
        for c in in_copies(0, 0):
            c.start()

    @pl.when(t >= 2)
    def _():
        for c in out_copies(t - 2, slot):
            c.wait()

    nxt = jnp.minimum(t + 1, nsteps - 1)
    for c in in_copies(nxt, 1 - slot):
        c.start()
    for c in in_copies(t, slot):
        c.wait()

    u = ubuf[slot].reshape(rows, nb * SSM_LANE_BLOCK)
    z = zbuf[slot].reshape(rows, nb * SSM_LANE_BLOCK)
    u16 = u.astype(BF16)
    th = tt // S5_POST_SPLIT

    def input_matmuls(part):
        r = slice(part * th * bsz, (part + 1) * th * bsz)
        for j in range(nb):
            bu_refs[j][r] = jnp.dot(u16[r, SSM_LANE_BLOCK * j:SSM_LANE_BLOCK * (j + 1)], bw_ref[j],
                                    preferred_element_type=F32)

    input_matmuls(0)
    a_re = [jnp.broadcast_to(ar_ref[:, sb * j:sb * (j + 1)], (bsz, sb)) for j in range(nb)]
    a_im = [jnp.broadcast_to(ai_ref[:, sb * j:sb * (j + 1)], (bsz, sb)) for j in range(nb)]
    state = [(carry_ref[:, 2 * sb * j:2 * sb * j + sb], carry_ref[:, 2 * sb * j + sb:2 * sb * (j + 1)])
             for j in range(nb)]
    for part in range(S5_POST_SPLIT):
        if part + 1 < S5_POST_SPLIT:
            input_matmuls(part + 1)
        for j in range(nb):
            s_r, s_i = state[j]
            for i in range(part * th, (part + 1) * th):
                r = slice(i * bsz, (i + 1) * bsz)
                s_r, s_i = (a_re[j] * s_r - a_im[j] * s_i + bu_refs[j][r, :sb],
                            a_re[j] * s_i + a_im[j] * s_r + bu_refs[j][r, sb:])
                s_refs[j][r, :sb] = s_r.astype(BF16)
                s_refs[j][r, sb:] = s_i.astype(BF16)
            state[j] = (s_r, s_i)
        r = slice(part * th * bsz, (part + 1) * th * bsz)
        y = jnp.concatenate(
            [jnp.dot(s_refs[j][r], cw_ref[j], preferred_element_type=F32)
             for j in range(nb)], axis=1)
        y = y + d_ref[...] * u[r]
        y = jax.nn.gelu(y)
        y = y * jax.nn.sigmoid(_dot(y, wglu_ref[...]) + bglu_ref[...])
        y = y * jax.nn.silu(z[r])
        ybuf[slot, part * th:(part + 1) * th] = y.reshape(th, bsz, y.shape[-1])
    for j in range(nb):
        carry_ref[:, 2 * sb * j:2 * sb * j + sb] = state[j][0]
        carry_ref[:, 2 * sb * j + sb:2 * sb * (j + 1)] = state[j][1]

    for c in out_copies(t, slot):
        c.start()

    @pl.when(t == nsteps - 1)
    def _():
        for c in in_copies(nxt, 1 - slot):
            c.wait()
        for c in out_copies(t, slot):
            c.wait()
        if nsteps >= 2:
            for c in out_copies(t - 1, 1 - slot):
                c.wait()


def _s5(us, zs, bw, cw, a_r, a_i, d_vec, w_glu, b_glu):
    bsz, seq, d_ssm = us.shape
    nb = bw.shape[0]
    tt = min(S5_TIME_TILE, seq)
    rows = tt * bsz
    nsteps = seq // tt
    n_state = 2 * SSM_STATE_BLOCK * nb
    full = lambda shape: pl.BlockSpec(shape, lambda t: (0,) * len(shape))
    hbm = pl.BlockSpec(memory_space=pl.ANY)
    return pl.pallas_call(
        functools.partial(_s5_kernel, tt=tt, bsz=bsz, nb=nb, nsteps=nsteps),
        grid=(nsteps,),
        in_specs=[hbm, hbm, full(bw.shape), full(cw.shape), full(a_r.shape), full(a_i.shape),
                  full(d_vec.shape), full(w_glu.shape), full(b_glu.shape)],
        out_specs=hbm,
        out_shape=jax.ShapeDtypeStruct((bsz, seq, d_ssm), F32),
        scratch_shapes=[pltpu.VMEM((2, tt, bsz, d_ssm), F32),
                        pltpu.VMEM((2, tt, bsz, d_ssm), F32),
                        pltpu.VMEM((2, tt, bsz, d_ssm), F32),
                        pltpu.SemaphoreType.DMA((2, 2)),
                        pltpu.SemaphoreType.DMA((2,)),
                        pltpu.VMEM((bsz, n_state), F32)]
        + [pltpu.VMEM((rows, 2 * SSM_STATE_BLOCK), F32) for _ in range(nb)]
        + [pltpu.VMEM((rows, 2 * SSM_STATE_BLOCK), BF16) for _ in range(nb)],
        compiler_params=pltpu.CompilerParams(dimension_semantics=("arbitrary",),
                                             vmem_limit_bytes=VMEM_LIMIT),
        name="s5",
    )(us, zs, bw, cw, a_r, a_i, d_vec, w_glu, b_glu)


def _chunk_sums(g, sum_ref):
    n = g.shape[0]
    hi = g.astype(BF16).astype(F32)
    mid = (g - hi).astype(BF16).astype(F32)
    lo = g - hi - mid
    parts = jnp.concatenate([hi, mid, lo], axis=0).astype(BF16)
    s = jnp.dot(parts, sum_ref[...], preferred_element_type=F32)
    s = s[:n] + s[n:2 * n] + s[2 * n:]
    tc = g.shape[1]
    return s[:, :tc], s[:, tc:]


_M_CAUSAL, _M_STRICT, _M_EYE, _M_PAIR = 0, 1, 2, 3


def _gdn_fill_masks(mask_ref, lvl_ref, sum_ref, tc):
    ri = lax.broadcasted_iota(jnp.int32, (tc, tc), 0)
    ci = lax.broadcasted_iota(jnp.int32, (tc, tc), 1)
    same = (ri // DN_CHUNK) == (ci // DN_CHUNK)
    mask_ref[_M_CAUSAL] = (same & (ri >= ci)).astype(F32)
    mask_ref[_M_STRICT] = (same & (ri > ci)).astype(F32)
    mask_ref[_M_EYE] = (ri == ci).astype(F32)
    sum_ref[:, :tc] = (same & (ri <= ci)).astype(F32).astype(BF16)
    sum_ref[:, tc:] = (same & (ri > ci)).astype(F32).astype(BF16)
    s = 1
    lvl = 0
    while s < DN_CHUNK:
        m = (((ri // (2 * s)) == (ci // (2 * s))) & (((ri // s) % 2) == 1) & (((ci // s) % 2) == 0)).astype(F32)
        if lvl == 0:
            mask_ref[_M_PAIR] = m
        else:
            lvl_ref[lvl - 1] = m.astype(BF16)
        s *= 2
        lvl += 1


def _gdn_prep(bb, qkv_ref, ba_ref, alog_ref, dtb_ref, mask_ref, sum_ref, tc, heads):
    hd = DN_HEAD_DIM
    d_dn = heads * hd
    ba = ba_ref[bb]
    beta_r = jax.nn.sigmoid(ba)
    g_r = -jnp.exp(alog_ref[...]) * jax.nn.softplus(ba + dtb_ref[...])
    gc_r, gs_r = _chunk_sums(g_r, sum_ref)
    egs_r = jnp.exp(gs_r)
    egl_r = jnp.exp(gc_r + gs_r)
    pad = jnp.zeros((LANES - 3 * 2 * heads, tc), F32)
    cols = jnp.concatenate([beta_r, gc_r, egl_r, pad], axis=0).T
    out = []
    for h in range(heads):
        q = qkv_ref[bb, :, hd * h:hd * (h + 1)]
        k = qkv_ref[bb, :, d_dn + hd * h:d_dn + hd * (h + 1)]
        v = qkv_ref[bb, :, 2 * d_dn + hd * h:2 * d_dn + hd * (h + 1)]
        beta_c = cols[:, h:h + 1]
        gc_c = cols[:, 3 * heads + h:3 * heads + h + 1]
        gc_row = gc_r[heads + h:heads + h + 1, :]
        eg_c = jnp.exp(gc_c)
        kb = k * beta_c
        k16 = k.astype(BF16)
        attn16, a16, t16 = [], [], []
        for r0 in range(0, tc, GDN_PREP_ROWS):
            r = slice(r0, r0 + GDN_PREP_ROWS)
            decay = jnp.exp(jnp.minimum(gc_c[r] - gc_row, 0.0))
            qk_kk = _dot_nt(jnp.concatenate([q[r], k[r]], axis=0), k16)
            a = qk_kk[GDN_PREP_ROWS:] * (beta_c[r] * decay) * mask_ref[_M_STRICT, r]
            attn16.append((qk_kk[:GDN_PREP_ROWS] * decay * mask_ref[_M_CAUSAL, r]).astype(BF16))
            a16.append(a.astype(BF16))
            t16.append((mask_ref[_M_EYE, r] - a * mask_ref[_M_PAIR, r]).astype(BF16))
        out.append(dict(
            egl_c=cols[:, 5 * heads + h:5 * heads + h + 1],
            qg16=(q * eg_c).astype(BF16),
            rhs16=jnp.concatenate([v * beta_c, kb * eg_c], axis=1).astype(BF16),
            kdec16=(k.T * egs_r[heads + h:heads + h + 1, :]).astype(BF16),
            attn16=jnp.concatenate(attn16, axis=0),
            a16=jnp.concatenate(a16, axis=0),
            t16=jnp.concatenate(t16, axis=0)))
    return out


def _gdn_body(units, zd_ref, ng_ref, y_ref, state_ref, lvl_ref, *, tc, heads):
    hd = DN_HEAD_DIM
    nchunk = tc // DN_CHUNK

    for lvl in range(lvl_ref.shape[0]):
        for u in units:
            u["x16"] = jnp.dot(u["a16"], u["t16"], preferred_element_type=F32).astype(BF16)
        for u in units:
            y16 = jnp.dot(u["t16"], u["x16"], preferred_element_type=F32).astype(BF16)
            u["t16"] = u["t16"] - y16 * lvl_ref[lvl]
    for u in units:
        sol = jnp.dot(u["t16"], u["rhs16"], preferred_element_type=F32)
        u["u"] = sol[:, :hd]
        u["w16"] = sol[:, hd:].astype(BF16)
        u["st"] = state_ref[u["bb"] * heads + u["h"]]
        u["outs"] = []

    for n in range(nchunk):
        r0 = n * DN_CHUNK
        r1 = r0 + DN_CHUNK
        for u in units:
            u["ws_qs"] = jnp.dot(jnp.concatenate([u["w16"][r0:r1], u["qg16"][r0:r1]], axis=0),
                                 u["st"].astype(BF16), preferred_element_type=F32)
        for u in units:
            v_new = (u["u"][r0:r1] - u["ws_qs"][:DN_CHUNK]).astype(BF16)
            u["av_kv"] = jnp.dot(jnp.concatenate([u["attn16"][r0:r1, r0:r1], u["kdec16"][:, r0:r1]], axis=0),
                                 v_new, preferred_element_type=F32)
        for u in units:
            u["outs"].append(u["ws_qs"][DN_CHUNK:] + u["av_kv"][:DN_CHUNK])
            u["st"] = u["st"] * u["egl_c"][r0:r1] + u["av_kv"][DN_CHUNK:]
    for u in units:
        bb, h = u["bb"], u["h"]
        state_ref[bb * heads + h] = u["st"]
        o = jnp.concatenate(u["outs"], axis=0)
        z = zd_ref[bb, :, hd * h:hd * (h + 1)]
        y_ref[bb, :, hd * h:hd * (h + 1)] = _rms(o, ng_ref[...]) * jax.nn.silu(z)


def _proj_gdn_kernel(x_ref, g_ref, wm_ref, wba_ref, cw_ref, cs_ref, alog_ref, dtb_ref, ng_ref,
                     us_ref, zs_ref, y_ref,
                     tail_ref, xpad_ref, qkv_ref, zd_ref, ba_ref, state_ref, mask_ref, lvl_ref, sum_ref,
                     *, d_ssm, d_dn, tc, heads, nbb):
    hd = DN_HEAD_DIM
    nslab = 3 * d_dn // LANES
    nrow = ba_ref.shape[1]

    @pl.when(pl.program_id(1) == 0)
    def _():
        tail_ref[...] = jnp.zeros_like(tail_ref)
        state_ref[...] = jnp.zeros_like(state_ref)
        _gdn_fill_masks(mask_ref, lvl_ref, sum_ref, tc)

    a = _rms(x_ref[...].reshape(nbb * tc, x_ref.shape[-1]), g_ref[...]).astype(BF16)
    others = [(us_ref, slice(0, d_ssm)), (zs_ref, slice(d_ssm, 2 * d_ssm)),
              (zd_ref, slice(2 * d_ssm + 3 * d_dn, 2 * d_ssm + 4 * d_dn))]

    def other_projection():
        ref, cols = others.pop(0)
        ref[...] = jnp.dot(a, wm_ref[:, cols], preferred_element_type=F32).reshape(ref.shape)

    units = []
    for bb in range(nbb):
        if bb % GDN_PROJ_GROUP == 0:
            grows = slice(bb * tc, (bb + GDN_PROJ_GROUP) * tc)
            qkv_g = jnp.dot(a[grows], wm_ref[:, 2 * d_ssm:2 * d_ssm + 3 * d_dn], preferred_element_type=F32)
            ba_g = jnp.dot(a[grows], wba_ref[...], preferred_element_type=F32)
            if others:
                other_projection()
        rows = slice((bb % GDN_PROJ_GROUP) * tc, (bb % GDN_PROJ_GROUP + 1) * tc)
        qkv = qkv_g[rows]
        ba_ref[bb] = ba_g[rows].T[:nrow]
        for s in range(nslab):
            lanes = slice(LANES * s, LANES * (s + 1))
            xpad_ref[bb, s, 0:8, :] = tail_ref[bb, s]
            xpad_ref[bb, s, 8:, :] = qkv[:, lanes]
            tail_ref[bb, s] = xpad_ref[bb, s, tc:, :]
            for r0 in range(0, tc, GDN_PREP_ROWS):
                acc = cw_ref[DN_CONV - 1:DN_CONV, lanes] * xpad_ref[bb, s, 8 + r0:8 + r0 + GDN_PREP_ROWS, :]
                for j in range(DN_CONV - 1):
                    acc = acc + cw_ref[j:j + 1, lanes] * xpad_ref[bb, s, pl.ds(cs_ref[j] + r0, GDN_PREP_ROWS), :]
                t = jax.nn.silu(acc)
                if LANES * s < 2 * d_dn:
                    t = t * lax.rsqrt(jnp.sum(t * t, axis=-1, keepdims=True) + NORM_EPS)
                    if LANES * s < d_dn:
                        t = t * (hd ** -0.5)
                qkv_ref[bb, r0:r0 + GDN_PREP_ROWS, lanes] = t

        for h, d in enumerate(_gdn_prep(bb, qkv_ref, ba_ref, alog_ref, dtb_ref, mask_ref, sum_ref, tc, heads)):
            units.append(dict(d, bb=bb, h=h))

    while others:
        other_projection()
    _gdn_body(units, zd_ref, ng_ref, y_ref, state_ref, lvl_ref, tc=tc, heads=heads)


def _proj_gdn(x, g, w_main, w_ba, conv_w, a_log, dt_bias, norm_g, d_ssm, d_dn):
    bsz, seq, dm = x.shape
    heads = d_dn // DN_HEAD_DIM
    tc = min(GDN_TILE, seq)
    nbb = GDN_BATCH_BLOCK
    nrow = 2 * heads
    nslab = 3 * d_dn // LANES
    assert DN_CHUNK == DN_HEAD_DIM and tc % DN_CHUNK == 0 and bsz % nbb == 0
    n_lvl = DN_CHUNK.bit_length() - 2
    col = lambda t: jnp.concatenate([jnp.zeros((heads,), F32), t.astype(F32)]).reshape(nrow, 1)
    full = lambda shape: pl.BlockSpec(shape, lambda b, l: (0,) * len(shape))
    tok = lambda c: pl.BlockSpec((nbb, tc, c), lambda b, l: (b, l, 0))
    conv_starts = 8 - (DN_CONV - 1) + jnp.arange(DN_CONV - 1, dtype=jnp.int32)
    return pl.pallas_call(
        functools.partial(_proj_gdn_kernel, d_ssm=d_ssm, d_dn=d_dn, tc=tc, heads=heads, nbb=nbb),
        grid=(bsz // nbb, seq // tc),
        in_specs=[tok(dm), full((1, dm)), full(w_main.shape), full(w_ba.shape), full(conv_w.shape),
                  pl.BlockSpec(memory_space=pltpu.SMEM),
                  full((nrow, 1)), full((nrow, 1)), full((1, DN_HEAD_DIM))],
        out_specs=[tok(d_ssm), tok(d_ssm), tok(d_dn)],
        out_shape=(jax.ShapeDtypeStruct((bsz, seq, d_ssm), F32),
                   jax.ShapeDtypeStruct((bsz, seq, d_ssm), F32),
                   jax.ShapeDtypeStruct((bsz, seq, d_dn), F32)),
        scratch_shapes=[pltpu.VMEM((nbb, nslab, 8, LANES), F32),
                        pltpu.VMEM((nbb, nslab, 8 + tc, LANES), F32),
                        pltpu.VMEM((nbb, tc, 3 * d_dn), F32),
                        pltpu.VMEM((nbb, tc, d_dn), F32),
                        pltpu.VMEM((nbb, nrow, tc), F32),
                        pltpu.VMEM((nbb * heads, DN_HEAD_DIM, DN_HEAD_DIM), F32),
                        pltpu.VMEM((_M_PAIR + 1, tc, tc), F32),
                        pltpu.VMEM((n_lvl, tc, tc), BF16),
                        pltpu.VMEM((tc, 2 * tc), BF16)],
        compiler_params=pltpu.CompilerParams(dimension_semantics=("arbitrary", "arbitrary"),
                                             vmem_limit_bytes=VMEM_LIMIT),
        name="proj_gdn",
    )(x, g, w_main, w_ba, conv_w, conv_starts, col(a_log), col(dt_bias),
      norm_g.astype(F32).reshape(1, DN_HEAD_DIM))


def _out_kernel(ys_ref, yd_ref, x_ref, p_ref, wos_ref, wod_ref, wp_ref, pg_ref, wg_ref, fg_ref, o_ref, *, final):
    tm = x_ref.shape[1]
    rows = tm // PROJ_SUBTILES
    for s in range(PROJ_SUBTILES):
        r = slice(s * rows, (s + 1) * rows)
        h = x_ref[0, r] + _dot(ys_ref[0, r], wos_ref[...]) + _dot(yd_ref[0, r], wod_ref[...])
        e = _rms(_dot(p_ref[0, r], wp_ref[...]), pg_ref[...])
        h = h + jax.nn.sigmoid(_dot(h, wg_ref[...])) * e
        o_ref[0, r] = _rms(h, fg_ref[...]) if final else h


def _out_proj(ys, yd, x, p, w_out_s, w_out_d, w_ple, ple_g, w_gate, final_g, final):
    bsz, seq, dm = x.shape
    d_ssm = w_out_s.shape[0]
    d_dn = w_out_d.shape[0]
    tm = min(PROJ_TILE, seq)
    full = lambda shape: pl.BlockSpec(shape, lambda b, l: (0,) * len(shape))
    tok = lambda c: pl.BlockSpec((1, tm, c), lambda b, l: (b, l, 0))
    return pl.pallas_call(
        functools.partial(_out_kernel, final=final),
        grid=(bsz, seq // tm),
        in_specs=[tok(d_ssm), tok(d_dn), tok(dm), tok(p.shape[-1]),
                  full(w_out_s.shape), full(w_out_d.shape), full(w_ple.shape), full((1, dm)),
                  full(w_gate.shape), full((1, dm))],
        out_specs=tok(dm),
        out_shape=jax.ShapeDtypeStruct((bsz, seq, dm), F32),
        compiler_params=pltpu.CompilerParams(dimension_semantics=("arbitrary", "arbitrary"),
                                             vmem_limit_bytes=VMEM_LIMIT),
        name="out_proj",
    )(ys, yd, x, p, w_out_s, w_out_d, w_ple, ple_g, w_gate, final_g)


def kernel(x, p, norm_mix_g, w_in, ssm_A_re, ssm_A_im, ssm_B_re, ssm_B_im, ssm_C_re, ssm_C_im, ssm_D, ssm_log_dt, ssm_w_glu, ssm_b_glu, dn_conv_w, dn_A_log, dn_dt_bias, dn_norm_g, w_out, w_ple_proj, ple_norm_g, w_ple_gate, final_norm_g):
    bsz, seq, dm = x.shape
    depth = w_in.shape[0]
    d_ssm = ssm_D.shape[-1]
    heads = dn_A_log.shape[-1]
    d_dn = heads * DN_HEAD_DIM
    n_main = 2 * d_ssm + 4 * d_dn
    row = lambda t: t.astype(F32).reshape(1, -1)
    h = x
    for i in range(depth):
        w_main = w_in[i, :, :n_main].astype(BF16)
        w_ba = jnp.pad(w_in[i, :, n_main:], ((0, 0), (0, LANES - 2 * heads))).astype(BF16)
        us, zs, yd = _proj_gdn(h, row(norm_mix_g[i]), w_main, w_ba, dn_conv_w[i].astype(F32),
                               dn_A_log[i], dn_dt_bias[i], dn_norm_g[i], d_ssm, d_dn)

        a_r, a_i, bb_r, bb_i = _s5_prep(ssm_A_re[i], ssm_A_im[i], ssm_log_dt[i], ssm_B_re[i], ssm_B_im[i])
        bw, cw = _s5_layout(bb_r, bb_i, ssm_C_re[i], ssm_C_im[i])
        ys = _s5(us, zs, bw, cw, a_r, a_i, row(ssm_D[i]), ssm_w_glu[i].astype(BF16), row(ssm_b_glu[i]))

        h = _out_proj(ys, yd, h, p[i],
                      w_out[i, :d_ssm].astype(BF16), w_out[i, d_ssm:].astype(BF16),
                      w_ple_proj[i].astype(BF16), row(ple_norm_g[i]), w_ple_gate[i].astype(BF16),
                      row(final_norm_g), final=(i == depth - 1))
    return h
```

```python
import functools

import jax
import jax.numpy as jnp
from jax import lax
from jax.experimental import pallas as pl
from jax.experimental.pallas import tpu as pltpu

F32 = jnp.float32
BF16 = jnp.bfloat16

NORM_EPS = 1e-6
SSM_GROUP = 16
SSM_STATE = 64
DN_HEAD_DIM = 128
DN_CONV = 4
DN_CHUNK = 128

LANES = 128
SSM_LANE_BLOCK = 128
SSM_STATE_BLOCK = (SSM_LANE_BLOCK // SSM_GROUP) * SSM_STATE

PROJ_TILE = 1024
PROJ_SUBTILES = 4
S5_TIME_TILE = 64
S5_POST_SPLIT = 4
GDN_TILE = 128
GDN_BATCH_BLOCK = 4
GDN_PROJ_GROUP = 2
GDN_PREP_ROWS = 64
VMEM_LIMIT = 48 * 1024 * 1024


def _dot(a, b):
    return jnp.dot(a.astype(BF16), b.astype(BF16), preferred_element_type=F32)


def _dot_nt(a, b):
    return lax.dot_general(a.astype(BF16), b.astype(BF16), (((1,), (1,)), ((), ())),
                           preferred_element_type=F32)


def _rms(x, g):
    return x * lax.rsqrt(jnp.mean(x * x, axis=-1, keepdims=True) + NORM_EPS) * g


def _s5_prep_kernel(lr_ref, li_ref, ldt_ref, br_ref, bi_ref, abr_ref, abi_ref, bbr_ref, bbi_ref):
    lr = lr_ref[...]
    li = li_ref[...]
    dt = jnp.exp(ldt_ref[...])
    mag = jnp.exp(lr * dt)
    ang = li * dt
    ab_r = mag * jnp.cos(ang)
    ab_i = mag * jnp.sin(ang)
    den = lr * lr + li * li
    nr = ab_r - 1.0
    ni = ab_i
    cr = (nr * lr + ni * li) / den
    ci = (ni * lr - nr * li) / den
    abr_ref[...] = ab_r
    abi_ref[...] = ab_i
    br = br_ref[...]
    bi = bi_ref[...]
    bbr_ref[...] = cr * br - ci * bi
    bbi_ref[...] = cr * bi + ci * br


def _s5_prep(a_re, a_im, log_dt, b_re, b_im):
    g, p = a_re.shape
    h = b_re.shape[-1]
    n = g * p
    row = lambda t: t.reshape(1, n).astype(F32)
    chan = lambda t: t.astype(F32).transpose(2, 0, 1).reshape(h, n)
    ldt = jnp.broadcast_to(log_dt.astype(F32)[:, None], (g, p))
    return pl.pallas_call(
        _s5_prep_kernel,
        out_shape=(jax.ShapeDtypeStruct((1, n), F32), jax.ShapeDtypeStruct((1, n), F32),
                   jax.ShapeDtypeStruct((h, n), F32), jax.ShapeDtypeStruct((h, n), F32)),
        name="s5_prep",
    )(row(a_re), row(a_im), row(ldt), chan(b_re), chan(b_im))


def _s5_layout(bb_r, bb_i, c_re, c_im):
    g, h, p = c_re.shape
    gpb = SSM_LANE_BLOCK // h
    nb = g // gpb
    eye = jnp.eye(gpb, dtype=F32)

    def in_w(bb):
        t = bb.reshape(h, nb, gpb, p)
        return jnp.einsum('hjkp,gk->jghkp', t, eye).reshape(nb, gpb * h, gpb * p)

    def out_w(c):
        t = c.astype(F32).reshape(nb, gpb, h, p)
        return jnp.einsum('jghp,gk->jgpkh', t, eye).reshape(nb, gpb * p, gpb * h)

    bw = jnp.concatenate([in_w(bb_r), in_w(bb_i)], axis=2).astype(BF16)
    cw = jnp.concatenate([out_w(c_re), -out_w(c_im)], axis=1).astype(BF16)
    return bw, cw


def _s5_kernel(u_hbm, z_hbm, bw_ref, cw_ref, ar_ref, ai_ref, d_ref, wglu_ref, bglu_ref, y_hbm,
               ubuf, zbuf, ybuf, sem_in, sem_out, carry_ref, *blk_refs, tt, bsz, nb, nsteps):
    sb = SSM_STATE_BLOCK
    rows = tt * bsz
    t = pl.program_id(0)
    slot = t % 2
    bu_refs, s_refs = blk_refs[:nb], blk_refs[nb:]

    def in_copies(step, sl):
        cps = []
        for b in range(bsz):
            cps.append(pltpu.make_async_copy(u_hbm.at[b, pl.ds(step * tt, tt), :], ubuf.at[sl, :, b, :],
                                             sem_in.at[sl, 0]))
            cps.append(pltpu.make_async_copy(z_hbm.at[b, pl.ds(step * tt, tt), :], zbuf.at[sl, :, b, :],
                                             sem_in.at[sl, 1]))
        return cps

    def out_copies(step, sl):
        return [pltpu.make_async_copy(ybuf.at[sl, :, b, :], y_hbm.at[b, pl.ds(step * tt, tt), :], sem_out.at[sl])
                for b in range(bsz)]

    @pl.when(t == 0)
    def _():
        carry_ref[...] = jnp.zeros_like(carry_ref)
        for c in in_copies(0, 0):
            c.start()

    @pl.when(t >= 2)
    def _():
        for c in out_copies(t - 2, slot):
            c.wait()

    nxt = jnp.minimum(t + 1, nsteps - 1)
    for c in in_copies(nxt, 1 - slot):
        c.start()
    for c in in_copies(t, slot):
        c.wait()

    u = ubuf[slot].reshape(rows, nb * SSM_LANE_BLOCK)
    z = zbuf[slot].reshape(rows, nb * SSM_LANE_BLOCK)
    u16 = u.astype(BF16)
    th = tt // S5_POST_SPLIT

    def input_matmuls(part):
        r = slice(part * th * bsz, (part + 1) * th * bsz)
        for j in range(nb):
            bu_refs[j][r] = jnp.dot(u16[r, SSM_LANE_BLOCK * j:SSM_LANE_BLOCK * (j + 1)], bw_ref[j],
                                    preferred_element_type=F32)

    input_matmuls(0)
    a_re = [jnp.broadcast_to(ar_ref[:, sb * j:sb * (j + 1)], (bsz, sb)) for j in range(nb)]
    a_im = [jnp.broadcast_to(ai_ref[:, sb * j:sb * (j + 1)], (bsz, sb)) for j in range(nb)]
    state = [(carry_ref[:, 2 * sb * j:2 * sb * j + sb], carry_ref[:, 2 * sb * j + sb:2 * sb * (j + 1)])
             for j in range(nb)]
    for part in range(S5_POST_SPLIT):
        if part + 1 < S5_POST_SPLIT:
            input_matmuls(part + 1)
        for j in range(nb):
            s_r, s_i = state[j]
            for i in range(part * th, (part + 1) * th):
                r = slice(i * bsz, (i + 1) * bsz)
                s_r, s_i = (a_re[j] * s_r - a_im[j] * s_i + bu_refs[j][r, :sb],
                            a_re[j] * s_i + a_im[j] * s_r + bu_refs[j][r, sb:])
                s_refs[j][r, :sb] = s_r.astype(BF16)
                s_refs[j][r, sb:] = s_i.astype(BF16)
            state[j] = (s_r, s_i)
        r = slice(part * th * bsz, (part + 1) * th * bsz)
        y = jnp.concatenate(
            [jnp.dot(s_refs[j][r], cw_ref[j], preferred_element_type=F32)
             for j in range(nb)], axis=1)
        y = y + d_ref[...] * u[r]
        y = jax.nn.gelu(y)
        y = y * jax.nn.sigmoid(_dot(y, wglu_ref[...]) + bglu_ref[...])
        y = y * jax.nn.silu(z[r])
        ybuf[slot, part * th:(part + 1) * th] = y.reshape(th, bsz, y.shape[-1])
    for j in range(nb):
        carry_ref[:, 2 * sb * j:2 * sb * j + sb] = state[j][0]
        carry_ref[:, 2 * sb * j + sb:2 * sb * (j + 1)] = state[j][1]

    for c in out_copies(t, slot):
        c.start()

    @pl.when(t == nsteps - 1)
    def _():
        for c in in_copies(nxt, 1 - slot):
            c.wait()
        for c in out_copies(t, slot):
            c.wait()
        if nsteps >= 2:
            for c in out_copies(t - 1, 1 - slot):
                c.wait()


def _s5(us, zs, bw, cw, a_r, a_i, d_vec, w_glu, b_glu):
    bsz, seq, d_ssm = us.shape
    nb = bw.shape[0]
    tt = min(S5_TIME_TILE, seq)
    rows = tt * bsz
    nsteps = seq // tt
    n_state = 2 * SSM_STATE_BLOCK * nb
    full = lambda shape: pl.BlockSpec(shape, lambda t: (0,) * len(shape))
    hbm = pl.BlockSpec(memory_space=pl.ANY)
    return pl.pallas_call(
        functools.partial(_s5_kernel, tt=tt, bsz=bsz, nb=nb, nsteps=nsteps),
        grid=(nsteps,),
        in_specs=[hbm, hbm, full(bw.shape), full(cw.shape), full(a_r.shape), full(a_i.shape),
                  full(d_vec.shape), full(w_glu.shape), full(b_glu.shape)],
        out_specs=hbm,
        out_shape=jax.ShapeDtypeStruct((bsz, seq, d_ssm), F32),
        scratch_shapes=[pltpu.VMEM((2, tt, bsz, d_ssm), F32),
                        pltpu.VMEM((2, tt, bsz, d_ssm), F32),
                        pltpu.VMEM((2, tt, bsz, d_ssm), F32),
                        pltpu.SemaphoreType.DMA((2, 2)),
                        pltpu.SemaphoreType.DMA((2,)),
                        pltpu.VMEM((bsz, n_state), F32)]
        + [pltpu.VMEM((rows, 2 * SSM_STATE_BLOCK), F32) for _ in range(nb)]
        + [pltpu.VMEM((rows, 2 * SSM_STATE_BLOCK), BF16) for _ in range(nb)],
        compiler_params=pltpu.CompilerParams(dimension_semantics=("arbitrary",),
                                             vmem_limit_bytes=VMEM_LIMIT),
        name="s5",
    )(us, zs, bw, cw, a_r, a_i, d_vec, w_glu, b_glu)


def _chunk_sums(g, sum_ref):
    n = g.shape[0]
    hi = g.astype(BF16).astype(F32)
    mid = (g - hi).astype(BF16).astype(F32)
    lo = g - hi - mid
    parts = jnp.concatenate([hi, mid, lo], axis=0).astype(BF16)
    s = jnp.dot(parts, sum_ref[...], preferred_element_type=F32)
    s = s[:n] + s[n:2 * n] + s[2 * n:]
    tc = g.shape[1]
    return s[:, :tc], s[:, tc:]


_M_CAUSAL, _M_STRICT, _M_EYE, _M_PAIR = 0, 1, 2, 3


def _gdn_fill_masks(mask_ref, lvl_ref, sum_ref, tc):
    ri = lax.broadcasted_iota(jnp.int32, (tc, tc), 0)
    ci = lax.broadcasted_iota(jnp.int32, (tc, tc), 1)
    same = (ri // DN_CHUNK) == (ci // DN_CHUNK)
    mask_ref[_M_CAUSAL] = (same & (ri >= ci)).astype(F32)
    mask_ref[_M_STRICT] = (same & (ri > ci)).astype(F32)
    mask_ref[_M_EYE] = (ri == ci).astype(F32)
    sum_ref[:, :tc] = (same & (ri <= ci)).astype(F32).astype(BF16)
    sum_ref[:, tc:] = (same & (ri > ci)).astype(F32).astype(BF16)
    s = 1
    lvl = 0
    while s < DN_CHUNK:
        m = (((ri // (2 * s)) == (ci // (2 * s))) & (((ri // s) % 2) == 1) & (((ci // s) % 2) == 0)).astype(F32)
        if lvl == 0:
            mask_ref[_M_PAIR] = m
        else:
            lvl_ref[lvl - 1] = m.astype(BF16)
        s *= 2
        lvl += 1


def _gdn_prep(bb, qkv_ref, ba_ref, alog_ref, dtb_ref, mask_ref, sum_ref, tc, heads):
    hd = DN_HEAD_DIM
    d_dn = heads * hd
    ba = ba_ref[bb]
    beta_r = jax.nn.sigmoid(ba)
    g_r = -jnp.exp(alog_ref[...]) * jax.nn.softplus(ba + dtb_ref[...])
    gc_r, gs_r = _chunk_sums(g_r, sum_ref)
    egs_r = jnp.exp(gs_r)
    egl_r = jnp.exp(gc_r + gs_r)
    pad = jnp.zeros((LANES - 3 * 2 * heads, tc), F32)
    cols = jnp.concatenate([beta_r, gc_r, egl_r, pad], axis=0).T
    out = []
    for h in range(heads):
        q = qkv_ref[bb, :, hd * h:hd * (h + 1)]
        k = qkv_ref[bb, :, d_dn + hd * h:d_dn + hd * (h + 1)]
        v = qkv_ref[bb, :, 2 * d_dn + hd * h:2 * d_dn + hd * (h + 1)]
        beta_c = cols[:, h:h + 1]
        gc_c = cols[:, 3 * heads + h:3 * heads + h + 1]
        gc_row = gc_r[heads + h:heads + h + 1, :]
        eg_c = jnp.exp(gc_c)
        kb = k * beta_c
        k16 = k.astype(BF16)
        attn16, a16, t16 = [], [], []
        for r0 in range(0, tc, GDN_PREP_ROWS):
            r = slice(r0, r0 + GDN_PREP_ROWS)
            decay = jnp.exp(jnp.minimum(gc_c[r] - gc_row, 0.0))
            qk_kk = _dot_nt(jnp.concatenate([q[r], k[r]], axis=0), k16)
            a = qk_kk[GDN_PREP_ROWS:] * (beta_c[r] * decay) * mask_ref[_M_STRICT, r]
            attn16.append((qk_kk[:GDN_PREP_ROWS] * decay * mask_ref[_M_CAUSAL, r]).astype(BF16))
            a16.append(a.astype(BF16))
            t16.append((mask_ref[_M_EYE, r] - a * mask_ref[_M_PAIR, r]).astype(BF16))
        out.append(dict(
            egl_c=cols[:, 5 * heads + h:5 * heads + h + 1],
            qg16=(q * eg_c).astype(BF16),
            rhs16=jnp.concatenate([v * beta_c, kb * eg_c], axis=1).astype(BF16),
            kdec16=(k.T * egs_r[heads + h:heads + h + 1, :]).astype(BF16),
            attn16=jnp.concatenate(attn16, axis=0),
            a16=jnp.concatenate(a16, axis=0),
            t16=jnp.concatenate(t16, axis=0)))
    return out


def _gdn_body(units, zd_ref, ng_ref, y_ref, state_ref, lvl_ref, *, tc, heads):
    hd = DN_HEAD_DIM
    nchunk = tc // DN_CHUNK

    for lvl in range(lvl_ref.shape[0]):
        for u in units:
            u["x16"] = jnp.dot(u["a16"], u["t16"], preferred_element_type=F32).astype(BF16)
        for u in units:
            y16 = jnp.dot(u["t16"], u["x16"], preferred_element_type=F32).astype(BF16)
            u["t16"] = u["t16"] - y16 * lvl_ref[lvl]
    for u in units:
        sol = jnp.dot(u["t16"], u["rhs16"], preferred_element_type=F32)
        u["u"] = sol[:, :hd]
        u["w16"] = sol[:, hd:].astype(BF16)
        u["st"] = state_ref[u["bb"] * heads + u["h"]]
        u["outs"] = []

    for n in range(nchunk):
        r0 = n * DN_CHUNK
        r1 = r0 + DN_CHUNK
        for u in units:
            u["ws_qs"] = jnp.dot(jnp.concatenate([u["w16"][r0:r1], u["qg16"][r0:r1]], axis=0),
                                 u["st"].astype(BF16), preferred_element_type=F32)
        for u in units:
            v_new = (u["u"][r0:r1] - u["ws_qs"][:DN_CHUNK]).astype(BF16)
            u["av_kv"] = jnp.dot(jnp.concatenate([u["attn16"][r0:r1, r0:r1], u["kdec16"][:, r0:r1]], axis=0),
                                 v_new, preferred_element_type=F32)
        for u in units:
            u["outs"].append(u["ws_qs"][DN_CHUNK:] + u["av_kv"][:DN_CHUNK])
            u["st"] = u["st"] * u["egl_c"][r0:r1] + u["av_kv"][DN_CHUNK:]
    for u in units:
        bb, h = u["bb"], u["h"]
        state_ref[bb * heads + h] = u["st"]
        o = jnp.concatenate(u["outs"], axis=0)
        z = zd_ref[bb, :, hd * h:hd * (h + 1)]
        y_ref[bb, :, hd * h:hd * (h + 1)] = _rms(o, ng_ref[...]) * jax.nn.silu(z)


def _proj_gdn_kernel(x_ref, g_ref, wm_ref, wba_ref, cw_ref, cs_ref, alog_ref, dtb_ref, ng_ref,
                     us_ref, zs_ref, y_ref,
                     tail_ref, xpad_ref, qkv_ref, zd_ref, ba_ref, state_ref, mask_ref, lvl_ref, sum_ref,
                     *, d_ssm, d_dn, tc, heads, nbb):
    hd = DN_HEAD_DIM
    nslab = 3 * d_dn // LANES
    nrow = ba_ref.shape[1]

    @pl.when(pl.program_id(1) == 0)
    def _():
        tail_ref[...] = jnp.zeros_like(tail_ref)
        state_ref[...] = jnp.zeros_like(state_ref)
        _gdn_fill_masks(mask_ref, lvl_ref, sum_ref, tc)

    a = _rms(x_ref[...].reshape(nbb * tc, x_ref.shape[-1]), g_ref[...]).astype(BF16)
    others = [(us_ref, slice(0, d_ssm)), (zs_ref, slice(d_ssm, 2 * d_ssm)),
              (zd_ref, slice(2 * d_ssm + 3 * d_dn, 2 * d_ssm + 4 * d_dn))]

    def other_projection():
        ref, cols = others.pop(0)
        ref[...] = jnp.dot(a, wm_ref[:, cols], preferred_element_type=F32).reshape(ref.shape)

    units = []
    for bb in range(nbb):
        if bb % GDN_PROJ_GROUP == 0:
            grows = slice(bb * tc, (bb + GDN_PROJ_GROUP) * tc)
            qkv_g = jnp.dot(a[grows], wm_ref[:, 2 * d_ssm:2 * d_ssm + 3 * d_dn], preferred_element_type=F32)
            ba_g = jnp.dot(a[grows], wba_ref[...], preferred_element_type=F32)
            if others:
                other_projection()
        rows = slice((bb % GDN_PROJ_GROUP) * tc, (bb % GDN_PROJ_GROUP + 1) * tc)
        qkv = qkv_g[rows]
        ba_ref[bb] = ba_g[rows].T[:nrow]
        for s in range(nslab):
            lanes = slice(LANES * s, LANES * (s + 1))
            xpad_ref[bb, s, 0:8, :] = tail_ref[bb, s]
            xpad_ref[bb, s, 8:, :] = qkv[:, lanes]
            tail_ref[bb, s] = xpad_ref[bb, s, tc:, :]
            for r0 in range(0, tc, GDN_PREP_ROWS):
                acc = cw_ref[DN_CONV - 1:DN_CONV, lanes] * xpad_ref[bb, s, 8 + r0:8 + r0 + GDN_PREP_ROWS, :]
                for j in range(DN_CONV - 1):
                    acc = acc + cw_ref[j:j + 1, lanes] * xpad_ref[bb, s, pl.ds(cs_ref[j] + r0, GDN_PREP_ROWS), :]
                t = jax.nn.silu(acc)
                if LANES * s < 2 * d_dn:
                    t = t * lax.rsqrt(jnp.sum(t * t, axis=-1, keepdims=True) + NORM_EPS)
                    if LANES * s < d_dn:
                        t = t * (hd ** -0.5)
                qkv_ref[bb, r0:r0 + GDN_PREP_ROWS, lanes] = t

        for h, d in enumerate(_gdn_prep(bb, qkv_ref, ba_ref, alog_ref, dtb_ref, mask_ref, sum_ref, tc, heads)):
            units.append(dict(d, bb=bb, h=h))

    while others:
        other_projection()
    _gdn_body(units, zd_ref, ng_ref, y_ref, state_ref, lvl_ref, tc=tc, heads=heads)


def _proj_gdn(x, g, w_main, w_ba, conv_w, a_log, dt_bias, norm_g, d_ssm, d_dn):
    bsz, seq, dm = x.shape
    heads = d_dn // DN_HEAD_DIM
    tc = min(GDN_TILE, seq)
    nbb = GDN_BATCH_BLOCK
    nrow = 2 * heads
    nslab = 3 * d_dn // LANES
    assert DN_CHUNK == DN_HEAD_DIM and tc % DN_CHUNK == 0 and bsz % nbb == 0
    n_lvl = DN_CHUNK.bit_length() - 2
    col = lambda t: jnp.concatenate([jnp.zeros((heads,), F32), t.astype(F32)]).reshape(nrow, 1)
    full = lambda shape: pl.BlockSpec(shape, lambda b, l: (0,) * len(shape))
    tok = lambda c: pl.BlockSpec((nbb, tc, c), lambda b, l: (b, l, 0))
    conv_starts = 8 - (DN_CONV - 1) + jnp.arange(DN_CONV - 1, dtype=jnp.int32)
    return pl.pallas_call(
        functools.partial(_proj_gdn_kernel, d_ssm=d_ssm, d_dn=d_dn, tc=tc, heads=heads, nbb=nbb),
        grid=(bsz // nbb, seq // tc),
        in_specs=[tok(dm), full((1, dm)), full(w_main.shape), full(w_ba.shape), full(conv_w.shape),
                  pl.BlockSpec(memory_space=pltpu.SMEM),
                  full((nrow, 1)), full((nrow, 1)), full((1, DN_HEAD_DIM))],
        out_specs=[tok(d_ssm), tok(d_ssm), tok(d_dn)],
        out_shape=(jax.ShapeDtypeStruct((bsz, seq, d_ssm), F32),
                   jax.ShapeDtypeStruct((bsz, seq, d_ssm), F32),
                   jax.ShapeDtypeStruct((bsz, seq, d_dn), F32)),
        scratch_shapes=[pltpu.VMEM((nbb, nslab, 8, LANES), F32),
                        pltpu.VMEM((nbb, nslab, 8 + tc, LANES), F32),
                        pltpu.VMEM((nbb, tc, 3 * d_dn), F32),
                        pltpu.VMEM((nbb, tc, d_dn), F32),
                        pltpu.VMEM((nbb, nrow, tc), F32),
                        pltpu.VMEM((nbb * heads, DN_HEAD_DIM, DN_HEAD_DIM), F32),
                        pltpu.VMEM((_M_PAIR + 1, tc, tc), F32),
                        pltpu.VMEM((n_lvl, tc, tc), BF16),
                        pltpu.VMEM((tc, 2 * tc), BF16)],
        compiler_params=pltpu.CompilerParams(dimension_semantics=("arbitrary", "arbitrary"),
                                             vmem_limit_bytes=VMEM_LIMIT),
        name="proj_gdn",
    )(x, g, w_main, w_ba, conv_w, conv_starts, col(a_log), col(dt_bias),
      norm_g.astype(F32).reshape(1, DN_HEAD_DIM))


def _out_kernel(ys_ref, yd_ref, x_ref, p_ref, wos_ref, wod_ref, wp_ref, pg_ref, wg_ref, fg_ref, o_ref, *, final):
    tm = x_ref.shape[1]
    rows = tm // PROJ_SUBTILES
    for s in range(PROJ_SUBTILES):
        r = slice(s * rows, (s + 1) * rows)
        h = x_ref[0, r] + _dot(ys_ref[0, r], wos_ref[...]) + _dot(yd_ref[0, r], wod_ref[...])
        e = _rms(_dot(p_ref[0, r], wp_ref[...]), pg_ref[...])
        h = h + jax.nn.sigmoid(_dot(h, wg_ref[...])) * e
        o_ref[0, r] = _rms(h, fg_ref[...]) if final else h


def _out_proj(ys, yd, x, p, w_out_s, w_out_d, w_ple, ple_g, w_gate, final_g, final):
    bsz, seq, dm = x.shape
    d_ssm = w_out_s.shape[0]
    d_dn = w_out_d.shape[0]
    tm = min(PROJ_TILE, seq)
    full = lambda shape: pl.BlockSpec(shape, lambda b, l: (0,) * len(shape))
    tok = lambda c: pl.BlockSpec((1, tm, c), lambda b, l: (b, l, 0))
    return pl.pallas_call(
        functools.partial(_out_kernel, final=final),
        grid=(bsz, seq // tm),
        in_specs=[tok(d_ssm), tok(d_dn), tok(dm), tok(p.shape[-1]),
                  full(w_out_s.shape), full(w_out_d.shape), full(w_ple.shape), full((1, dm)),
                  full(w_gate.shape), full((1, dm))],
        out_specs=tok(dm),
        out_shape=jax.ShapeDtypeStruct((bsz, seq, dm), F32),
        compiler_params=pltpu.CompilerParams(dimension_semantics=("arbitrary", "arbitrary"),
                                             vmem_limit_bytes=VMEM_LIMIT),
        name="out_proj",
    )(ys, yd, x, p, w_out_s, w_out_d, w_ple, ple_g, w_gate, final_g)


def kernel(x, p, norm_mix_g, w_in, ssm_A_re, ssm_A_im, ssm_B_re, ssm_B_im, ssm_C_re, ssm_C_im, ssm_D, ssm_log_dt, ssm_w_glu, ssm_b_glu, dn_conv_w, dn_A_log, dn_dt_bias, dn_norm_g, w_out, w_ple_proj, ple_norm_g, w_ple_gate, final_norm_g):
    bsz, seq, dm = x.shape
    depth = w_in.shape[0]
    d_ssm = ssm_D.shape[-1]
    heads = dn_A_log.shape[-1]
    d_dn = heads * DN_HEAD_DIM
    n_main = 2 * d_ssm + 4 * d_dn
    row = lambda t: t.astype(F32).reshape(1, -1)
    h = x
    for i in range(depth):
        w_main = w_in[i, :, :n_main].astype(BF16)
        w_ba = jnp.pad(w_in[i, :, n_main:], ((0, 0), (0, LANES - 2 * heads))).astype(BF16)
        us, zs, yd = _proj_gdn(h, row(norm_mix_g[i]), w_main, w_ba, dn_conv_w[i].astype(F32),
                               dn_A_log[i], dn_dt_bias[i], dn_norm_g[i], d_ssm, d_dn)

        a_r, a_i, bb_r, bb_i = _s5_prep(ssm_A_re[i], ssm_A_im[i], ssm_log_dt[i], ssm_B_re[i], ssm_B_im[i])
        bw, cw = _s5_layout(bb_r, bb_i, ssm_C_re[i], ssm_C_im[i])
        ys = _s5(us, zs, bw, cw, a_r, a_i, row(ssm_D[i]), ssm_w_glu[i].astype(BF16), row(ssm_b_glu[i]))

        h = _out_proj(ys, yd, h, p[i],
                      w_out[i, :d_ssm].astype(BF16), w_out[i, d_ssm:].astype(BF16),
                      w_ple_proj[i].astype(BF16), row(ple_norm_g[i]), w_ple_gate[i].astype(BF16),
                      row(final_norm_g), final=(i == depth - 1))
    return h
```

```python
import functools

import jax
import jax.numpy as jnp
from jax import lax
from jax.experimental import pallas as pl
from jax.experimental.pallas import tpu as pltpu

F32 = jnp.float32
BF16 = jnp.bfloat16

NORM_EPS = 1e-6
SSM_GROUP = 16
SSM_STATE = 64
DN_HEAD_DIM = 128
DN_CONV = 4
DN_CHUNK = 128

LANES = 128
SSM_LANE_BLOCK = 128
SSM_STATE_BLOCK = (SSM_LANE_BLOCK // SSM_GROUP) * SSM_STATE

PROJ_TILE = 1024
PROJ_SUBTILES = 4
S5_TIME_TILE = 64
S5_POST_SPLIT = 4
GDN_TILE = 128
GDN_BATCH_BLOCK = 4
GDN_PROJ_GROUP = 2
GDN_PREP_ROWS = 64
VMEM_LIMIT = 48 * 1024 * 1024


def _dot(a, b):
    return jnp.dot(a.astype(BF16), b.astype(BF16), preferred_element_type=F32)


def _dot_nt(a, b):
    return lax.dot_general(a.astype(BF16), b.astype(BF16), (((1,), (1,)), ((), ())),
                           preferred_element_type=F32)


def _rms(x, g):
    return x * lax.rsqrt(jnp.mean(x * x, axis=-1, keepdims=True) + NORM_EPS) * g


def _s5_prep_kernel(lr_ref, li_ref, ldt_ref, br_ref, bi_ref, abr_ref, abi_ref, bbr_ref, bbi_ref):
    lr = lr_ref[...]
    li = li_ref[...]
    dt = jnp.exp(ldt_ref[...])
    mag = jnp.exp(lr * dt)
    ang = li * dt
    ab_r = mag * jnp.cos(ang)
    ab_i = mag * jnp.sin(ang)
    den = lr * lr + li * li
    nr = ab_r - 1.0
    ni = ab_i
    cr = (nr * lr + ni * li) / den
    ci = (ni * lr - nr * li) / den
    abr_ref[...] = ab_r
    abi_ref[...] = ab_i
    br = br_ref[...]
    bi = bi_ref[...]
    bbr_ref[...] = cr * br - ci * bi
    bbi_ref[...] = cr * bi + ci * br


def _s5_prep(a_re, a_im, log_dt, b_re, b_im):
    g, p = a_re.shape
    h = b_re.shape[-1]
    n = g * p
    row = lambda t: t.reshape(1, n).astype(F32)
    chan = lambda t: t.astype(F32).transpose(2, 0, 1).reshape(h, n)
    ldt = jnp.broadcast_to(log_dt.astype(F32)[:, None], (g, p))
    return pl.pallas_call(
        _s5_prep_kernel,
        out_shape=(jax.ShapeDtypeStruct((1, n), F32), jax.ShapeDtypeStruct((1, n), F32),
                   jax.ShapeDtypeStruct((h, n), F32), jax.ShapeDtypeStruct((h, n), F32)),
        name="s5_prep",
    )(row(a_re), row(a_im), row(ldt), chan(b_re), chan(b_im))


def _s5_layout(bb_r, bb_i, c_re, c_im):
    g, h, p = c_re.shape
    gpb = SSM_LANE_BLOCK // h
    nb = g // gpb
    eye = jnp.eye(gpb, dtype=F32)

    def in_w(bb):
        t = bb.reshape(h, nb, gpb, p)
        return jnp.einsum('hjkp,gk->jghkp', t, eye).reshape(nb, gpb * h, gpb * p)

    def out_w(c):
        t = c.astype(F32).reshape(nb, gpb, h, p)
        return jnp.einsum('jghp,gk->jgpkh', t, eye).reshape(nb, gpb * p, gpb * h)

    bw = jnp.concatenate([in_w(bb_r), in_w(bb_i)], axis=2).astype(BF16)
    cw = jnp.concatenate([out_w(c_re), -out_w(c_im)], axis=1).astype(BF16)
    return bw, cw


def _s5_kernel(u_hbm, z_hbm, bw_ref, cw_ref, ar_ref, ai_ref, d_ref, wglu_ref, bglu_ref, y_hbm,
               ubuf, zbuf, ybuf, sem_in, sem_out, carry_ref, *blk_refs, tt, bsz, nb, nsteps):
    sb = SSM_STATE_BLOCK
    rows = tt * bsz
    t = pl.program_id(0)
    slot = t % 2
    bu_refs, s_refs = blk_refs[:nb], blk_refs[nb:]

    def in_copies(step, sl):
        cps = []
        for b in range(bsz):
            cps.append(pltpu.make_async_copy(u_hbm.at[b, pl.ds(step * tt, tt), :], ubuf.at[sl, :, b, :],
                                             sem_in.at[sl, 0]))
            cps.append(pltpu.make_async_copy(z_hbm.at[b, pl.ds(step * tt, tt), :], zbuf.at[sl, :, b, :],
                                             sem_in.at[sl, 1]))
        return cps

    def out_copies(step, sl):
        return [pltpu.make_async_copy(ybuf.at[sl, :, b, :], y_hbm.at[b, pl.ds(step * tt, tt), :], sem_out.at[sl])
                for b in range(bsz)]

    @pl.when(t == 0)
    def _():
        carry_ref[...] = jnp.zeros_like(carry_ref)
        for c in in_copies(0, 0):
            c.start()

    @pl.when(t >= 2)
    def _():
        for c in out_copies(t - 2, slot):
            c.wait()

    nxt = jnp.minimum(t + 1, nsteps - 1)
    for c in in_copies(nxt, 1 - slot):
        c.start()
    for c in in_copies(t, slot):
        c.wait()

    u = ubuf[slot].reshape(rows, nb * SSM_LANE_BLOCK)
    z = zbuf[slot].reshape(rows, nb * SSM_LANE_BLOCK)
    u16 = u.astype(BF16)
    th = tt // S5_POST_SPLIT

    def input_matmuls(part):
        r = slice(part * th * bsz, (part + 1) * th * bsz)
        for j in range(nb):
            bu_refs[j][r] = jnp.dot(u16[r, SSM_LANE_BLOCK * j:SSM_LANE_BLOCK * (j + 1)], bw_ref[j],
                                    preferred_element_type=F32)

    input_matmuls(0)
    a_re = [jnp.broadcast_to(ar_ref[:, sb * j:sb * (j + 1)], (bsz, sb)) for j in range(nb)]
    a_im = [jnp.broadcast_to(ai_ref[:, sb * j:sb * (j + 1)], (bsz, sb)) for j in range(nb)]
    state = [(carry_ref[:, 2 * sb * j:2 * sb * j + sb], carry_ref[:, 2 * sb * j + sb:2 * sb * (j + 1)])
             for j in range(nb)]
    for part in range(S5_POST_SPLIT):
        if part + 1 < S5_POST_SPLIT:
            input_matmuls(part + 1)
        for j in range(nb):
            s_r, s_i = state[j]
            for i in range(part * th, (part + 1) * th):
                r = slice(i * bsz, (i + 1) * bsz)
                s_r, s_i = (a_re[j] * s_r - a_im[j] * s_i + bu_refs[j][r, :sb],
                            a_re[j] * s_i + a_im[j] * s_r + bu_refs[j][r, sb:])
                s_refs[j][r, :sb] = s_r.astype(BF16)
                s_refs[j][r, sb:] = s_i.astype(BF16)
            state[j] = (s_r, s_i)
        r = slice(part * th * bsz, (part + 1) * th * bsz)
        y = jnp.concatenate(
            [jnp.dot(s_refs[j][r], cw_ref[j], preferred_element_type=F32)
             for j in range(nb)], axis=1)
        y = y + d_ref[...] * u[r]
        y = jax.nn.gelu(y)
        y = y * jax.nn.sigmoid(_dot(y, wglu_ref[...]) + bglu_ref[...])
        y = y * jax.nn.silu(z[r])
        ybuf[slot, part * th:(part + 1) * th] = y.reshape(th, bsz, y.shape[-1])
    for j in range(nb):
        carry_ref[:, 2 * sb * j:2 * sb * j + sb] = state[j][0]
        carry_ref[:, 2 * sb * j + sb:2 * sb * (j + 1)] = state[j][1]

    for c in out_copies(t, slot):
        c.start()

    @pl.when(t == nsteps - 1)
    def _():
        for c in in_copies(nxt, 1 - slot):
            c.wait()
        for c in out_copies(t, slot):
            c.wait()
        if nsteps >= 2:
            for c in out_copies(t - 1, 1 - slot):
                c.wait()


def _s5(us, zs, bw, cw, a_r, a_i, d_vec, w_glu, b_glu):
    bsz, seq, d_ssm = us.shape
    nb = bw.shape[0]
    tt = min(S5_TIME_TILE, seq)
    rows = tt * bsz
    nsteps = seq // tt
    n_state = 2 * SSM_STATE_BLOCK * nb
    full = lambda shape: pl.BlockSpec(shape, lambda t: (0,) * len(shape))
    hbm = pl.BlockSpec(memory_space=pl.ANY)
    return pl.pallas_call(
        functools.partial(_s5_kernel, tt=tt, bsz=bsz, nb=nb, nsteps=nsteps),
        grid=(nsteps,),
        in_specs=[hbm, hbm, full(bw.shape), full(cw.shape), full(a_r.shape), full(a_i.shape),
                  full(d_vec.shape), full(w_glu.shape), full(b_glu.shape)],
        out_specs=hbm,
        out_shape=jax.ShapeDtypeStruct((bsz, seq, d_ssm), F32),
        scratch_shapes=[pltpu.VMEM((2, tt, bsz, d_ssm), F32),
                        pltpu.VMEM((2, tt, bsz, d_ssm), F32),
                        pltpu.VMEM((2, tt, bsz, d_ssm), F32),
                        pltpu.SemaphoreType.DMA((2, 2)),
                        pltpu.SemaphoreType.DMA((2,)),
                        pltpu.VMEM((bsz, n_state), F32)]
        + [pltpu.VMEM((rows, 2 * SSM_STATE_BLOCK), F32) for _ in range(nb)]
        + [pltpu.VMEM((rows, 2 * SSM_STATE_BLOCK), BF16) for _ in range(nb)],
        compiler_params=pltpu.CompilerParams(dimension_semantics=("arbitrary",),
                                             vmem_limit_bytes=VMEM_LIMIT),
        name="s5",
    )(us, zs, bw, cw, a_r, a_i, d_vec, w_glu, b_glu)


def _chunk_sums(g, sum_ref):
    n = g.shape[0]
    hi = g.astype(BF16).astype(F32)
    mid = (g - hi).astype(BF16).astype(F32)
    lo = g - hi - mid
    parts = jnp.concatenate([hi, mid, lo], axis=0).astype(BF16)
    s = jnp.dot(parts, sum_ref[...], preferred_element_type=F32)
    s = s[:n] + s[n:2 * n] + s[2 * n:]
    tc = g.shape[1]
    return s[:, :tc], s[:, tc:]


_M_CAUSAL, _M_STRICT, _M_EYE, _M_PAIR = 0, 1, 2, 3


def _gdn_fill_masks(mask_ref, lvl_ref, sum_ref, tc):
    ri = lax.broadcasted_iota(jnp.int32, (tc, tc), 0)
    ci = lax.broadcasted_iota(jnp.int32, (tc, tc), 1)
    same = (ri // DN_CHUNK) == (ci // DN_CHUNK)
    mask_ref[_M_CAUSAL] = (same & (ri >= ci)).astype(F32)
    mask_ref[_M_STRICT] = (same & (ri > ci)).astype(F32)
    mask_ref[_M_EYE] = (ri == ci).astype(F32)
    sum_ref[:, :tc] = (same & (ri <= ci)).astype(F32).astype(BF16)
    sum_ref[:, tc:] = (same & (ri > ci)).astype(F32).astype(BF16)
    s = 1
    lvl = 0
    while s < DN_CHUNK:
        m = (((ri // (2 * s)) == (ci // (2 * s))) & (((ri // s) % 2) == 1) & (((ci // s) % 2) == 0)).astype(F32)
        if lvl == 0:
            mask_ref[_M_PAIR] = m
        else:
            lvl_ref[lvl - 1] = m.astype(BF16)
        s *= 2
        lvl += 1


def _gdn_prep(bb, qkv_ref, ba_ref, alog_ref, dtb_ref, mask_ref, sum_ref, tc, heads):
    hd = DN_HEAD_DIM
    d_dn = heads * hd
    ba = ba_ref[bb]
    beta_r = jax.nn.sigmoid(ba)
    g_r = -jnp.exp(alog_ref[...]) * jax.nn.softplus(ba + dtb_ref[...])
    gc_r, gs_r = _chunk_sums(g_r, sum_ref)
    egs_r = jnp.exp(gs_r)
    egl_r = jnp.exp(gc_r + gs_r)
    pad = jnp.zeros((LANES - 3 * 2 * heads, tc), F32)
    cols = jnp.concatenate([beta_r, gc_r, egl_r, pad], axis=0).T
    out = []
    for h in range(heads):
        q = qkv_ref[bb, :, hd * h:hd * (h + 1)]
        k = qkv_ref[bb, :, d_dn + hd * h:d_dn + hd * (h + 1)]
        v = qkv_ref[bb, :, 2 * d_dn + hd * h:2 * d_dn + hd * (h + 1)]
        beta_c = cols[:, h:h + 1]
        gc_c = cols[:, 3 * heads + h:3 * heads + h + 1]
        gc_row = gc_r[heads + h:heads + h + 1, :]
        eg_c = jnp.exp(gc_c)
        kb = k * beta_c
        k16 = k.astype(BF16)
        attn16, a16, t16 = [], [], []
        for r0 in range(0, tc, GDN_PREP_ROWS):
            r = slice(r0, r0 + GDN_PREP_ROWS)
            decay = jnp.exp(jnp.minimum(gc_c[r] - gc_row, 0.0))
            qk_kk = _dot_nt(jnp.concatenate([q[r], k[r]], axis=0), k16)
            a = qk_kk[GDN_PREP_ROWS:] * (beta_c[r] * decay) * mask_ref[_M_STRICT, r]
            attn16.append((qk_kk[:GDN_PREP_ROWS] * decay * mask_ref[_M_CAUSAL, r]).astype(BF16))
            a16.append(a.astype(BF16))
            t16.append((mask_ref[_M_EYE, r] - a * mask_ref[_M_PAIR, r]).astype(BF16))
        out.append(dict(
            egl_c=cols[:, 5 * heads + h:5 * heads + h + 1],
            qg16=(q * eg_c).astype(BF16),
            rhs16=jnp.concatenate([v * beta_c, kb * eg_c], axis=1).astype(BF16),
            kdec16=(k.T * egs_r[heads + h:heads + h + 1, :]).astype(BF16),
            attn16=jnp.concatenate(attn16, axis=0),
            a16=jnp.concatenate(a16, axis=0),
            t16=jnp.concatenate(t16, axis=0)))
    return out


def _gdn_body(units, zd_ref, ng_ref, y_ref, state_ref, lvl_ref, *, tc, heads):
    hd = DN_HEAD_DIM
    nchunk = tc // DN_CHUNK

    for lvl in range(lvl_ref.shape[0]):
        for u in units:
            u["x16"] = jnp.dot(u["a16"], u["t16"], preferred_element_type=F32).astype(BF16)
        for u in units:
            y16 = jnp.dot(u["t16"], u["x16"], preferred_element_type=F32).astype(BF16)
            u["t16"] = u["t16"] - y16 * lvl_ref[lvl]
    for u in units:
        sol = jnp.dot(u["t16"], u["rhs16"], preferred_element_type=F32)
        u["u"] = sol[:, :hd]
        u["w16"] = sol[:, hd:].astype(BF16)
        u["st"] = state_ref[u["bb"] * heads + u["h"]]
        u["outs"] = []

    for n in range(nchunk):
        r0 = n * DN_CHUNK
        r1 = r0 + DN_CHUNK
        for u in units:
            u["ws_qs"] = jnp.dot(jnp.concatenate([u["w16"][r0:r1], u["qg16"][r0:r1]], axis=0),
                                 u["st"].astype(BF16), preferred_element_type=F32)
        for u in units:
            v_new = (u["u"][r0:r1] - u["ws_qs"][:DN_CHUNK]).astype(BF16)
            u["av_kv"] = jnp.dot(jnp.concatenate([u["attn16"][r0:r1, r0:r1], u["kdec16"][:, r0:r1]], axis=0),
                                 v_new, preferred_element_type=F32)
        for u in units:
            u["outs"].append(u["ws_qs"][DN_CHUNK:] + u["av_kv"][:DN_CHUNK])
            u["st"] = u["st"] * u["egl_c"][r0:r1] + u["av_kv"][DN_CHUNK:]
    for u in units:
        bb, h = u["bb"], u["h"]
        state_ref[bb * heads + h] = u["st"]
        o = jnp.concatenate(u["outs"], axis=0)
        z = zd_ref[bb, :, hd * h:hd * (h + 1)]
        y_ref[bb, :, hd * h:hd * (h + 1)] = _rms(o, ng_ref[...]) * jax.nn.silu(z)


def _proj_gdn_kernel(x_ref, g_ref, wm_ref, cw_ref, cs_ref, alog_ref, dtb_ref, ng_ref,
                     us_ref, zs_ref, y_ref,
                     tail_ref, xpad_ref, qkv_ref, zd_ref, ba_ref, state_ref, mask_ref, lvl_ref, sum_ref,
                     *, d_ssm, d_dn, tc, heads, nbb):
    hd = DN_HEAD_DIM
    nslab = 3 * d_dn // LANES
    nrow = ba_ref.shape[1]

    @pl.when(pl.program_id(1) == 0)
    def _():
        tail_ref[...] = jnp.zeros_like(tail_ref)
        state_ref[...] = jnp.zeros_like(state_ref)
        _gdn_fill_masks(mask_ref, lvl_ref, sum_ref, tc)

    a = _rms(x_ref[...].reshape(nbb * tc, x_ref.shape[-1]), g_ref[...]).astype(BF16)
    others = [(us_ref, slice(0, d_ssm)), (zs_ref, slice(d_ssm, 2 * d_ssm)),
              (zd_ref, slice(2 * d_ssm + 3 * d_dn, 2 * d_ssm + 4 * d_dn))]

    def other_projection():
        ref, cols = others.pop(0)
        ref[...] = jnp.dot(a, wm_ref[:, cols], preferred_element_type=F32).reshape(ref.shape)

    units = []
    for bb in range(nbb):
        if bb % GDN_PROJ_GROUP == 0:
            grows = slice(bb * tc, (bb + GDN_PROJ_GROUP) * tc)
            qkv_g = jnp.dot(a[grows], wm_ref[:, 2 * d_ssm:2 * d_ssm + 3 * d_dn], preferred_element_type=F32)
            ba_g = jnp.dot(a[grows], wm_ref[:, 2 * d_ssm + 4 * d_dn:], preferred_element_type=F32)
            if others:
                other_projection()
        rows = slice((bb % GDN_PROJ_GROUP) * tc, (bb % GDN_PROJ_GROUP + 1) * tc)
        qkv = qkv_g[rows]
        ba_ref[bb] = ba_g[rows].T[:nrow]
        for s in range(nslab):
            lanes = slice(LANES * s, LANES * (s + 1))
            xpad_ref[bb, s, 0:8, :] = tail_ref[bb, s]
            xpad_ref[bb, s, 8:, :] = qkv[:, lanes]
            tail_ref[bb, s] = xpad_ref[bb, s, tc:, :]
            for r0 in range(0, tc, GDN_PREP_ROWS):
                acc = cw_ref[DN_CONV - 1:DN_CONV, lanes] * xpad_ref[bb, s, 8 + r0:8 + r0 + GDN_PREP_ROWS, :]
                for j in range(DN_CONV - 1):
                    acc = acc + cw_ref[j:j + 1, lanes] * xpad_ref[bb, s, pl.ds(cs_ref[j] + r0, GDN_PREP_ROWS), :]
                t = jax.nn.silu(acc)
                if LANES * s < 2 * d_dn:
                    t = t * lax.rsqrt(jnp.sum(t * t, axis=-1, keepdims=True) + NORM_EPS)
                    if LANES * s < d_dn:
                        t = t * (hd ** -0.5)
                qkv_ref[bb, r0:r0 + GDN_PREP_ROWS, lanes] = t

        for h, d in enumerate(_gdn_prep(bb, qkv_ref, ba_ref, alog_ref, dtb_ref, mask_ref, sum_ref, tc, heads)):
            units.append(dict(d, bb=bb, h=h))

    while others:
        other_projection()
    _gdn_body(units, zd_ref, ng_ref, y_ref, state_ref, lvl_ref, tc=tc, heads=heads)


def _proj_gdn(x, g, w_main, conv_w, a_log, dt_bias, norm_g, d_ssm, d_dn):
    bsz, seq, dm = x.shape
    heads = d_dn // DN_HEAD_DIM
    tc = min(GDN_TILE, seq)
    nbb = GDN_BATCH_BLOCK
    nrow = 2 * heads
    nslab = 3 * d_dn // LANES
    assert DN_CHUNK == DN_HEAD_DIM and tc % DN_CHUNK == 0 and bsz % nbb == 0
    n_lvl = DN_CHUNK.bit_length() - 2
    col = lambda t: jnp.concatenate([jnp.zeros((heads,), F32), t.astype(F32)]).reshape(nrow, 1)
    full = lambda shape: pl.BlockSpec(shape, lambda b, l: (0,) * len(shape))
    tok = lambda c: pl.BlockSpec((nbb, tc, c), lambda b, l: (b, l, 0))
    conv_starts = 8 - (DN_CONV - 1) + jnp.arange(DN_CONV - 1, dtype=jnp.int32)
    return pl.pallas_call(
        functools.partial(_proj_gdn_kernel, d_ssm=d_ssm, d_dn=d_dn, tc=tc, heads=heads, nbb=nbb),
        grid=(bsz // nbb, seq // tc),
        in_specs=[tok(dm), full((1, dm)), full(w_main.shape), full(conv_w.shape),
                  pl.BlockSpec(memory_space=pltpu.SMEM),
                  full((nrow, 1)), full((nrow, 1)), full((1, DN_HEAD_DIM))],
        out_specs=[tok(d_ssm), tok(d_ssm), tok(d_dn)],
        out_shape=(jax.ShapeDtypeStruct((bsz, seq, d_ssm), F32),
                   jax.ShapeDtypeStruct((bsz, seq, d_ssm), F32),
                   jax.ShapeDtypeStruct((bsz, seq, d_dn), F32)),
        scratch_shapes=[pltpu.VMEM((nbb, nslab, 8, LANES), F32),
                        pltpu.VMEM((nbb, nslab, 8 + tc, LANES), F32),
                        pltpu.VMEM((nbb, tc, 3 * d_dn), F32),
                        pltpu.VMEM((nbb, tc, d_dn), F32),
                        pltpu.VMEM((nbb, nrow, tc), F32),
                        pltpu.VMEM((nbb * heads, DN_HEAD_DIM, DN_HEAD_DIM), F32),
                        pltpu.VMEM((_M_PAIR + 1, tc, tc), F32),
                        pltpu.VMEM((n_lvl, tc, tc), BF16),
                        pltpu.VMEM((tc, 2 * tc), BF16)],
        compiler_params=pltpu.CompilerParams(dimension_semantics=("arbitrary", "arbitrary"),
                                             vmem_limit_bytes=VMEM_LIMIT),
        name="proj_gdn",
    )(x, g, w_main, conv_w, conv_starts, col(a_log), col(dt_bias),
      norm_g.astype(F32).reshape(1, DN_HEAD_DIM))


def _out_kernel(ys_ref, yd_ref, x_ref, p_ref, wos_ref, wod_ref, wp_ref, pg_ref, wg_ref, fg_ref, o_ref, *, final):
    tm = x_ref.shape[1]
    rows = tm // PROJ_SUBTILES
    for s in range(PROJ_SUBTILES):
        r = slice(s * rows, (s + 1) * rows)
        h = x_ref[0, r] + _dot(ys_ref[0, r], wos_ref[...]) + _dot(yd_ref[0, r], wod_ref[...])
        e = _rms(_dot(p_ref[0, r], wp_ref[...]), pg_ref[...])
        h = h + jax.nn.sigmoid(_dot(h, wg_ref[...])) * e
        o_ref[0, r] = _rms(h, fg_ref[...]) if final else h


def _out_proj(ys, yd, x, p, w_out_s, w_out_d, w_ple, ple_g, w_gate, final_g, final):
    bsz, seq, dm = x.shape
    d_ssm = w_out_s.shape[0]
    d_dn = w_out_d.shape[0]
    tm = min(PROJ_TILE, seq)
    full = lambda shape: pl.BlockSpec(shape, lambda b, l: (0,) * len(shape))
    tok = lambda c: pl.BlockSpec((1, tm, c), lambda b, l: (b, l, 0))
    return pl.pallas_call(
        functools.partial(_out_kernel, final=final),
        grid=(bsz, seq // tm),
        in_specs=[tok(d_ssm), tok(d_dn), tok(dm), tok(p.shape[-1]),
                  full(w_out_s.shape), full(w_out_d.shape), full(w_ple.shape), full((1, dm)),
                  full(w_gate.shape), full((1, dm))],
        out_specs=tok(dm),
        out_shape=jax.ShapeDtypeStruct((bsz, seq, dm), F32),
        compiler_params=pltpu.CompilerParams(dimension_semantics=("arbitrary", "arbitrary"),
                                             vmem_limit_bytes=VMEM_LIMIT),
        name="out_proj",
    )(ys, yd, x, p, w_out_s, w_out_d, w_ple, ple_g, w_gate, final_g)


def kernel(x, p, norm_mix_g, w_in, ssm_A_re, ssm_A_im, ssm_B_re, ssm_B_im, ssm_C_re, ssm_C_im, ssm_D, ssm_log_dt, ssm_w_glu, ssm_b_glu, dn_conv_w, dn_A_log, dn_dt_bias, dn_norm_g, w_out, w_ple_proj, ple_norm_g, w_ple_gate, final_norm_g):
    bsz, seq, dm = x.shape
    depth = w_in.shape[0]
    d_ssm = ssm_D.shape[-1]
    heads = dn_A_log.shape[-1]
    d_dn = heads * DN_HEAD_DIM
    row = lambda t: t.astype(F32).reshape(1, -1)
    h = x
    for i in range(depth):
        w_all = jnp.pad(w_in[i], ((0, 0), (0, LANES - 2 * heads))).astype(BF16)
        us, zs, yd = _proj_gdn(h, row(norm_mix_g[i]), w_all, dn_conv_w[i].astype(F32),
                               dn_A_log[i], dn_dt_bias[i], dn_norm_g[i], d_ssm, d_dn)

        a_r, a_i, bb_r, bb_i = _s5_prep(ssm_A_re[i], ssm_A_im[i], ssm_log_dt[i], ssm_B_re[i], ssm_B_im[i])
        bw, cw = _s5_layout(bb_r, bb_i, ssm_C_re[i], ssm_C_im[i])
        ys = _s5(us, zs, bw, cw, a_r, a_i, row(ssm_D[i]), ssm_w_glu[i].astype(BF16), row(ssm_b_glu[i]))

        h = _out_proj(ys, yd, h, p[i],
                      w_out[i, :d_ssm].astype(BF16), w_out[i, d_ssm:].astype(BF16),
                      w_ple_proj[i].astype(BF16), row(ple_norm_g[i]), w_ple_gate[i].astype(BF16),
                      row(final_norm_g), final=(i == depth - 1))
    return h
```

```python
import functools

import jax
import jax.numpy as jnp
from jax import lax
from jax.experimental import pallas as pl
from jax.experimental.pallas import tpu as pltpu

F32 = jnp.float32
BF16 = jnp.bfloat16

NORM_EPS = 1e-6
SSM_GROUP = 16
SSM_STATE = 64
DN_HEAD_DIM = 128
DN_CONV = 4
DN_CHUNK = 128

LANES = 128
SSM_LANE_BLOCK = 128
SSM_STATE_BLOCK = (SSM_LANE_BLOCK // SSM_GROUP) * SSM_STATE

PROJ_TILE = 1024
PROJ_SUBTILES = 4
S5_TIME_TILE = 64
S5_POST_SPLIT = 4
GDN_TILE = 128
GDN_BATCH_BLOCK = 4
GDN_PROJ_GROUP = 2
GDN_PREP_ROWS = 64
VMEM_LIMIT = 48 * 1024 * 1024


def _dot(a, b):
    return jnp.dot(a.astype(BF16), b.astype(BF16), preferred_element_type=F32)


def _dot_nt(a, b):
    return lax.dot_general(a.astype(BF16), b.astype(BF16), (((1,), (1,)), ((), ())),
                           preferred_element_type=F32)


def _rms(x, g):
    return x * lax.rsqrt(jnp.mean(x * x, axis=-1, keepdims=True) + NORM_EPS) * g


def _s5_prep_kernel(lr_ref, li_ref, ldt_ref, br_ref, bi_ref, abr_ref, abi_ref, bbr_ref, bbi_ref):
    lr = lr_ref[...]
    li = li_ref[...]
    dt = jnp.exp(ldt_ref[...])
    mag = jnp.exp(lr * dt)
    ang = li * dt
    ab_r = mag * jnp.cos(ang)
    ab_i = mag * jnp.sin(ang)
    den = lr * lr + li * li
    nr = ab_r - 1.0
    ni = ab_i
    cr = (nr * lr + ni * li) / den
    ci = (ni * lr - nr * li) / den
    abr_ref[...] = ab_r
    abi_ref[...] = ab_i
    br = br_ref[...]
    bi = bi_ref[...]
    bbr_ref[...] = cr * br - ci * bi
    bbi_ref[...] = cr * bi + ci * br


def _s5_prep(a_re, a_im, log_dt, b_re, b_im):
    g, p = a_re.shape
    h = b_re.shape[-1]
    n = g * p
    row = lambda t: t.reshape(1, n).astype(F32)
    chan = lambda t: t.astype(F32).transpose(2, 0, 1).reshape(h, n)
    ldt = jnp.broadcast_to(log_dt.astype(F32)[:, None], (g, p))
    return pl.pallas_call(
        _s5_prep_kernel,
        out_shape=(jax.ShapeDtypeStruct((1, n), F32), jax.ShapeDtypeStruct((1, n), F32),
                   jax.ShapeDtypeStruct((h, n), F32), jax.ShapeDtypeStruct((h, n), F32)),
        name="s5_prep",
    )(row(a_re), row(a_im), row(ldt), chan(b_re), chan(b_im))


def _s5_layout(bb_r, bb_i, c_re, c_im):
    g, h, p = c_re.shape
    gpb = SSM_LANE_BLOCK // h
    nb = g // gpb
    eye = jnp.eye(gpb, dtype=F32)

    def in_w(bb):
        t = bb.reshape(h, nb, gpb, p)
        return jnp.einsum('hjkp,gk->jghkp', t, eye).reshape(nb, gpb * h, gpb * p)

    def out_w(c):
        t = c.astype(F32).reshape(nb, gpb, h, p)
        return jnp.einsum('jghp,gk->jgpkh', t, eye).reshape(nb, gpb * p, gpb * h)

    bw = jnp.concatenate([in_w(bb_r), in_w(bb_i)], axis=2).astype(BF16)
    cw = jnp.concatenate([out_w(c_re), -out_w(c_im)], axis=1).astype(BF16)
    return bw, cw


def _s5_kernel(u_hbm, bw_ref, cw_ref, ar_ref, ai_ref, d_ref, wglu_ref, bglu_ref, y_hbm,
               ubuf, ybuf, sem_in, sem_out, carry_ref, *blk_refs, tt, bsz, nb, nsteps):
    sb = SSM_STATE_BLOCK
    rows = tt * bsz
    t = pl.program_id(0)
    slot = t % 2
    bu_refs, s_refs = blk_refs[:nb], blk_refs[nb:]

    def in_copies(step, sl):
        return [pltpu.make_async_copy(u_hbm.at[b, pl.ds(step * tt, tt), :], ubuf.at[sl, :, b, :], sem_in.at[sl])
                for b in range(bsz)]

    def out_copies(step, sl):
        return [pltpu.make_async_copy(ybuf.at[sl, :, b, :], y_hbm.at[b, pl.ds(step * tt, tt), :], sem_out.at[sl])
                for b in range(bsz)]

    @pl.when(t == 0)
    def _():
        carry_ref[...] = jnp.zeros_like(carry_ref)
        for c in in_copies(0, 0):
            c.start()

    @pl.when(t >= 2)
    def _():
        for c in out_copies(t - 2, slot):
            c.wait()

    nxt = jnp.minimum(t + 1, nsteps - 1)
    for c in in_copies(nxt, 1 - slot):
        c.start()
    for c in in_copies(t, slot):
        c.wait()

    u = ubuf[slot].reshape(rows, nb * SSM_LANE_BLOCK)
    u16 = u.astype(BF16)
    th = tt // S5_POST_SPLIT

    def input_matmuls(part):
        r = slice(part * th * bsz, (part + 1) * th * bsz)
        for j in range(nb):
            bu_refs[j][r] = jnp.dot(u16[r, SSM_LANE_BLOCK * j:SSM_LANE_BLOCK * (j + 1)], bw_ref[j],
                                    preferred_element_type=F32)

    input_matmuls(0)
    a_re = [jnp.broadcast_to(ar_ref[:, sb * j:sb * (j + 1)], (bsz, sb)) for j in range(nb)]
    a_im = [jnp.broadcast_to(ai_ref[:, sb * j:sb * (j + 1)], (bsz, sb)) for j in range(nb)]
    state = [(carry_ref[:, 2 * sb * j:2 * sb * j + sb], carry_ref[:, 2 * sb * j + sb:2 * sb * (j + 1)])
             for j in range(nb)]
    for part in range(S5_POST_SPLIT):
        if part + 1 < S5_POST_SPLIT:
            input_matmuls(part + 1)
        for j in range(nb):
            s_r, s_i = state[j]
            for i in range(part * th, (part + 1) * th):
                r = slice(i * bsz, (i + 1) * bsz)
                s_r, s_i = (a_re[j] * s_r - a_im[j] * s_i + bu_refs[j][r, :sb],
                            a_re[j] * s_i + a_im[j] * s_r + bu_refs[j][r, sb:])
                s_refs[j][r, :sb] = s_r.astype(BF16)
                s_refs[j][r, sb:] = s_i.astype(BF16)
            state[j] = (s_r, s_i)
        r = slice(part * th * bsz, (part + 1) * th * bsz)
        y = jnp.concatenate(
            [jnp.dot(s_refs[j][r], cw_ref[j], preferred_element_type=F32)
             for j in range(nb)], axis=1)
        y = y + d_ref[...] * u[r]
        y = jax.nn.gelu(y)
        y = y * jax.nn.sigmoid(_dot(y, wglu_ref[...]) + bglu_ref[...])
        ybuf[slot, part * th:(part + 1) * th] = y.reshape(th, bsz, y.shape[-1])
    for j in range(nb):
        carry_ref[:, 2 * sb * j:2 * sb * j + sb] = state[j][0]
        carry_ref[:, 2 * sb * j + sb:2 * sb * (j + 1)] = state[j][1]

    for c in out_copies(t, slot):
        c.start()

    @pl.when(t == nsteps - 1)
    def _():
        for c in in_copies(nxt, 1 - slot):
            c.wait()
        for c in out_copies(t, slot):
            c.wait()
        if nsteps >= 2:
            for c in out_copies(t - 1, 1 - slot):
                c.wait()


def _s5(us, bw, cw, a_r, a_i, d_vec, w_glu, b_glu):
    bsz, seq, d_ssm = us.shape
    nb = bw.shape[0]
    tt = min(S5_TIME_TILE, seq)
    rows = tt * bsz
    nsteps = seq // tt
    n_state = 2 * SSM_STATE_BLOCK * nb
    full = lambda shape: pl.BlockSpec(shape, lambda t: (0,) * len(shape))
    hbm = pl.BlockSpec(memory_space=pl.ANY)
    return pl.pallas_call(
        functools.partial(_s5_kernel, tt=tt, bsz=bsz, nb=nb, nsteps=nsteps),
        grid=(nsteps,),
        in_specs=[hbm, full(bw.shape), full(cw.shape), full(a_r.shape), full(a_i.shape),
                  full(d_vec.shape), full(w_glu.shape), full(b_glu.shape)],
        out_specs=hbm,
        out_shape=jax.ShapeDtypeStruct((bsz, seq, d_ssm), F32),
        scratch_shapes=[pltpu.VMEM((2, tt, bsz, d_ssm), F32),
                        pltpu.VMEM((2, tt, bsz, d_ssm), F32),
                        pltpu.SemaphoreType.DMA((2,)),
                        pltpu.SemaphoreType.DMA((2,)),
                        pltpu.VMEM((bsz, n_state), F32)]
        + [pltpu.VMEM((rows, 2 * SSM_STATE_BLOCK), F32) for _ in range(nb)]
        + [pltpu.VMEM((rows, 2 * SSM_STATE_BLOCK), BF16) for _ in range(nb)],
        compiler_params=pltpu.CompilerParams(dimension_semantics=("arbitrary",),
                                             vmem_limit_bytes=VMEM_LIMIT),
        name="s5",
    )(us, bw, cw, a_r, a_i, d_vec, w_glu, b_glu)


def _chunk_sums(g, sum_ref):
    n = g.shape[0]
    hi = g.astype(BF16).astype(F32)
    mid = (g - hi).astype(BF16).astype(F32)
    lo = g - hi - mid
    parts = jnp.concatenate([hi, mid, lo], axis=0).astype(BF16)
    s = jnp.dot(parts, sum_ref[...], preferred_element_type=F32)
    s = s[:n] + s[n:2 * n] + s[2 * n:]
    tc = g.shape[1]
    return s[:, :tc], s[:, tc:]


_M_CAUSAL, _M_STRICT, _M_EYE, _M_PAIR = 0, 1, 2, 3


def _gdn_fill_masks(mask_ref, lvl_ref, sum_ref, tc):
    ri = lax.broadcasted_iota(jnp.int32, (tc, tc), 0)
    ci = lax.broadcasted_iota(jnp.int32, (tc, tc), 1)
    same = (ri // DN_CHUNK) == (ci // DN_CHUNK)
    mask_ref[_M_CAUSAL] = (same & (ri >= ci)).astype(F32)
    mask_ref[_M_STRICT] = (same & (ri > ci)).astype(F32)
    mask_ref[_M_EYE] = (ri == ci).astype(F32)
    sum_ref[:, :tc] = (same & (ri <= ci)).astype(F32).astype(BF16)
    sum_ref[:, tc:] = (same & (ri > ci)).astype(F32).astype(BF16)
    s = 1
    lvl = 0
    while s < DN_CHUNK:
        m = (((ri // (2 * s)) == (ci // (2 * s))) & (((ri // s) % 2) == 1) & (((ci // s) % 2) == 0)).astype(F32)
        if lvl == 0:
            mask_ref[_M_PAIR] = m
        else:
            lvl_ref[lvl - 1] = m.astype(BF16)
        s *= 2
        lvl += 1


def _gdn_prep(bb, qkv_ref, ba_ref, alog_ref, dtb_ref, mask_ref, sum_ref, tc, heads):
    hd = DN_HEAD_DIM
    d_dn = heads * hd
    ba = ba_ref[bb]
    beta_r = jax.nn.sigmoid(ba)
    g_r = -jnp.exp(alog_ref[...]) * jax.nn.softplus(ba + dtb_ref[...])
    gc_r, gs_r = _chunk_sums(g_r, sum_ref)
    egs_r = jnp.exp(gs_r)
    egl_r = jnp.exp(gc_r + gs_r)
    pad = jnp.zeros((LANES - 3 * 2 * heads, tc), F32)
    cols = jnp.concatenate([beta_r, gc_r, egl_r, pad], axis=0).T
    out = []
    for h in range(heads):
        q = qkv_ref[bb, :, hd * h:hd * (h + 1)]
        k = qkv_ref[bb, :, d_dn + hd * h:d_dn + hd * (h + 1)]
        v = qkv_ref[bb, :, 2 * d_dn + hd * h:2 * d_dn + hd * (h + 1)]
        beta_c = cols[:, h:h + 1]
        gc_c = cols[:, 3 * heads + h:3 * heads + h + 1]
        gc_row = gc_r[heads + h:heads + h + 1, :]
        eg_c = jnp.exp(gc_c)
        kb = k * beta_c
        k16 = k.astype(BF16)
        attn16, a16, t16 = [], [], []
        for r0 in range(0, tc, GDN_PREP_ROWS):
            r = slice(r0, r0 + GDN_PREP_ROWS)
            decay = jnp.exp(jnp.minimum(gc_c[r] - gc_row, 0.0))
            qk_kk = _dot_nt(jnp.concatenate([q[r], k[r]], axis=0), k16)
            a = qk_kk[GDN_PREP_ROWS:] * (beta_c[r] * decay) * mask_ref[_M_STRICT, r]
            attn16.append((qk_kk[:GDN_PREP_ROWS] * decay * mask_ref[_M_CAUSAL, r]).astype(BF16))
            a16.append(a.astype(BF16))
            t16.append((mask_ref[_M_EYE, r] - a * mask_ref[_M_PAIR, r]).astype(BF16))
        out.append(dict(
            egl_c=cols[:, 5 * heads + h:5 * heads + h + 1],
            qg16=(q * eg_c).astype(BF16),
            rhs16=jnp.concatenate([v * beta_c, kb * eg_c], axis=1).astype(BF16),
            kdec16=(k.T * egs_r[heads + h:heads + h + 1, :]).astype(BF16),
            attn16=jnp.concatenate(attn16, axis=0),
            a16=jnp.concatenate(a16, axis=0),
            t16=jnp.concatenate(t16, axis=0)))
    return out


def _gdn_body(units, zd_ref, ng_ref, y_ref, state_ref, lvl_ref, *, tc, heads):
    hd = DN_HEAD_DIM
    nchunk = tc // DN_CHUNK

    for lvl in range(lvl_ref.shape[0]):
        for u in units:
            u["x16"] = jnp.dot(u["a16"], u["t16"], preferred_element_type=F32).astype(BF16)
        for u in units:
            y16 = jnp.dot(u["t16"], u["x16"], preferred_element_type=F32).astype(BF16)
            u["t16"] = u["t16"] - y16 * lvl_ref[lvl]
    for u in units:
        sol = jnp.dot(u["t16"], u["rhs16"], preferred_element_type=F32)
        u["u"] = sol[:, :hd]
        u["w16"] = sol[:, hd:].astype(BF16)
        u["st"] = state_ref[u["bb"] * heads + u["h"]]
        u["outs"] = []

    for n in range(nchunk):
        r0 = n * DN_CHUNK
        r1 = r0 + DN_CHUNK
        for u in units:
            u["ws_qs"] = jnp.dot(jnp.concatenate([u["w16"][r0:r1], u["qg16"][r0:r1]], axis=0),
                                 u["st"].astype(BF16), preferred_element_type=F32)
        for u in units:
            v_new = (u["u"][r0:r1] - u["ws_qs"][:DN_CHUNK]).astype(BF16)
            u["av_kv"] = jnp.dot(jnp.concatenate([u["attn16"][r0:r1, r0:r1], u["kdec16"][:, r0:r1]], axis=0),
                                 v_new, preferred_element_type=F32)
        for u in units:
            u["outs"].append(u["ws_qs"][DN_CHUNK:] + u["av_kv"][:DN_CHUNK])
            u["st"] = u["st"] * u["egl_c"][r0:r1] + u["av_kv"][DN_CHUNK:]
    for u in units:
        bb, h = u["bb"], u["h"]
        state_ref[bb * heads + h] = u["st"]
        o = jnp.concatenate(u["outs"], axis=0)
        z = zd_ref[bb, :, hd * h:hd * (h + 1)]
        y_ref[bb, :, hd * h:hd * (h + 1)] = _rms(o, ng_ref[...]) * jax.nn.silu(z)


def _proj_gdn_kernel(x_ref, g_ref, wm_ref, wba_ref, cw_ref, cs_ref, alog_ref, dtb_ref, ng_ref,
                     us_ref, zs_ref, y_ref,
                     tail_ref, xpad_ref, qkv_ref, zd_ref, ba_ref, state_ref, mask_ref, lvl_ref, sum_ref,
                     *, d_ssm, d_dn, tc, heads, nbb):
    hd = DN_HEAD_DIM
    nslab = 3 * d_dn // LANES
    nrow = ba_ref.shape[1]

    @pl.when(pl.program_id(1) == 0)
    def _():
        tail_ref[...] = jnp.zeros_like(tail_ref)
        state_ref[...] = jnp.zeros_like(state_ref)
        _gdn_fill_masks(mask_ref, lvl_ref, sum_ref, tc)

    a = _rms(x_ref[...].reshape(nbb * tc, x_ref.shape[-1]), g_ref[...]).astype(BF16)
    others = [(us_ref, slice(0, d_ssm)), (zs_ref, slice(d_ssm, 2 * d_ssm)),
              (zd_ref, slice(2 * d_ssm + 3 * d_dn, 2 * d_ssm + 4 * d_dn))]

    def other_projection():
        ref, cols = others.pop(0)
        ref[...] = jnp.dot(a, wm_ref[:, cols], preferred_element_type=F32).reshape(ref.shape)

    units = []
    for bb in range(nbb):
        if bb % GDN_PROJ_GROUP == 0:
            grows = slice(bb * tc, (bb + GDN_PROJ_GROUP) * tc)
            qkv_g = jnp.dot(a[grows], wm_ref[:, 2 * d_ssm:2 * d_ssm + 3 * d_dn], preferred_element_type=F32)
            ba_g = jnp.dot(a[grows], wba_ref[...], preferred_element_type=F32)
            if others:
                other_projection()
        rows = slice((bb % GDN_PROJ_GROUP) * tc, (bb % GDN_PROJ_GROUP + 1) * tc)
        qkv = qkv_g[rows]
        ba_ref[bb] = ba_g[rows].T[:nrow]
        for s in range(nslab):
            lanes = slice(LANES * s, LANES * (s + 1))
            xpad_ref[bb, s, 0:8, :] = tail_ref[bb, s]
            xpad_ref[bb, s, 8:, :] = qkv[:, lanes]
            tail_ref[bb, s] = xpad_ref[bb, s, tc:, :]
            for r0 in range(0, tc, GDN_PREP_ROWS):
                acc = cw_ref[DN_CONV - 1:DN_CONV, lanes] * xpad_ref[bb, s, 8 + r0:8 + r0 + GDN_PREP_ROWS, :]
                for j in range(DN_CONV - 1):
                    acc = acc + cw_ref[j:j + 1, lanes] * xpad_ref[bb, s, pl.ds(cs_ref[j] + r0, GDN_PREP_ROWS), :]
                t = jax.nn.silu(acc)
                if LANES * s < 2 * d_dn:
                    t = t * lax.rsqrt(jnp.sum(t * t, axis=-1, keepdims=True) + NORM_EPS)
                    if LANES * s < d_dn:
                        t = t * (hd ** -0.5)
                qkv_ref[bb, r0:r0 + GDN_PREP_ROWS, lanes] = t

        for h, d in enumerate(_gdn_prep(bb, qkv_ref, ba_ref, alog_ref, dtb_ref, mask_ref, sum_ref, tc, heads)):
            units.append(dict(d, bb=bb, h=h))

    while others:
        other_projection()
    _gdn_body(units, zd_ref, ng_ref, y_ref, state_ref, lvl_ref, tc=tc, heads=heads)


def _proj_gdn(x, g, w_main, w_ba, conv_w, a_log, dt_bias, norm_g, d_ssm, d_dn):
    bsz, seq, dm = x.shape
    heads = d_dn // DN_HEAD_DIM
    tc = min(GDN_TILE, seq)
    nbb = GDN_BATCH_BLOCK
    nrow = 2 * heads
    nslab = 3 * d_dn // LANES
    assert DN_CHUNK == DN_HEAD_DIM and tc % DN_CHUNK == 0 and bsz % nbb == 0
    n_lvl = DN_CHUNK.bit_length() - 2
    col = lambda t: jnp.concatenate([jnp.zeros((heads,), F32), t.astype(F32)]).reshape(nrow, 1)
    full = lambda shape: pl.BlockSpec(shape, lambda b, l: (0,) * len(shape))
    tok = lambda c: pl.BlockSpec((nbb, tc, c), lambda b, l: (b, l, 0))
    conv_starts = 8 - (DN_CONV - 1) + jnp.arange(DN_CONV - 1, dtype=jnp.int32)
    return pl.pallas_call(
        functools.partial(_proj_gdn_kernel, d_ssm=d_ssm, d_dn=d_dn, tc=tc, heads=heads, nbb=nbb),
        grid=(bsz // nbb, seq // tc),
        in_specs=[tok(dm), full((1, dm)), full(w_main.shape), full(w_ba.shape), full(conv_w.shape),
                  pl.BlockSpec(memory_space=pltpu.SMEM),
                  full((nrow, 1)), full((nrow, 1)), full((1, DN_HEAD_DIM))],
        out_specs=[tok(d_ssm), tok(d_ssm), tok(d_dn)],
        out_shape=(jax.ShapeDtypeStruct((bsz, seq, d_ssm), F32),
                   jax.ShapeDtypeStruct((bsz, seq, d_ssm), F32),
                   jax.ShapeDtypeStruct((bsz, seq, d_dn), F32)),
        scratch_shapes=[pltpu.VMEM((nbb, nslab, 8, LANES), F32),
                        pltpu.VMEM((nbb, nslab, 8 + tc, LANES), F32),
                        pltpu.VMEM((nbb, tc, 3 * d_dn), F32),
                        pltpu.VMEM((nbb, tc, d_dn), F32),
                        pltpu.VMEM((nbb, nrow, tc), F32),
                        pltpu.VMEM((nbb * heads, DN_HEAD_DIM, DN_HEAD_DIM), F32),
                        pltpu.VMEM((_M_PAIR + 1, tc, tc), F32),
                        pltpu.VMEM((n_lvl, tc, tc), BF16),
                        pltpu.VMEM((tc, 2 * tc), BF16)],
        compiler_params=pltpu.CompilerParams(dimension_semantics=("arbitrary", "arbitrary"),
                                             vmem_limit_bytes=VMEM_LIMIT),
        name="proj_gdn",
    )(x, g, w_main, w_ba, conv_w, conv_starts, col(a_log), col(dt_bias),
      norm_g.astype(F32).reshape(1, DN_HEAD_DIM))


def _out_kernel(ys_ref, zs_ref, yd_ref, x_ref, p_ref, wos_ref, wod_ref, wp_ref, pg_ref, wg_ref, fg_ref, o_ref,
                *, final):
    tm = x_ref.shape[1]
    rows = tm // PROJ_SUBTILES
    for s in range(PROJ_SUBTILES):
        r = slice(s * rows, (s + 1) * rows)
        ys = ys_ref[0, r] * jax.nn.silu(zs_ref[0, r])
        h = x_ref[0, r] + _dot(ys, wos_ref[...]) + _dot(yd_ref[0, r], wod_ref[...])
        e = _rms(_dot(p_ref[0, r], wp_ref[...]), pg_ref[...])
        h = h + jax.nn.sigmoid(_dot(h, wg_ref[...])) * e
        o_ref[0, r] = _rms(h, fg_ref[...]) if final else h


def _out_proj(ys, zs, yd, x, p, w_out_s, w_out_d, w_ple, ple_g, w_gate, final_g, final):
    bsz, seq, dm = x.shape
    d_ssm = w_out_s.shape[0]
    d_dn = w_out_d.shape[0]
    tm = min(PROJ_TILE, seq)
    full = lambda shape: pl.BlockSpec(shape, lambda b, l: (0,) * len(shape))
    tok = lambda c: pl.BlockSpec((1, tm, c), lambda b, l: (b, l, 0))
    return pl.pallas_call(
        functools.partial(_out_kernel, final=final),
        grid=(bsz, seq // tm),
        in_specs=[tok(d_ssm), tok(d_ssm), tok(d_dn), tok(dm), tok(p.shape[-1]),
                  full(w_out_s.shape), full(w_out_d.shape), full(w_ple.shape), full((1, dm)),
                  full(w_gate.shape), full((1, dm))],
        out_specs=tok(dm),
        out_shape=jax.ShapeDtypeStruct((bsz, seq, dm), F32),
        compiler_params=pltpu.CompilerParams(dimension_semantics=("arbitrary", "arbitrary"),
                                             vmem_limit_bytes=VMEM_LIMIT),
        name="out_proj",
    )(ys, zs, yd, x, p, w_out_s, w_out_d, w_ple, ple_g, w_gate, final_g)


def kernel(x, p, norm_mix_g, w_in, ssm_A_re, ssm_A_im, ssm_B_re, ssm_B_im, ssm_C_re, ssm_C_im, ssm_D, ssm_log_dt, ssm_w_glu, ssm_b_glu, dn_conv_w, dn_A_log, dn_dt_bias, dn_norm_g, w_out, w_ple_proj, ple_norm_g, w_ple_gate, final_norm_g):
    bsz, seq, dm = x.shape
    depth = w_in.shape[0]
    d_ssm = ssm_D.shape[-1]
    heads = dn_A_log.shape[-1]
    d_dn = heads * DN_HEAD_DIM
    n_main = 2 * d_ssm + 4 * d_dn
    row = lambda t: t.astype(F32).reshape(1, -1)
    h = x
    for i in range(depth):
        w_main = w_in[i, :, :n_main].astype(BF16)
        w_ba = jnp.pad(w_in[i, :, n_main:], ((0, 0), (0, LANES - 2 * heads))).astype(BF16)
        us, zs, yd = _proj_gdn(h, row(norm_mix_g[i]), w_main, w_ba, dn_conv_w[i].astype(F32),
                               dn_A_log[i], dn_dt_bias[i], dn_norm_g[i], d_ssm, d_dn)

        a_r, a_i, bb_r, bb_i = _s5_prep(ssm_A_re[i], ssm_A_im[i], ssm_log_dt[i], ssm_B_re[i], ssm_B_im[i])
        bw, cw = _s5_layout(bb_r, bb_i, ssm_C_re[i], ssm_C_im[i])
        ys = _s5(us, bw, cw, a_r, a_i, row(ssm_D[i]), ssm_w_glu[i].astype(BF16), row(ssm_b_glu[i]))

        h = _out_proj(ys, zs, yd, h, p[i],
                      w_out[i, :d_ssm].astype(BF16), w_out[i, d_ssm:].astype(BF16),
                      w_ple_proj[i].astype(BF16), row(ple_norm_g[i]), w_ple_gate[i].astype(BF16),
                      row(final_norm_g), final=(i == depth - 1))
    return h
```

```python
import functools

import jax
import jax.numpy as jnp
from jax import lax
from jax.experimental import pallas as pl
from jax.experimental.pallas import tpu as pltpu

F32 = jnp.float32
BF16 = jnp.bfloat16

NORM_EPS = 1e-6
SSM_GROUP = 16
SSM_STATE = 64
DN_HEAD_DIM = 128
DN_CONV = 4
DN_CHUNK = 128

LANES = 128
SSM_LANE_BLOCK = 128
SSM_STATE_BLOCK = (SSM_LANE_BLOCK // SSM_GROUP) * SSM_STATE

PROJ_TILE = 1024
PROJ_SUBTILES = 4
S5_TIME_TILE = 64
S5_POST_SPLIT = 4
GDN_TILE = 128
GDN_BATCH_BLOCK = 4
GDN_PROJ_GROUP = 2
GDN_PREP_ROWS = 64
VMEM_LIMIT = 48 * 1024 * 1024


def _dot(a, b):
    return jnp.dot(a.astype(BF16), b.astype(BF16), preferred_element_type=F32)


def _dot_nt(a, b):
    return lax.dot_general(a.astype(BF16), b.astype(BF16), (((1,), (1,)), ((), ())),
                           preferred_element_type=F32)


def _rms(x, g):
    return x * lax.rsqrt(jnp.mean(x * x, axis=-1, keepdims=True) + NORM_EPS) * g


def _s5_prep_kernel(lr_ref, li_ref, ldt_ref, br_ref, bi_ref, abr_ref, abi_ref, bbr_ref, bbi_ref):
    lr = lr_ref[...]
    li = li_ref[...]
    dt = jnp.exp(ldt_ref[...])
    mag = jnp.exp(lr * dt)
    ang = li * dt
    ab_r = mag * jnp.cos(ang)
    ab_i = mag * jnp.sin(ang)
    den = lr * lr + li * li
    nr = ab_r - 1.0
    ni = ab_i
    cr = (nr * lr + ni * li) / den
    ci = (ni * lr - nr * li) / den
    abr_ref[...] = ab_r
    abi_ref[...] = ab_i
    br = br_ref[...]
    bi = bi_ref[...]
    bbr_ref[...] = cr * br - ci * bi
    bbi_ref[...] = cr * bi + ci * br


def _s5_prep(a_re, a_im, log_dt, b_re, b_im):
    g, p = a_re.shape
    h = b_re.shape[-1]
    n = g * p
    row = lambda t: t.reshape(1, n).astype(F32)
    chan = lambda t: t.astype(F32).transpose(2, 0, 1).reshape(h, n)
    ldt = jnp.broadcast_to(log_dt.astype(F32)[:, None], (g, p))
    return pl.pallas_call(
        _s5_prep_kernel,
        out_shape=(jax.ShapeDtypeStruct((1, n), F32), jax.ShapeDtypeStruct((1, n), F32),
                   jax.ShapeDtypeStruct((h, n), F32), jax.ShapeDtypeStruct((h, n), F32)),
        name="s5_prep",
    )(row(a_re), row(a_im), row(ldt), chan(b_re), chan(b_im))


def _s5_layout(bb_r, bb_i, c_re, c_im):
    g, h, p = c_re.shape
    gpb = SSM_LANE_BLOCK // h
    nb = g // gpb
    eye = jnp.eye(gpb, dtype=F32)

    def in_w(bb):
        t = bb.reshape(h, nb, gpb, p)
        return jnp.einsum('hjkp,gk->jghkp', t, eye).reshape(nb, gpb * h, gpb * p)

    def out_w(c):
        t = c.astype(F32).reshape(nb, gpb, h, p)
        return jnp.einsum('jghp,gk->jgpkh', t, eye).reshape(nb, gpb * p, gpb * h)

    bw = jnp.concatenate([in_w(bb_r), in_w(bb_i)], axis=2).astype(BF16)
    cw = jnp.concatenate([out_w(c_re), -out_w(c_im)], axis=1).astype(BF16)
    return bw, cw


def _s5_kernel(u_hbm, z_hbm, bw_ref, cw_ref, ar_ref, ai_ref, d_ref, wglu_ref, bglu_ref, y_hbm,
               ubuf, zbuf, ybuf, sem_in, sem_out, carry_ref, *blk_refs, tt, bsz, nb, nsteps):
    sb = SSM_STATE_BLOCK
    rows = tt * bsz
    t = pl.program_id(0)
    slot = t % 2
    bu_refs, s_refs = blk_refs[:nb], blk_refs[nb:]

    def in_copies(step, sl):
        cps = []
        for b in range(bsz):
            cps.append(pltpu.make_async_copy(u_hbm.at[b, pl.ds(step * tt, tt), :], ubuf.at[sl, :, b, :],
                                             sem_in.at[sl, 0]))
            cps.append(pltpu.make_async_copy(z_hbm.at[b, pl.ds(step * tt, tt), :], zbuf.at[sl, :, b, :],
                                             sem_in.at[sl, 1]))
        return cps

    def out_copies(step, sl):
        return [pltpu.make_async_copy(ybuf.at[sl, :, b, :], y_hbm.at[b, pl.ds(step * tt, tt), :], sem_out.at[sl])
                for b in range(bsz)]

    @pl.when(t == 0)
    def _():
        carry_ref[...] = jnp.zeros_like(carry_ref)
        for c in in_copies(0, 0):
            c.start()

    @pl.when(t >= 2)
    def _():
        for c in out_copies(t - 2, slot):
            c.wait()

    nxt = jnp.minimum(t + 1, nsteps - 1)
    for n, c in enumerate(in_copies(nxt, 1 - slot)):
        c.start(priority=n % 2)
    for c in in_copies(t, slot):
        c.wait()

    u = ubuf[slot].reshape(rows, nb * SSM_LANE_BLOCK)
    z = zbuf[slot].reshape(rows, nb * SSM_LANE_BLOCK)
    u16 = u.astype(BF16)
    th = tt // S5_POST_SPLIT

    def input_matmuls(part):
        r = slice(part * th * bsz, (part + 1) * th * bsz)
        for j in range(nb):
            bu_refs[j][r] = jnp.dot(u16[r, SSM_LANE_BLOCK * j:SSM_LANE_BLOCK * (j + 1)], bw_ref[j],
                                    preferred_element_type=F32)

    input_matmuls(0)
    a_re = [jnp.broadcast_to(ar_ref[:, sb * j:sb * (j + 1)], (bsz, sb)) for j in range(nb)]
    a_im = [jnp.broadcast_to(ai_ref[:, sb * j:sb * (j + 1)], (bsz, sb)) for j in range(nb)]
    state = [(carry_ref[:, 2 * sb * j:2 * sb * j + sb], carry_ref[:, 2 * sb * j + sb:2 * sb * (j + 1)])
             for j in range(nb)]
    for part in range(S5_POST_SPLIT):
        if part + 1 < S5_POST_SPLIT:
            input_matmuls(part + 1)
        for j in range(nb):
            s_r, s_i = state[j]
            for i in range(part * th, (part + 1) * th):
                r = slice(i * bsz, (i + 1) * bsz)
                s_r, s_i = (a_re[j] * s_r - a_im[j] * s_i + bu_refs[j][r, :sb],
                            a_re[j] * s_i + a_im[j] * s_r + bu_refs[j][r, sb:])
                s_refs[j][r, :sb] = s_r.astype(BF16)
                s_refs[j][r, sb:] = s_i.astype(BF16)
            state[j] = (s_r, s_i)
        r = slice(part * th * bsz, (part + 1) * th * bsz)
        y = jnp.concatenate(
            [jnp.dot(s_refs[j][r], cw_ref[j], preferred_element_type=F32)
             for j in range(nb)], axis=1)
        y = y + d_ref[...] * u[r]
        y = jax.nn.gelu(y)
        y = y * jax.nn.sigmoid(_dot(y, wglu_ref[...]) + bglu_ref[...])
        y = y * jax.nn.silu(z[r])
        ybuf[slot, part * th:(part + 1) * th] = y.reshape(th, bsz, y.shape[-1])
    for j in range(nb):
        carry_ref[:, 2 * sb * j:2 * sb * j + sb] = state[j][0]
        carry_ref[:, 2 * sb * j + sb:2 * sb * (j + 1)] = state[j][1]

    for n, c in enumerate(out_copies(t, slot)):
        c.start(priority=n % 2)

    @pl.when(t == nsteps - 1)
    def _():
        for c in in_copies(nxt, 1 - slot):
            c.wait()
        for c in out_copies(t, slot):
            c.wait()
        if nsteps >= 2:
            for c in out_copies(t - 1, 1 - slot):
                c.wait()


def _s5(us, zs, bw, cw, a_r, a_i, d_vec, w_glu, b_glu):
    bsz, seq, d_ssm = us.shape
    nb = bw.shape[0]
    tt = min(S5_TIME_TILE, seq)
    rows = tt * bsz
    nsteps = seq // tt
    n_state = 2 * SSM_STATE_BLOCK * nb
    full = lambda shape: pl.BlockSpec(shape, lambda t: (0,) * len(shape))
    hbm = pl.BlockSpec(memory_space=pl.ANY)
    return pl.pallas_call(
        functools.partial(_s5_kernel, tt=tt, bsz=bsz, nb=nb, nsteps=nsteps),
        grid=(nsteps,),
        in_specs=[hbm, hbm, full(bw.shape), full(cw.shape), full(a_r.shape), full(a_i.shape),
                  full(d_vec.shape), full(w_glu.shape), full(b_glu.shape)],
        out_specs=hbm,
        out_shape=jax.ShapeDtypeStruct((bsz, seq, d_ssm), F32),
        scratch_shapes=[pltpu.VMEM((2, tt, bsz, d_ssm), F32),
                        pltpu.VMEM((2, tt, bsz, d_ssm), F32),
                        pltpu.VMEM((2, tt, bsz, d_ssm), F32),
                        pltpu.SemaphoreType.DMA((2, 2)),
                        pltpu.SemaphoreType.DMA((2,)),
                        pltpu.VMEM((bsz, n_state), F32)]
        + [pltpu.VMEM((rows, 2 * SSM_STATE_BLOCK), F32) for _ in range(nb)]
        + [pltpu.VMEM((rows, 2 * SSM_STATE_BLOCK), BF16) for _ in range(nb)],
        compiler_params=pltpu.CompilerParams(dimension_semantics=("arbitrary",),
                                             vmem_limit_bytes=VMEM_LIMIT),
        name="s5",
    )(us, zs, bw, cw, a_r, a_i, d_vec, w_glu, b_glu)


def _chunk_sums(g, sum_ref):
    n = g.shape[0]
    hi = g.astype(BF16).astype(F32)
    mid = (g - hi).astype(BF16).astype(F32)
    lo = g - hi - mid
    parts = jnp.concatenate([hi, mid, lo], axis=0).astype(BF16)
    s = jnp.dot(parts, sum_ref[...], preferred_element_type=F32)
    s = s[:n] + s[n:2 * n] + s[2 * n:]
    tc = g.shape[1]
    return s[:, :tc], s[:, tc:]


_M_CAUSAL, _M_STRICT, _M_EYE, _M_PAIR = 0, 1, 2, 3


def _gdn_fill_masks(mask_ref, lvl_ref, sum_ref, tc):
    ri = lax.broadcasted_iota(jnp.int32, (tc, tc), 0)
    ci = lax.broadcasted_iota(jnp.int32, (tc, tc), 1)
    same = (ri // DN_CHUNK) == (ci // DN_CHUNK)
    mask_ref[_M_CAUSAL] = (same & (ri >= ci)).astype(F32)
    mask_ref[_M_STRICT] = (same & (ri > ci)).astype(F32)
    mask_ref[_M_EYE] = (ri == ci).astype(F32)
    sum_ref[:, :tc] = (same & (ri <= ci)).astype(F32).astype(BF16)
    sum_ref[:, tc:] = (same & (ri > ci)).astype(F32).astype(BF16)
    s = 1
    lvl = 0
    while s < DN_CHUNK:
        m = (((ri // (2 * s)) == (ci // (2 * s))) & (((ri // s) % 2) == 1) & (((ci // s) % 2) == 0)).astype(F32)
        if lvl == 0:
            mask_ref[_M_PAIR] = m
        else:
            lvl_ref[lvl - 1] = m.astype(BF16)
        s *= 2
        lvl += 1


def _gdn_prep(bb, qkv_ref, ba_ref, alog_ref, dtb_ref, mask_ref, sum_ref, tc, heads):
    hd = DN_HEAD_DIM
    d_dn = heads * hd
    ba = ba_ref[bb]
    beta_r = jax.nn.sigmoid(ba)
    g_r = -jnp.exp(alog_ref[...]) * jax.nn.softplus(ba + dtb_ref[...])
    gc_r, gs_r = _chunk_sums(g_r, sum_ref)
    egs_r = jnp.exp(gs_r)
    egl_r = jnp.exp(gc_r + gs_r)
    pad = jnp.zeros((LANES - 3 * 2 * heads, tc), F32)
    cols = jnp.concatenate([beta_r, gc_r, egl_r, pad], axis=0).T
    out = []
    for h in range(heads):
        q = qkv_ref[bb, :, hd * h:hd * (h + 1)]
        k = qkv_ref[bb, :, d_dn + hd * h:d_dn + hd * (h + 1)]
        v = qkv_ref[bb, :, 2 * d_dn + hd * h:2 * d_dn + hd * (h + 1)]
        beta_c = cols[:, h:h + 1]
        gc_c = cols[:, 3 * heads + h:3 * heads + h + 1]
        gc_row = gc_r[heads + h:heads + h + 1, :]
        eg_c = jnp.exp(gc_c)
        kb = k * beta_c
        k16 = k.astype(BF16)
        attn16, a16, t16 = [], [], []
        for r0 in range(0, tc, GDN_PREP_ROWS):
            r = slice(r0, r0 + GDN_PREP_ROWS)
            decay = jnp.exp(jnp.minimum(gc_c[r] - gc_row, 0.0))
            qk_kk = _dot_nt(jnp.concatenate([q[r], k[r]], axis=0), k16)
            a = qk_kk[GDN_PREP_ROWS:] * (beta_c[r] * decay) * mask_ref[_M_STRICT, r]
            attn16.append((qk_kk[:GDN_PREP_ROWS] * decay * mask_ref[_M_CAUSAL, r]).astype(BF16))
            a16.append(a.astype(BF16))
            t16.append((mask_ref[_M_EYE, r] - a * mask_ref[_M_PAIR, r]).astype(BF16))
        out.append(dict(
            egl_c=cols[:, 5 * heads + h:5 * heads + h + 1],
            qg16=(q * eg_c).astype(BF16),
            rhs16=jnp.concatenate([v * beta_c, kb * eg_c], axis=1).astype(BF16),
            kdec16=(k.T * egs_r[heads + h:heads + h + 1, :]).astype(BF16),
            attn16=jnp.concatenate(attn16, axis=0),
            a16=jnp.concatenate(a16, axis=0),
            t16=jnp.concatenate(t16, axis=0)))
    return out


def _gdn_body(units, zd_ref, ng_ref, y_ref, state_ref, lvl_ref, *, tc, heads):
    hd = DN_HEAD_DIM
    nchunk = tc // DN_CHUNK

    for lvl in range(lvl_ref.shape[0]):
        for u in units:
            u["x16"] = jnp.dot(u["a16"], u["t16"], preferred_element_type=F32).astype(BF16)
        for u in units:
            y16 = jnp.dot(u["t16"], u["x16"], preferred_element_type=F32).astype(BF16)
            u["t16"] = u["t16"] - y16 * lvl_ref[lvl]
    for u in units:
        sol = jnp.dot(u["t16"], u["rhs16"], preferred_element_type=F32)
        u["u"] = sol[:, :hd]
        u["w16"] = sol[:, hd:].astype(BF16)
        u["st"] = state_ref[u["bb"] * heads + u["h"]]
        u["outs"] = []

    for n in range(nchunk):
        r0 = n * DN_CHUNK
        r1 = r0 + DN_CHUNK
        for u in units:
            u["ws_qs"] = jnp.dot(jnp.concatenate([u["w16"][r0:r1], u["qg16"][r0:r1]], axis=0),
                                 u["st"].astype(BF16), preferred_element_type=F32)
        for u in units:
            v_new = (u["u"][r0:r1] - u["ws_qs"][:DN_CHUNK]).astype(BF16)
            u["av_kv"] = jnp.dot(jnp.concatenate([u["attn16"][r0:r1, r0:r1], u["kdec16"][:, r0:r1]], axis=0),
                                 v_new, preferred_element_type=F32)
        for u in units:
            u["outs"].append(u["ws_qs"][DN_CHUNK:] + u["av_kv"][:DN_CHUNK])
            u["st"] = u["st"] * u["egl_c"][r0:r1] + u["av_kv"][DN_CHUNK:]
    for u in units:
        bb, h = u["bb"], u["h"]
        state_ref[bb * heads + h] = u["st"]
        o = jnp.concatenate(u["outs"], axis=0)
        z = zd_ref[bb, :, hd * h:hd * (h + 1)]
        y_ref[bb, :, hd * h:hd * (h + 1)] = _rms(o, ng_ref[...]) * jax.nn.silu(z)


def _proj_gdn_kernel(x_ref, g_ref, wm_ref, wba_ref, cw_ref, cs_ref, alog_ref, dtb_ref, ng_ref,
                     us_ref, zs_ref, y_ref,
                     tail_ref, xpad_ref, qkv_ref, zd_ref, ba_ref, state_ref, mask_ref, lvl_ref, sum_ref,
                     *, d_ssm, d_dn, tc, heads, nbb):
    hd = DN_HEAD_DIM
    nslab = 3 * d_dn // LANES
    nrow = ba_ref.shape[1]

    @pl.when(pl.program_id(1) == 0)
    def _():
        tail_ref[...] = jnp.zeros_like(tail_ref)
        state_ref[...] = jnp.zeros_like(state_ref)
        _gdn_fill_masks(mask_ref, lvl_ref, sum_ref, tc)

    a = _rms(x_ref[...].reshape(nbb * tc, x_ref.shape[-1]), g_ref[...]).astype(BF16)
    others = [(us_ref, slice(0, d_ssm)), (zs_ref, slice(d_ssm, 2 * d_ssm)),
              (zd_ref, slice(2 * d_ssm + 3 * d_dn, 2 * d_ssm + 4 * d_dn))]

    def other_projection():
        ref, cols = others.pop(0)
        ref[...] = jnp.dot(a, wm_ref[:, cols], preferred_element_type=F32).reshape(ref.shape)

    units = []
    for bb in range(nbb):
        if bb % GDN_PROJ_GROUP == 0:
            grows = slice(bb * tc, (bb + GDN_PROJ_GROUP) * tc)
            qkv_g = jnp.dot(a[grows], wm_ref[:, 2 * d_ssm:2 * d_ssm + 3 * d_dn], preferred_element_type=F32)
            ba_g = jnp.dot(a[grows], wba_ref[...], preferred_element_type=F32)
            if others:
                other_projection()
        rows = slice((bb % GDN_PROJ_GROUP) * tc, (bb % GDN_PROJ_GROUP + 1) * tc)
        qkv = qkv_g[rows]
        ba_ref[bb] = ba_g[rows].T[:nrow]
        for s in range(nslab):
            lanes = slice(LANES * s, LANES * (s + 1))
            xpad_ref[bb, s, 0:8, :] = tail_ref[bb, s]
            xpad_ref[bb, s, 8:, :] = qkv[:, lanes]
            tail_ref[bb, s] = xpad_ref[bb, s, tc:, :]
            for r0 in range(0, tc, GDN_PREP_ROWS):
                acc = cw_ref[DN_CONV - 1:DN_CONV, lanes] * xpad_ref[bb, s, 8 + r0:8 + r0 + GDN_PREP_ROWS, :]
                for j in range(DN_CONV - 1):
                    acc = acc + cw_ref[j:j + 1, lanes] * xpad_ref[bb, s, pl.ds(cs_ref[j] + r0, GDN_PREP_ROWS), :]
                t = jax.nn.silu(acc)
                if LANES * s < 2 * d_dn:
                    t = t * lax.rsqrt(jnp.sum(t * t, axis=-1, keepdims=True) + NORM_EPS)
                    if LANES * s < d_dn:
                        t = t * (hd ** -0.5)
                qkv_ref[bb, r0:r0 + GDN_PREP_ROWS, lanes] = t

        for h, d in enumerate(_gdn_prep(bb, qkv_ref, ba_ref, alog_ref, dtb_ref, mask_ref, sum_ref, tc, heads)):
            units.append(dict(d, bb=bb, h=h))

    while others:
        other_projection()
    _gdn_body(units, zd_ref, ng_ref, y_ref, state_ref, lvl_ref, tc=tc, heads=heads)


def _proj_gdn(x, g, w_main, w_ba, conv_w, a_log, dt_bias, norm_g, d_ssm, d_dn):
    bsz, seq, dm = x.shape
    heads = d_dn // DN_HEAD_DIM
    tc = min(GDN_TILE, seq)
    nbb = GDN_BATCH_BLOCK
    nrow = 2 * heads
    nslab = 3 * d_dn // LANES
    assert DN_CHUNK == DN_HEAD_DIM and tc % DN_CHUNK == 0 and bsz % nbb == 0
    n_lvl = DN_CHUNK.bit_length() - 2
    col = lambda t: jnp.concatenate([jnp.zeros((heads,), F32), t.astype(F32)]).reshape(nrow, 1)
    full = lambda shape: pl.BlockSpec(shape, lambda b, l: (0,) * len(shape))
    tok = lambda c: pl.BlockSpec((nbb, tc, c), lambda b, l: (b, l, 0))
    conv_starts = 8 - (DN_CONV - 1) + jnp.arange(DN_CONV - 1, dtype=jnp.int32)
    return pl.pallas_call(
        functools.partial(_proj_gdn_kernel, d_ssm=d_ssm, d_dn=d_dn, tc=tc, heads=heads, nbb=nbb),
        grid=(bsz // nbb, seq // tc),
        in_specs=[tok(dm), full((1, dm)), full(w_main.shape), full(w_ba.shape), full(conv_w.shape),
                  pl.BlockSpec(memory_space=pltpu.SMEM),
                  full((nrow, 1)), full((nrow, 1)), full((1, DN_HEAD_DIM))],
        out_specs=[tok(d_ssm), tok(d_ssm), tok(d_dn)],
        out_shape=(jax.ShapeDtypeStruct((bsz, seq, d_ssm), F32),
                   jax.ShapeDtypeStruct((bsz, seq, d_ssm), F32),
                   jax.ShapeDtypeStruct((bsz, seq, d_dn), F32)),
        scratch_shapes=[pltpu.VMEM((nbb, nslab, 8, LANES), F32),
                        pltpu.VMEM((nbb, nslab, 8 + tc, LANES), F32),
                        pltpu.VMEM((nbb, tc, 3 * d_dn), F32),
                        pltpu.VMEM((nbb, tc, d_dn), F32),
                        pltpu.VMEM((nbb, nrow, tc), F32),
                        pltpu.VMEM((nbb * heads, DN_HEAD_DIM, DN_HEAD_DIM), F32),
                        pltpu.VMEM((_M_PAIR + 1, tc, tc), F32),
                        pltpu.VMEM((n_lvl, tc, tc), BF16),
                        pltpu.VMEM((tc, 2 * tc), BF16)],
        compiler_params=pltpu.CompilerParams(dimension_semantics=("arbitrary", "arbitrary"),
                                             vmem_limit_bytes=VMEM_LIMIT),
        name="proj_gdn",
    )(x, g, w_main, w_ba, conv_w, conv_starts, col(a_log), col(dt_bias),
      norm_g.astype(F32).reshape(1, DN_HEAD_DIM))


def _out_kernel(ys_ref, yd_ref, x_ref, p_ref, wos_ref, wod_ref, wp_ref, pg_ref, wg_ref, fg_ref, o_ref, *, final):
    tm = x_ref.shape[1]
    rows = tm // PROJ_SUBTILES
    for s in range(PROJ_SUBTILES):
        r = slice(s * rows, (s + 1) * rows)
        h = x_ref[0, r] + _dot(ys_ref[0, r], wos_ref[...]) + _dot(yd_ref[0, r], wod_ref[...])
        e = _rms(_dot(p_ref[0, r], wp_ref[...]), pg_ref[...])
        h = h + jax.nn.sigmoid(_dot(h, wg_ref[...])) * e
        o_ref[0, r] = _rms(h, fg_ref[...]) if final else h


def _out_proj(ys, yd, x, p, w_out_s, w_out_d, w_ple, ple_g, w_gate, final_g, final):
    bsz, seq, dm = x.shape
    d_ssm = w_out_s.shape[0]
    d_dn = w_out_d.shape[0]
    tm = min(PROJ_TILE, seq)
    full = lambda shape: pl.BlockSpec(shape, lambda b, l: (0,) * len(shape))
    tok = lambda c: pl.BlockSpec((1, tm, c), lambda b, l: (b, l, 0))
    return pl.pallas_call(
        functools.partial(_out_kernel, final=final),
        grid=(bsz, seq // tm),
        in_specs=[tok(d_ssm), tok(d_dn), tok(dm), tok(p.shape[-1]),
                  full(w_out_s.shape), full(w_out_d.shape), full(w_ple.shape), full((1, dm)),
                  full(w_gate.shape), full((1, dm))],
        out_specs=tok(dm),
        out_shape=jax.ShapeDtypeStruct((bsz, seq, dm), F32),
        compiler_params=pltpu.CompilerParams(dimension_semantics=("arbitrary", "arbitrary"),
                                             vmem_limit_bytes=VMEM_LIMIT),
        name="out_proj",
    )(ys, yd, x, p, w_out_s, w_out_d, w_ple, ple_g, w_gate, final_g)


def kernel(x, p, norm_mix_g, w_in, ssm_A_re, ssm_A_im, ssm_B_re, ssm_B_im, ssm_C_re, ssm_C_im, ssm_D, ssm_log_dt, ssm_w_glu, ssm_b_glu, dn_conv_w, dn_A_log, dn_dt_bias, dn_norm_g, w_out, w_ple_proj, ple_norm_g, w_ple_gate, final_norm_g):
    bsz, seq, dm = x.shape
    depth = w_in.shape[0]
    d_ssm = ssm_D.shape[-1]
    heads = dn_A_log.shape[-1]
    d_dn = heads * DN_HEAD_DIM
    n_main = 2 * d_ssm + 4 * d_dn
    row = lambda t: t.astype(F32).reshape(1, -1)
    h = x
    for i in range(depth):
        w_main = w_in[i, :, :n_main].astype(BF16)
        w_ba = jnp.pad(w_in[i, :, n_main:], ((0, 0), (0, LANES - 2 * heads))).astype(BF16)
        us, zs, yd = _proj_gdn(h, row(norm_mix_g[i]), w_main, w_ba, dn_conv_w[i].astype(F32),
                               dn_A_log[i], dn_dt_bias[i], dn_norm_g[i], d_ssm, d_dn)

        a_r, a_i, bb_r, bb_i = _s5_prep(ssm_A_re[i], ssm_A_im[i], ssm_log_dt[i], ssm_B_re[i], ssm_B_im[i])
        bw, cw = _s5_layout(bb_r, bb_i, ssm_C_re[i], ssm_C_im[i])
        ys = _s5(us, zs, bw, cw, a_r, a_i, row(ssm_D[i]), ssm_w_glu[i].astype(BF16), row(ssm_b_glu[i]))

        h = _out_proj(ys, yd, h, p[i],
                      w_out[i, :d_ssm].astype(BF16), w_out[i, d_ssm:].astype(BF16),
                      w_ple_proj[i].astype(BF16), row(ple_norm_g[i]), w_ple_gate[i].astype(BF16),
                      row(final_norm_g), final=(i == depth - 1))
    return h
```
